```python
import math
import jax, jax.numpy as jnp
from jax import lax
import numpy as np

D_MODEL = 1024
BATCH = 2
SEQ = 8192
DEPTH = 2

GRID_W = 64
CTX_LEN = 256
N_EVEN = (DEPTH + 1) // 2
N_ODD = DEPTH // 2
EPS = 1e-6
ROPE_BASE = 10000.0
BLOCK = 128

F_GROUPS = 4
F_GROUP_DIM = 128
F_WIDTH = F_GROUPS * F_GROUP_DIM

MLA_HEADS = 8
MLA_NOPE = 64
MLA_ROPE = 32
MLA_V = 64
MLA_Q_RANK = 384
MLA_KV_RANK = 256
MLA_WIDTH = MLA_HEADS * MLA_V
MLA_SCALE = 1.0 / math.sqrt(MLA_NOPE + MLA_ROPE)
EVEN_SPLITS = (F_WIDTH, F_WIDTH, MLA_Q_RANK, MLA_KV_RANK, MLA_ROPE, MLA_WIDTH)
EVEN_IN = sum(EVEN_SPLITS)
KV_COL0 = 2 * F_WIDTH + MLA_Q_RANK
KV_COL1 = KV_COL0 + MLA_KV_RANK + MLA_ROPE
EVEN_OUT = F_WIDTH + MLA_WIDTH

GQA_HEADS = 16
GQA_KV_HEADS = 4
GQA_GROUP = GQA_HEADS // GQA_KV_HEADS
GQA_HEAD_DIM = 64
WINDOW = 128
GQA_Q = GQA_HEADS * GQA_HEAD_DIM
GQA_KV = GQA_KV_HEADS * GQA_HEAD_DIM
GQA_WIDTH = GQA_Q
GQA_SCALE = 1.0 / math.sqrt(GQA_HEAD_DIM)
ODD_SPLITS = (GQA_Q, GQA_KV, GQA_KV, GQA_WIDTH)
ODD_IN = sum(ODD_SPLITS)

kernel_name = "hybrid_fnet_mla_swa_dit_block"


def split_cols(p, sizes):
    out = []
    start = 0
    for s in sizes:
        out.append(p[..., start:start + s])
        start += s
    return out


def rmsnorm(x, g):
    x32 = x.astype(jnp.float32)
    y = x32 * lax.rsqrt(jnp.mean(x32 * x32, axis=-1, keepdims=True) + EPS)
    return (y * g.astype(jnp.float32)).astype(x.dtype)


def axial_rope_tables(n, rot_dim):
    rows = n // GRID_W
    row = jnp.broadcast_to(jnp.arange(rows)[:, None], (rows, GRID_W)).reshape(-1).astype(jnp.float32)
    col = jnp.broadcast_to(jnp.arange(GRID_W)[None, :], (rows, GRID_W)).reshape(-1).astype(jnp.float32)
    nf = rot_dim // 4
    inv = ROPE_BASE ** (-jnp.arange(nf, dtype=jnp.float32) / nf)
    ang = jnp.concatenate([row[:, None] * inv, col[:, None] * inv], axis=-1)
    return jnp.cos(ang), jnp.sin(ang)


def apply_rope(x, cos, sin):
    extra = x.ndim - 3
    shp = (cos.shape[0],) + (1,) * extra + (cos.shape[1],)
    cs, sn = cos.reshape(shp), sin.reshape(shp)
    x32 = x.astype(jnp.float32)
    half = x.shape[-1] // 2
    x1, x2 = x32[..., :half], x32[..., half:]
    return jnp.concatenate([x1 * cs - x2 * sn, x2 * cs + x1 * sn], axis=-1).astype(x.dtype)


def dense_attention(q, k, v, scale):
    s = jnp.einsum('bqhd,bkhd->bhqk', q, k).astype(jnp.float32) * scale
    p = jax.nn.softmax(s, axis=-1).astype(v.dtype)
    return jnp.einsum('bhqk,bkhd->bqhd', p, v)


def blocked_dense_attention(q, k, v, scale):
    b, n, h, d = q.shape
    nb = n // BLOCK
    qb = q.reshape(b, nb, BLOCK, h, d).swapaxes(0, 1)
    o = lax.map(lambda qblk: dense_attention(qblk, k, v, scale), qb)
    return o.swapaxes(0, 1).reshape(b, n, h, v.shape[-1])


def fourier_mix(u):
    b, n, _ = u.shape
    f = jnp.fft.fft2(u.reshape(b, n, F_GROUPS, F_GROUP_DIM).astype(jnp.float32), axes=(1, 3), norm='ortho')
    return f.real.reshape(b, n, F_WIDTH).astype(u.dtype)


def mla_q(q_a, q_norm, w_qb, rope):
    b, n = q_a.shape[:2]
    q = (rmsnorm(q_a, q_norm) @ w_qb).reshape(b, n, MLA_HEADS, MLA_NOPE + MLA_ROPE)
    q_nope, q_pe = q[..., :MLA_NOPE], q[..., MLA_NOPE:]
    if rope is not None:
        q_pe = apply_rope(q_pe, *rope)
    return jnp.concatenate([q_nope, q_pe], axis=-1)


def mla_kv(kv_a, k_pe, kv_norm, w_kvb, rope):
    b, n = kv_a.shape[:2]
    kv = (rmsnorm(kv_a, kv_norm) @ w_kvb).reshape(b, n, MLA_HEADS, MLA_NOPE + MLA_V)
    k_nope, v = kv[..., :MLA_NOPE], kv[..., MLA_NOPE:]
    k_pe = k_pe[:, :, None, :]
    if rope is not None:
        k_pe = apply_rope(k_pe, *rope)
    k = jnp.concatenate([k_nope, jnp.broadcast_to(k_pe, (b, n, MLA_HEADS, MLA_ROPE))], axis=-1)
    return k, v


def even_layer(h_lat, h_ctx, w_in, q_norm, w_qb, kv_norm, w_kvb, w_out, rope, ctx_out):
    b, n, _ = h_lat.shape
    f_in, f_gate, q_a, kv_a, k_pe, m_gate = split_cols(h_lat @ w_in, EVEN_SPLITS)
    if ctx_out:
        cf_in, cf_gate, cq_a, ckv_a, ck_pe, cm_gate = split_cols(h_ctx @ w_in, EVEN_SPLITS)
    else:
        ckv_a, ck_pe = split_cols(h_ctx @ w_in[:, KV_COL0:KV_COL1], (MLA_KV_RANK, MLA_ROPE))
    k_c, v_c = mla_kv(ckv_a, ck_pe, kv_norm, w_kvb, None)
    k_l, v_l = mla_kv(kv_a, k_pe, kv_norm, w_kvb, rope)
    q_l = mla_q(q_a, q_norm, w_qb, rope)
    a_l = blocked_dense_attention(q_l, jnp.concatenate([k_c, k_l], axis=1),
                                  jnp.concatenate([v_c, v_l], axis=1), MLA_SCALE)
    y_l = jnp.concatenate([fourier_mix(f_in) * jax.nn.silu(f_gate),
                           a_l.reshape(b, n, MLA_WIDTH) * jax.nn.silu(m_gate)], axis=-1) @ w_out
    if not ctx_out:
        return y_l, None
    cb, cn, _ = h_ctx.shape
    q_c = mla_q(cq_a, q_norm, w_qb, None)
    a_c = dense_attention(q_c, k_c, v_c, MLA_SCALE)
    y_c = jnp.concatenate([fourier_mix(cf_in) * jax.nn.silu(cf_gate),
                           a_c.reshape(cb, cn, MLA_WIDTH) * jax.nn.silu(cm_gate)], axis=-1) @ w_out
    return y_l, y_c


def windowed_gqa(q, k, v, kc, vc, sink):
    b, n = q.shape[:2]
    nb = n // BLOCK
    pad = ((0, 0), (BLOCK, BLOCK), (0, 0), (0, 0))

    def band(t):
        tp = jnp.pad(t, pad).reshape(b, nb + 2, BLOCK, *t.shape[2:])
        return jnp.concatenate([tp[:, :-2], tp[:, 1:-1], tp[:, 2:]], axis=2)

    kb, vb = band(k), band(v)
    qb = q.reshape(b, nb, BLOCK, GQA_KV_HEADS, GQA_GROUP, GQA_HEAD_DIM)
    s_ctx = jnp.einsum('bnqhgd,bchd->bnhgqc', qb, kc).astype(jnp.float32) * GQA_SCALE
    s_band = jnp.einsum('bnqhgd,bnkhd->bnhgqk', qb, kb).astype(jnp.float32) * GQA_SCALE
    blk = jnp.arange(nb)[:, None, None] * BLOCK
    qpos = blk + jnp.arange(BLOCK)[None, :, None]
    kpos = blk - BLOCK + jnp.arange(3 * BLOCK)[None, None, :]
    valid = (jnp.abs(qpos - kpos) <= WINDOW) & (kpos >= 0) & (kpos < n)
    s_band = jnp.where(valid[None, :, None, None], s_band, -jnp.inf)
    s_sink = jnp.broadcast_to(sink[None, None, :, :, None, None], s_ctx.shape[:-1] + (1,))
    p = jax.nn.softmax(jnp.concatenate([s_ctx, s_band, s_sink], axis=-1), axis=-1).astype(v.dtype)
    nc = kc.shape[1]
    o = (jnp.einsum('bnhgqc,bchd->bnqhgd', p[..., :nc], vc)
         + jnp.einsum('bnhgqk,bnkhd->bnqhgd', p[..., nc:nc + 3 * BLOCK], vb))
    return o.reshape(b, n, GQA_WIDTH)


def ctx_gqa(q, k, v, sink):
    b, n = q.shape[:2]
    s = jnp.einsum('bqhgd,bchd->bhgqc', q, k).astype(jnp.float32) * GQA_SCALE
    s_sink = jnp.broadcast_to(sink[None, :, :, None, None], s.shape[:-1] + (1,))
    p = jax.nn.softmax(jnp.concatenate([s, s_sink], axis=-1), axis=-1)[..., :-1].astype(v.dtype)
    return jnp.einsum('bhgqc,bchd->bqhgd', p, v).reshape(b, n, GQA_WIDTH)


def odd_layer(h_lat, h_ctx, w_in, sink, w_out, rope, ctx_out):
    b, n, _ = h_lat.shape
    cb, cn, _ = h_ctx.shape
    q, k, v, g = split_cols(h_lat @ w_in, ODD_SPLITS)
    q = apply_rope(q.reshape(b, n, GQA_KV_HEADS, GQA_GROUP, GQA_HEAD_DIM), *rope)
    k = apply_rope(k.reshape(b, n, GQA_KV_HEADS, GQA_HEAD_DIM), *rope)
    v = v.reshape(b, n, GQA_KV_HEADS, GQA_HEAD_DIM)
    if ctx_out:
        cq, ck, cv, cg = split_cols(h_ctx @ w_in, ODD_SPLITS)
    else:
        ck, cv = split_cols(h_ctx @ w_in[:, GQA_Q:GQA_Q + 2 * GQA_KV], (GQA_KV, GQA_KV))
    ck = ck.reshape(cb, cn, GQA_KV_HEADS, GQA_HEAD_DIM)
    cv = cv.reshape(cb, cn, GQA_KV_HEADS, GQA_HEAD_DIM)
    sink_hg = sink.reshape(GQA_KV_HEADS, GQA_GROUP).astype(jnp.float32)
    y_l = (windowed_gqa(q, k, v, ck, cv, sink_hg) * jax.nn.silu(g)) @ w_out
    if not ctx_out:
        return y_l, None
    cq = cq.reshape(cb, cn, GQA_KV_HEADS, GQA_GROUP, GQA_HEAD_DIM)
    y_c = (ctx_gqa(cq, ck, cv, sink_hg) * jax.nn.silu(cg)) @ w_out
    return y_l, y_c


def setup_inputs(seed: int = 0) -> dict:
    key = jax.random.key(seed)
    ks = jax.random.split(key, 17)

    def nrm(k, shape, scale):
        return jax.random.normal(k, shape, jnp.float32) * scale

    return {
        'x': nrm(ks[0], (BATCH, SEQ, D_MODEL), 1.0),
        'c': nrm(ks[1], (BATCH, D_MODEL), 1.0),
        'ctx': nrm(ks[2], (BATCH, CTX_LEN, D_MODEL), 1.0),
        'c_ctx': nrm(ks[3], (D_MODEL,), 1.0),
        'w_mod': nrm(ks[4], (DEPTH, D_MODEL, 3 * D_MODEL), 0.5 * D_MODEL ** -0.5),
        'b_mod': nrm(ks[5], (DEPTH, 3 * D_MODEL), 0.02),
        'norm_g': 1.0 + nrm(ks[6], (DEPTH, D_MODEL), 0.02),
        'e_w_in': nrm(ks[7], (N_EVEN, D_MODEL, EVEN_IN), D_MODEL ** -0.5),
        'e_q_norm': 1.0 + nrm(ks[8], (N_EVEN, MLA_Q_RANK), 0.02),
        'e_w_qb': nrm(ks[9], (N_EVEN, MLA_Q_RANK, MLA_HEADS * (MLA_NOPE + MLA_ROPE)), MLA_Q_RANK ** -0.5),
        'e_kv_norm': 1.0 + nrm(ks[10], (N_EVEN, MLA_KV_RANK), 0.02),
        'e_w_kvb': nrm(ks[11], (N_EVEN, MLA_KV_RANK, MLA_HEADS * (MLA_NOPE + MLA_V)), MLA_KV_RANK ** -0.5),
        'e_w_out': nrm(ks[12], (N_EVEN, EVEN_OUT, D_MODEL), EVEN_OUT ** -0.5),
        'o_w_in': nrm(ks[13], (N_ODD, D_MODEL, ODD_IN), D_MODEL ** -0.5),
        'o_sink': nrm(ks[14], (N_ODD, GQA_HEADS), 0.5),
        'o_w_out': nrm(ks[15], (N_ODD, GQA_WIDTH, D_MODEL), GQA_WIDTH ** -0.5),
        'final_g': 1.0 + nrm(ks[16], (D_MODEL,), 0.02),
    }


def reference(x, c, ctx, c_ctx, w_mod, b_mod, norm_g, e_w_in, e_q_norm, e_w_qb, e_kv_norm,
              e_w_kvb, e_w_out, o_w_in, o_sink, o_w_out, final_g):
    n = x.shape[1]
    rope_mla = axial_rope_tables(n, MLA_ROPE)
    rope_gqa = axial_rope_tables(n, GQA_HEAD_DIM)
    silu_c = jax.nn.silu(c)
    silu_cc = jax.nn.silu(c_ctx)
    for l in range(DEPTH):
        last = l == DEPTH - 1
        mod_l = silu_c @ w_mod[l] + b_mod[l]
        mod_c = silu_cc @ w_mod[l] + b_mod[l]
        sh_l, sc_l, g_l = jnp.split(mod_l, 3, axis=-1)
        sh_c, sc_c, g_c = jnp.split(mod_c, 3, axis=-1)
        h_l = rmsnorm(x, norm_g[l]) * (1 + sc_l[:, None]) + sh_l[:, None]
        h_c = rmsnorm(ctx, norm_g[l]) * (1 + sc_c) + sh_c
        i = l // 2
        if l % 2 == 0:
            y_l, y_c = even_layer(h_l, h_c, e_w_in[i], e_q_norm[i], e_w_qb[i], e_kv_norm[i],
                                  e_w_kvb[i], e_w_out[i], rope_mla, not last)
        else:
            y_l, y_c = odd_layer(h_l, h_c, o_w_in[i], o_sink[i], o_w_out[i], rope_gqa, not last)
        x = x + g_l[:, None] * y_l
        if not last:
            ctx = ctx + g_c * y_c
    return rmsnorm(x, final_g)
```

```python
import functools
import math

import numpy as np
import jax
import jax.numpy as jnp
from jax import lax
from jax.experimental import pallas as pl
from jax.experimental.pallas import tpu as pltpu

F32 = jnp.float32
BF16 = jnp.bfloat16

D_MODEL = 1024
GRID_W = 64
EPS = 1e-6
ROPE_BASE = 10000.0
LANES = 128

F_GROUPS = 4
F_GROUP_DIM = 128
F_WIDTH = F_GROUPS * F_GROUP_DIM

MLA_HEADS = 8
MLA_NOPE = 64
MLA_ROPE = 32
MLA_V = 64
MLA_Q_RANK = 384
MLA_KV_RANK = 256
MLA_WIDTH = MLA_HEADS * MLA_V
MLA_SCALE = 1.0 / math.sqrt(MLA_NOPE + MLA_ROPE)

GQA_HEADS = 16
GQA_KV_HEADS = 4
GQA_GROUP = GQA_HEADS // GQA_KV_HEADS
GQA_HEAD_DIM = 64
WINDOW = 128
BLOCK = 128
GQA_Q = GQA_HEADS * GQA_HEAD_DIM
GQA_KV = GQA_KV_HEADS * GQA_HEAD_DIM
GQA_SCALE = 1.0 / math.sqrt(GQA_HEAD_DIM)

E_FIN, E_FGATE, E_MGATE, E_QA, E_KVA, E_KPE, E_END = 0, 512, 1024, 1536, 1920, 2176, 2304
O_Q, O_K, O_V, O_G, O_END = 0, 1024, 1280, 1536, 2560

NEG_BIG = -1e30
VMEM_LIMIT = 56 * 1024 * 1024


def _cparams(sem):
    return pltpu.CompilerParams(dimension_semantics=sem, vmem_limit_bytes=VMEM_LIMIT)


def _dot(a, b):
    return jnp.dot(a, b, preferred_element_type=F32)


def _dot_nt(a, b):
    return lax.dot_general(a, b, (((1,), (1,)), ((), ())), preferred_element_type=F32)


def _rms(x, g):
    return x * lax.rsqrt(jnp.mean(x * x, axis=-1, keepdims=True) + EPS) * g


def _silu(x):
    return x * jax.nn.sigmoid(x)


def _rope(x, tab_ref, shift):
    up = pltpu.roll(x, LANES - shift, 1)
    dn = pltpu.roll(x, shift, 1)
    return x * tab_ref[0] + up * tab_ref[1] + dn * tab_ref[2]


def _axial_rope_tables(n, rot_dim):
    rows = n // GRID_W
    row = jnp.broadcast_to(jnp.arange(rows)[:, None], (rows, GRID_W)).reshape(-1).astype(F32)
    col = jnp.broadcast_to(jnp.arange(GRID_W)[None, :], (rows, GRID_W)).reshape(-1).astype(F32)
    nf = rot_dim // 4
    inv = ROPE_BASE ** (-jnp.arange(nf, dtype=F32) / nf)
    ang = jnp.concatenate([row[:, None] * inv, col[:, None] * inv], axis=-1)
    return jnp.cos(ang), jnp.sin(ang)


def _mla_rope_tab(cos, sin, scale):
    n = cos.shape[0]
    z = lambda w: jnp.zeros((n, w), F32)
    cf = jnp.concatenate([jnp.ones((n, MLA_NOPE), F32), cos, cos, z(32)], axis=-1)
    s1 = jnp.concatenate([z(MLA_NOPE), -sin, z(16), z(32)], axis=-1)
    s2 = jnp.concatenate([z(MLA_NOPE), z(16), sin, z(32)], axis=-1)
    return jnp.stack([cf, s1, s2]) * scale


def _gqa_rope_tab(cos, sin, scale):
    z = jnp.zeros_like(sin)
    cf = jnp.concatenate([cos, cos, cos, cos], axis=-1)
    s1 = jnp.concatenate([-sin, z, -sin, z], axis=-1)
    s2 = jnp.concatenate([z, sin, z, sin], axis=-1)
    return jnp.stack([cf, s1, s2]) * scale


def _dft_cs(n):
    idx = np.arange(n, dtype=np.int64)
    ang = 2.0 * np.pi * ((idx[:, None] * idx[None, :]) % n).astype(np.float64) / n
    return np.cos(ang), np.sin(ang)


def _mod_kernel(s_ref, w_ref, b_ref, o_ref):
    s = _silu(s_ref[...])
    o_ref[0] = _dot(s.astype(BF16), w_ref[0].astype(BF16)) + b_ref[0]


def _mod_call(svec, w_mod, b_mod):
    depth, d, d3 = w_mod.shape
    tn = 768
    return pl.pallas_call(
        _mod_kernel,
        grid=(depth, d3 // tn),
        in_specs=[
            pl.BlockSpec((8, d), lambda l, j: (0, 0)),
            pl.BlockSpec((1, d, tn), lambda l, j: (l, 0, j)),
            pl.BlockSpec((1, 1, tn), lambda l, j: (l, 0, j)),
        ],
        out_specs=pl.BlockSpec((1, 8, tn), lambda l, j: (l, 0, j)),
        out_shape=jax.ShapeDtypeStruct((depth, 8, d3), F32),
        compiler_params=_cparams(("arbitrary", "arbitrary")),
        name="mod",
    )(svec, w_mod, b_mod.reshape(depth, 1, d3))


def _even_front(x, ng, sc, sh, w_in_ref, qn, w_qb_ref, kvn, w_kv_ref, vone, tq_ref, tk_ref, fc_ref):
    h = (_rms(x, ng) * (1.0 + sc) + sh).astype(BF16)
    f_in = _dot(h, w_in_ref[:, E_FIN:E_FGATE])
    f_gate = _dot(h, w_in_ref[:, E_FGATE:E_MGATE])
    m_gate = _dot(h, w_in_ref[:, E_MGATE:E_QA])
    q_a = _dot(h, w_in_ref[:, E_QA:E_KVA])
    kv_a = _dot(h, w_in_ref[:, E_KVA:E_KPE])
    kpe = _dot(h, w_in_ref[:, E_KPE:E_END])
    qh = _rms(q_a, qn).astype(BF16)
    ch = _rms(kv_a, kvn).astype(BF16)
    kpe_r = _rope(kpe, tk_ref, 16)
    qs, ks, vs = [], [], []
    for hd in range(MLA_HEADS):
        lo = hd * LANES
        qs.append(_rope(_dot(qh, w_qb_ref[:, lo:lo + LANES]), tq_ref, 16))
        ks.append(_dot(ch, w_kv_ref[:, lo:lo + LANES]) + kpe_r)
        vs.append(_dot(ch, w_kv_ref[:, D_MODEL + lo:D_MODEL + lo + LANES]) + vone[:, lo:lo + LANES])
    zr, zi = [], []
    fb = f_in.astype(BF16)
    for g in range(F_GROUPS):
        z = _dot(fb[:, g * LANES:(g + 1) * LANES], fc_ref[...])
        zr.append(z[:, :LANES])
        zi.append(z[:, LANES:])
    return qs, ks, vs, zr, zi, f_gate, m_gate


def _front0_kernel(x_ref, ng_ref, sc_ref, sh_ref, w_in_ref, qn_ref, w_qb_ref, kvn_ref, w_kv_ref, vone_ref,
                   tq_ref, tk_ref, fc_ref, q_ref, k_ref, v_ref, zr_ref, zi_ref, gf_ref, gm_ref):
    qs, ks, vs, zr, zi, f_gate, m_gate = _even_front(
        x_ref[0], ng_ref[...], sc_ref[0], sh_ref[0], w_in_ref, qn_ref[...], w_qb_ref, kvn_ref[...],
        w_kv_ref, vone_ref[...], tq_ref, tk_ref, fc_ref)
    for hd in range(MLA_HEADS):
        q_ref[0, hd] = qs[hd].astype(BF16)
        k_ref[0, hd] = ks[hd].astype(BF16)
        v_ref[0, hd] = vs[hd].astype(BF16)
    for g in range(F_GROUPS):
        zr_ref[0, :, g * LANES:(g + 1) * LANES] = zr[g].astype(BF16)
        zi_ref[0, :, g * LANES:(g + 1) * LANES] = zi[g].astype(BF16)
    gf_ref[0] = _silu(f_gate).astype(BF16)
    gm_ref[0] = _silu(m_gate).astype(BF16)


def _full(shape):
    nd = len(shape)
    return pl.BlockSpec(shape, lambda *_: (0,) * nd)


def _front0_call(x, ng, sc, sh, w_in, qn, w_qb, kvn, w_kv, vone, tq, tk, fc, tm):
    b, n, d = x.shape
    hq = jax.ShapeDtypeStruct((b, MLA_HEADS, n, LANES), BF16)
    fw = jax.ShapeDtypeStruct((b, n, F_WIDTH), BF16)
    head_spec = pl.BlockSpec((1, MLA_HEADS, tm, LANES), lambda bi, i: (bi, 0, i, 0))
    row_spec = pl.BlockSpec((1, tm, F_WIDTH), lambda bi, i: (bi, i, 0))
    vec_spec = pl.BlockSpec((1, 1, d), lambda bi, i: (bi, 0, 0))
    tab_spec = pl.BlockSpec((3, tm, LANES), lambda bi, i: (0, i, 0))
    return pl.pallas_call(
        _front0_kernel,
        grid=(b, n // tm),
        in_specs=[
            pl.BlockSpec((1, tm, d), lambda bi, i: (bi, i, 0)),
            _full(ng.shape), vec_spec, vec_spec,
            _full(w_in.shape), _full(qn.shape), _full(w_qb.shape), _full(kvn.shape), _full(w_kv.shape),
            _full(vone.shape), tab_spec, tab_spec, _full(fc.shape),
        ],
        out_specs=[head_spec, head_spec, head_spec, row_spec, row_spec, row_spec, row_spec],
        out_shape=[hq, hq, hq, fw, fw, fw, fw],
        compiler_params=_cparams(("parallel", "arbitrary")),
        name="front0",
    )(x, ng, sc, sh, w_in, qn, w_qb, kvn, w_kv, vone, tq, tk, fc)


def _pair_out(acc_even, acc_odd):
    lane = lax.broadcasted_iota(jnp.int32, acc_even.shape, 1)
    o_even = acc_even / acc_even[:, MLA_V:MLA_V + 1]
    o_odd = acc_odd / acc_odd[:, 0:1]
    return jnp.where(lane < MLA_V, o_even, o_odd)


def _ctx_kernel(ctx_scale, x_ref, ng0_ref, sc0_ref, sh0_ref, g0_ref, ng1_ref, sc1_ref, sh1_ref,
                w_in_ref, qn_ref, w_qb_ref, kvn_ref, w_kv_ref, vone_ref, tq_ref, tk_ref, fc_ref,
                fn_ref, w_out_ref, w_kv1_ref,
                kc_ref, vc_ref, ck1_ref, cv1_ref):
    x = x_ref[0]
    qs, ks, vs, zr, zi, f_gate, m_gate = _even_front(
        x, ng0_ref[...], sc0_ref[...], sh0_ref[...], w_in_ref, qn_ref[...], w_qb_ref, kvn_ref[...],
        w_kv_ref, vone_ref[...], tq_ref, tk_ref, fc_ref)
    pairs = []
    for hp in range(MLA_HEADS // 2):
        accs = []
        for e in range(2):
            hd = 2 * hp + e
            kb = ks[hd].astype(BF16)
            vb = vs[hd].astype(BF16)
            kc_ref[0, hd] = kb
            vc_ref[0, hd] = vb
            s = _dot_nt(qs[hd].astype(BF16), kb)
            p = jnp.exp(s - jnp.max(s, axis=-1, keepdims=True))
            accs.append(_dot(p.astype(BF16), vb))
        pairs.append(_pair_out(accs[0], accs[1]))
    a = jnp.concatenate(pairs, axis=-1)
    z = jnp.concatenate([jnp.concatenate(zr, axis=-1), jnp.concatenate(zi, axis=-1)], axis=0).astype(BF16)
    fm = _dot(fn_ref[...], z) * ctx_scale
    yf = (fm * _silu(f_gate)).astype(BF16)
    ya = (a * _silu(m_gate)).astype(BF16)
    y = _dot(yf, w_out_ref[0:F_WIDTH, :]) + _dot(ya, w_out_ref[F_WIDTH:, :])
    x1 = x + g0_ref[...] * y
    h1 = (_rms(x1, ng1_ref[...]) * (1.0 + sc1_ref[...]) + sh1_ref[...]).astype(BF16)
    kv1 = _dot(h1, w_kv1_ref[...])
    ck1_ref[0] = kv1[:, :GQA_KV].astype(BF16)
    cv1_ref[0] = kv1[:, GQA_KV:].astype(BF16)


def _ctx_call(ctx, vecs0, vecs1, w_in, qn, w_qb, kvn, w_kv, vone, tq, tk, fc, fn, w_out, w_kv1):
    b, c, d = ctx.shape
    ng0, sc0, sh0, g0 = vecs0
    ng1, sc1, sh1 = vecs1
    ctx_scale = 1.0 / math.sqrt(c * F_GROUP_DIM)
    consts = [ng0, sc0, sh0, g0, ng1, sc1, sh1, w_in, qn, w_qb, kvn, w_kv, vone, tq, tk, fc, fn, w_out, w_kv1]
    hk = jax.ShapeDtypeStruct((b, MLA_HEADS, c, LANES), BF16)
    kv1 = jax.ShapeDtypeStruct((b, c, GQA_KV), BF16)
    head_spec = pl.BlockSpec((1, MLA_HEADS, c, LANES), lambda bi: (bi, 0, 0, 0))
    kv1_spec = pl.BlockSpec((1, c, GQA_KV), lambda bi: (bi, 0, 0))
    return pl.pallas_call(
        functools.partial(_ctx_kernel, ctx_scale),
        grid=(b,),
        in_specs=[pl.BlockSpec((1, c, d), lambda bi: (bi, 0, 0))] + [_full(a.shape) for a in consts],
        out_specs=[head_spec, head_spec, kv1_spec, kv1_spec],
        out_shape=[hk, hk, kv1, kv1],
        compiler_params=_cparams(("arbitrary",)),
        name="ctx",
    )(ctx, *consts)


def _attn0_kernel(q_ref, kc_ref, vc_ref, kl_ref, vl_ref, gm_ref, o_ref, m_sc, acc_sc):
    j = pl.program_id(3)

    def step(k_ref, v_ref):
        for e in range(2):
            s = _dot_nt(q_ref[0, e], k_ref[0, e])
            m_prev = m_sc[e]
            m_new = jnp.maximum(m_prev, jnp.max(s, axis=-1, keepdims=True))
            alpha = jnp.exp(m_prev - m_new)
            p = jnp.exp(s - m_new[:, 0:1])
            acc_sc[e] = alpha * acc_sc[e] + _dot(p.astype(BF16), v_ref[0, e])
            m_sc[e] = m_new

    @pl.when(j == 0)
    def _():
        m_sc[...] = jnp.full(m_sc.shape, -jnp.inf, F32)
        acc_sc[...] = jnp.zeros(acc_sc.shape, F32)
        step(kc_ref, vc_ref)

    @pl.when(j > 0)
    def _():
        step(kl_ref, vl_ref)

    @pl.when(j == pl.num_programs(3) - 1)
    def _():
        o_ref[0] = (_pair_out(acc_sc[0], acc_sc[1]) * gm_ref[0].astype(F32)).astype(BF16)


def _attn0_call(q, kc, vc, kl, vl, gm, tq, tk):
    b, h, n, _ = q.shape
    c = kc.shape[2]
    q_spec = pl.BlockSpec((1, 2, tq, LANES), lambda bi, hp, i, j: (bi, hp, i, 0))
    c_spec = pl.BlockSpec((1, 2, c, LANES), lambda bi, hp, i, j: (bi, hp, 0, 0))
    l_spec = pl.BlockSpec((1, 2, tk, LANES), lambda bi, hp, i, j: (bi, hp, jnp.maximum(j - 1, 0), 0))
    o_spec = pl.BlockSpec((1, tq, LANES), lambda bi, hp, i, j: (bi, i, hp))
    return pl.pallas_call(
        _attn0_kernel,
        grid=(b, h // 2, n // tq, 1 + n // tk),
        in_specs=[q_spec, c_spec, c_spec, l_spec, l_spec, o_spec],
        out_specs=o_spec,
        out_shape=jax.ShapeDtypeStruct((b, n, MLA_WIDTH), BF16),
        scratch_shapes=[pltpu.VMEM((2, tq, LANES), F32), pltpu.VMEM((2, tq, LANES), F32)],
        compiler_params=_cparams(("parallel", "parallel", "parallel", "arbitrary")),
        name="attn0",
    )(q, kc, vc, kl, vl, gm)


def _fft_a_kernel(t2, n1, zr_ref, zi_ref, fa_ref, tw_ref, tr_ref, ti_ref):
    z = jnp.concatenate([zr_ref[0], zi_ref[0]], axis=0)
    a = _dot(fa_ref[...], z)
    for j in range(t2):
        c = tw_ref[0, j]
        s = tw_ref[1, j]
        for g in range(F_GROUPS):
            lo = j * F_WIDTH + g * LANES
            ar = a[0:n1, lo:lo + LANES]
            ai = a[n1:2 * n1, lo:lo + LANES]
            tr_ref[0, :, lo:lo + LANES] = (ar * c + ai * s).astype(BF16)
            ti_ref[0, :, lo:lo + LANES] = (ai * c - ar * s).astype(BF16)


def _fft_a_call(zr, zi, fa, tw, t2):
    b, n1, w = zr.shape
    spec = pl.BlockSpec((1, n1, t2 * F_WIDTH), lambda bi, j: (bi, 0, j))
    out = jax.ShapeDtypeStruct((b, n1, w), BF16)
    return pl.pallas_call(
        functools.partial(_fft_a_kernel, t2, n1),
        grid=(b, w // (t2 * F_WIDTH)),
        in_specs=[spec, spec, _full(fa.shape), pl.BlockSpec((2, t2, n1, LANES), lambda bi, j: (0, j, 0, 0))],
        out_specs=[spec, spec],
        out_shape=[out, out],
        compiler_params=_cparams(("parallel", "arbitrary")),
        name="fft_a",
    )(zr, zi, fa, tw)


def _fft_c_kernel(t1, tr_ref, ti_ref, fcs_ref, gf_ref, o_ref):
    for k in range(t1):
        rhs = jnp.concatenate([tr_ref[0, k], ti_ref[0, k]], axis=0)
        lo = k * F_WIDTH
        out = _dot(fcs_ref[...], rhs)
        o_ref[0, :, lo:lo + F_WIDTH] = (out * gf_ref[0, :, lo:lo + F_WIDTH].astype(F32)).astype(BF16)


def _fft_c_call(tr, ti, fcs, gf, t1):
    b, n1, r, w = tr.shape
    t_spec = pl.BlockSpec((1, t1, r, w), lambda bi, i: (bi, i, 0, 0))
    g_spec = pl.BlockSpec((1, r, t1 * w), lambda bi, i: (bi, 0, i))
    return pl.pallas_call(
        functools.partial(_fft_c_kernel, t1),
        grid=(b, n1 // t1),
        in_specs=[t_spec, t_spec, _full(fcs.shape), g_spec],
        out_specs=g_spec,
        out_shape=jax.ShapeDtypeStruct((b, r, n1 * w), BF16),
        compiler_params=_cparams(("parallel", "arbitrary")),
        name="fft_c",
    )(tr, ti, fcs, gf)


def _mid_kernel(fm_ref, ag_ref, x_ref, g0_ref, w_out_ref, ng_ref, sc_ref, sh_ref, w_in_ref, tq_ref, tk_ref,
                x1_ref, q_ref, k_ref, v_ref, sg_ref):
    y = _dot(fm_ref[0], w_out_ref[0:F_WIDTH, :]) + _dot(ag_ref[0], w_out_ref[F_WIDTH:, :])
    x1 = x_ref[0] + g0_ref[0] * y
    x1_ref[0] = x1
    h = (_rms(x1, ng_ref[...]) * (1.0 + sc_ref[0]) + sh_ref[0]).astype(BF16)
    for c in range(GQA_Q // LANES):
        lo = c * LANES
        q_ref[0, :, lo:lo + LANES] = _rope(_dot(h, w_in_ref[:, O_Q + lo:O_Q + lo + LANES]), tq_ref, 32).astype(BF16)
    for c in range(GQA_KV // LANES):
        lo = c * LANES
        k_ref[0, :, lo:lo + LANES] = _rope(_dot(h, w_in_ref[:, O_K + lo:O_K + lo + LANES]), tk_ref, 32).astype(BF16)
    v_ref[0] = _dot(h, w_in_ref[:, O_V:O_G]).astype(BF16)
    sg_ref[0] = _silu(_dot(h, w_in_ref[:, O_G:O_END])).astype(BF16)


def _mid_call(fm, ag, x, g0, w_out, ng, sc, sh, w_in, tq, tk, tm):
    b, n, d = x.shape
    row = lambda w: pl.BlockSpec((1, tm, w), lambda bi, i: (bi, i, 0))
    vec_spec = pl.BlockSpec((1, 1, d), lambda bi, i: (bi, 0, 0))
    tab_spec = pl.BlockSpec((3, tm, LANES), lambda bi, i: (0, i, 0))
    return pl.pallas_call(
        _mid_kernel,
        grid=(b, n // tm),
        in_specs=[row(F_WIDTH), row(MLA_WIDTH), row(d), vec_spec, _full(w_out.shape), _full(ng.shape),
                  vec_spec, vec_spec, _full(w_in.shape), tab_spec, tab_spec],
        out_specs=[row(d), row(GQA_Q), row(GQA_KV), row(GQA_KV), row(GQA_Q)],
        out_shape=[jax.ShapeDtypeStruct((b, n, d), F32), jax.ShapeDtypeStruct((b, n, GQA_Q), BF16),
                   jax.ShapeDtypeStruct((b, n, GQA_KV), BF16), jax.ShapeDtypeStruct((b, n, GQA_KV), BF16),
                   jax.ShapeDtypeStruct((b, n, GQA_Q), BF16)],
        compiler_params=_cparams(("parallel", "arbitrary")),
        name="mid",
    )(fm, ag, x, g0, w_out, ng, sc, sh, w_in, tq, tk)


def _attn1_kernel(nc, sink_ref, q_ref, kp_ref, kc_ref, kn_ref, vp_ref, vc_ref, vn_ref, ck_ref, cv_ref,
                  sg_ref, x1_ref, g1_ref, w_out_ref, fg_ref, o_ref):
    i = pl.program_id(1)
    last = pl.num_programs(1) - 1
    nk = nc + 3 * BLOCK
    rows = GQA_GROUP * BLOCK
    r = lax.broadcasted_iota(jnp.int32, (rows, nk), 0) & (BLOCK - 1)
    col = lax.broadcasted_iota(jnp.int32, (rows, nk), 1) - nc
    kpos = (i - 1) * BLOCK + col
    dist = BLOCK + r - col
    valid = (col < 0) | ((dist <= WINDOW) & (dist >= -WINDOW) & (kpos >= 0) & (kpos < (last + 1) * BLOCK))
    lane = lax.broadcasted_iota(jnp.int32, (BLOCK, LANES), 1)
    low = lane < GQA_HEAD_DIM

    k_all = jnp.concatenate([ck_ref[0], kp_ref[0], kc_ref[0], kn_ref[0]], axis=0)
    v_all = jnp.concatenate([cv_ref[0], vp_ref[0], vc_ref[0], vn_ref[0]], axis=0)
    chunks = [None] * (GQA_Q // LANES)
    for g in range(GQA_KV_HEADS):
        ch = g // 2
        kg = k_all[:, ch * LANES:(ch + 1) * LANES]
        vg = v_all[:, ch * LANES:(ch + 1) * LANES]
        g_low = (g % 2) == 0
        keep = low if g_low else jnp.logical_not(low)
        qs = []
        sinks = []
        for p2 in range(2):
            qp = q_ref[0, :, (2 * g + p2) * LANES:(2 * g + p2 + 1) * LANES].astype(F32)
            qsw = pltpu.roll(qp, GQA_HEAD_DIM, 1)
            q_lo, q_hi = (qp, qsw) if g_low else (qsw, qp)
            zero = jnp.zeros_like(qp)
            qs.append(jnp.where(keep, q_lo, zero).astype(BF16))
            qs.append(jnp.where(keep, q_hi, zero).astype(BF16))
            sinks.append(jnp.full((BLOCK, 1), sink_ref[4 * g + 2 * p2], F32))
            sinks.append(jnp.full((BLOCK, 1), sink_ref[4 * g + 2 * p2 + 1], F32))
        q4 = jnp.concatenate(qs, axis=0)
        sk = jnp.concatenate(sinks, axis=0)
        s = jnp.where(valid, _dot_nt(q4, kg), NEG_BIG)
        m = jnp.maximum(jnp.max(s, axis=-1, keepdims=True), sk)
        p = jnp.exp(s - m)
        l = jnp.sum(p, axis=-1, keepdims=True) + jnp.exp(sk - m)
        res = _dot(p.astype(BF16), vg) / l
        for p2 in range(2):
            r0 = res[(2 * p2) * BLOCK:(2 * p2 + 1) * BLOCK]
            r1 = res[(2 * p2 + 1) * BLOCK:(2 * p2 + 2) * BLOCK]
            if g_low:
                chunks[2 * g + p2] = jnp.where(low, r0, pltpu.roll(r1, GQA_HEAD_DIM, 1))
            else:
                chunks[2 * g + p2] = jnp.where(low, pltpu.roll(r0, GQA_HEAD_DIM, 1), r1)
    o = jnp.concatenate(chunks, axis=-1)
    og = (o * sg_ref[0].astype(F32)).astype(BF16)
    y = _dot(og, w_out_ref[...])
    x2 = x1_ref[0] + g1_ref[0] * y
    o_ref[0] = _rms(x2, fg_ref[...])


def _attn1_call(sink, q, k, v, ck, cv, sg, x1, g1, w_out, fg):
    b, n, d = x1.shape
    nc = ck.shape[1]
    nb = n // BLOCK
    row = lambda w: pl.BlockSpec((1, BLOCK, w), lambda bi, i: (bi, i, 0))
    prev = pl.BlockSpec((1, BLOCK, GQA_KV), lambda bi, i: (bi, jnp.maximum(i - 1, 0), 0))
    nxt = pl.BlockSpec((1, BLOCK, GQA_KV), lambda bi, i: (bi, jnp.minimum(i + 1, nb - 1), 0))
    c_spec = pl.BlockSpec((1, nc, GQA_KV), lambda bi, i: (bi, 0, 0))
    vec_spec = pl.BlockSpec((1, 1, d), lambda bi, i: (bi, 0, 0))
    return pl.pallas_call(
        functools.partial(_attn1_kernel, nc),
        grid=(b, nb),
        in_specs=[pl.BlockSpec(memory_space=pltpu.SMEM), row(GQA_Q), prev, row(GQA_KV), nxt,
                  prev, row(GQA_KV), nxt, c_spec, c_spec, row(GQA_Q), row(d), vec_spec,
                  _full(w_out.shape), _full(fg.shape)],
        out_specs=row(d),
        out_shape=jax.ShapeDtypeStruct((b, n, d), F32),
        compiler_params=_cparams(("parallel", "arbitrary")),
        name="attn1",
    )(sink, q, k, k, k, v, v, v, ck, cv, sg, x1, g1, w_out, fg)


def _prep_even_weights(w_in, w_qb, w_kvb):
    d = w_in.shape[0]
    f_in, f_gate, q_a, kv_a, k_pe, m_gate = jnp.split(
        w_in, np.cumsum([F_WIDTH, F_WIDTH, MLA_Q_RANK, MLA_KV_RANK, MLA_ROPE])[:].tolist(), axis=1)
    kpe_blk = jnp.concatenate([jnp.zeros((d, MLA_NOPE), F32), k_pe, jnp.zeros((d, 32), F32)], axis=1)
    w_in_p = jnp.concatenate([f_in, f_gate, m_gate, q_a, kv_a, kpe_blk], axis=1).astype(BF16)
    qb = w_qb.reshape(MLA_Q_RANK, MLA_HEADS, MLA_NOPE + MLA_ROPE)
    w_qb_p = jnp.pad(qb, ((0, 0), (0, 0), (0, 32))).reshape(MLA_Q_RANK, MLA_HEADS * LANES).astype(BF16)
    kvb = w_kvb.reshape(MLA_KV_RANK, MLA_HEADS, MLA_NOPE + MLA_V)
    k_part = jnp.pad(kvb[:, :, :MLA_NOPE], ((0, 0), (0, 0), (0, 64))).reshape(MLA_KV_RANK, MLA_HEADS * LANES)
    v_src = kvb[:, :, MLA_NOPE:].reshape(MLA_KV_RANK, MLA_HEADS // 2, 2, MLA_V)
    zeros = jnp.zeros((MLA_KV_RANK, MLA_HEADS // 2, MLA_V), F32)
    v_even = jnp.concatenate([v_src[:, :, 0], zeros], axis=-1)
    v_odd = jnp.concatenate([zeros, v_src[:, :, 1]], axis=-1)
    v_part = jnp.stack([v_even, v_odd], axis=2).reshape(MLA_KV_RANK, MLA_HEADS * LANES)
    w_kv_p = jnp.concatenate([k_part, v_part], axis=1).astype(BF16)
    vone = np.zeros((1, MLA_HEADS * LANES), np.float32)
    for hd in range(MLA_HEADS):
        vone[0, hd * LANES + (MLA_V if hd % 2 == 0 else 0)] = 1.0
    return w_in_p, w_qb_p, w_kv_p, jnp.asarray(vone)


def _fft_consts(n, c):
    n1 = n // LANES
    c1, s1 = _dft_cs(n1)
    fa = np.block([[c1, s1], [-s1, c1]])
    c2, s2 = _dft_cs(LANES)
    fcs = np.concatenate([c2, s2], axis=1) / math.sqrt(n * F_GROUP_DIM)
    fc = np.concatenate([c2, -s2], axis=1)
    r = np.arange(LANES, dtype=np.int64)[:, None]
    k1 = np.arange(n1, dtype=np.int64)[None, :]
    ang = 2.0 * np.pi * ((r * k1) % n).astype(np.float64) / n
    tw = np.stack([np.cos(ang), np.sin(ang)])
    tw = np.broadcast_to(tw[..., None], (2, LANES, n1, LANES))
    cn, sn = _dft_cs(c)
    fn = np.concatenate([cn, sn], axis=1)
    as32 = lambda a: jnp.asarray(np.ascontiguousarray(a, dtype=np.float32))
    return as32(fa).astype(BF16), as32(fcs).astype(BF16), as32(fc).astype(BF16), as32(tw), as32(fn).astype(BF16)


def _pick_tile(n, pref):
    t = pref
    while n % t:
        t //= 2
    return t


def kernel(x, c, ctx, c_ctx, w_mod, b_mod, norm_g, e_w_in, e_q_norm, e_w_qb, e_kv_norm, e_w_kvb, e_w_out,
           o_w_in, o_sink, o_w_out, final_g):
    b, n, d = x.shape
    nc = ctx.shape[1]
    assert d == D_MODEL and n % (LANES * 8) == 0 and nc % LANES == 0 and b <= 7

    svec = jnp.zeros((8, d), F32).at[:b].set(c).at[b].set(c_ctx)
    mod = _mod_call(svec, w_mod, b_mod)
    sh, sc, gt = mod[:, :, :d], mod[:, :, d:2 * d], mod[:, :, 2 * d:]
    lat = lambda t, l: t[l, :b].reshape(b, 1, d)
    cvec = lambda t, l: t[l, b].reshape(1, d)
    ng0, ng1 = norm_g[0].reshape(1, d), norm_g[1].reshape(1, d)

    cos_m, sin_m = _axial_rope_tables(n, MLA_ROPE)
    tq0 = _mla_rope_tab(cos_m, sin_m, MLA_SCALE)
    tk0 = _mla_rope_tab(cos_m, sin_m, 1.0)
    one_c, zero_c = jnp.ones((nc, MLA_ROPE // 2), F32), jnp.zeros((nc, MLA_ROPE // 2), F32)
    tq0c = _mla_rope_tab(one_c, zero_c, MLA_SCALE)
    tk0c = _mla_rope_tab(one_c, zero_c, 1.0)
    cos_g, sin_g = _axial_rope_tables(n, GQA_HEAD_DIM)
    tq1 = _gqa_rope_tab(cos_g, sin_g, GQA_SCALE)
    tk1 = _gqa_rope_tab(cos_g, sin_g, 1.0)
    fa, fcs, fc, tw, fn = _fft_consts(n, nc)

    w_in0, w_qb0, w_kv0, vone = _prep_even_weights(e_w_in[0], e_w_qb[0], e_w_kvb[0])
    qn0 = e_q_norm[0].reshape(1, MLA_Q_RANK)
    kvn0 = e_kv_norm[0].reshape(1, MLA_KV_RANK)
    w_out0 = e_w_out[0].astype(BF16)
    w_in1 = o_w_in[0].astype(BF16)
    w_out1 = o_w_out[0].astype(BF16)

    kc, vc, ck1, cv1 = _ctx_call(
        ctx, (ng0, cvec(sc, 0), cvec(sh, 0), cvec(gt, 0)), (ng1, cvec(sc, 1), cvec(sh, 1)),
        w_in0, qn0, w_qb0, kvn0, w_kv0, vone, tq0c, tk0c, fc, fn, w_out0, w_in1[:, O_K:O_G])

    tm = _pick_tile(n, 256)
    q0, k0, v0, zr, zi, gf, gm = _front0_call(
        x, ng0, lat(sc, 0), lat(sh, 0), w_in0, qn0, w_qb0, kvn0, w_kv0, vone, tq0, tk0, fc, tm)

    ag = _attn0_call(q0, kc, vc, k0, v0, gm, _pick_tile(n, 512), _pick_tile(n, 512))

    n1 = n // LANES
    tr, ti = _fft_a_call(zr.reshape(b, n1, LANES * F_WIDTH), zi.reshape(b, n1, LANES * F_WIDTH), fa, tw, 8)
    fmg = _fft_c_call(tr.reshape(b, n1, LANES, F_WIDTH), ti.reshape(b, n1, LANES, F_WIDTH), fcs,
                      gf.reshape(b, LANES, n1 * F_WIDTH), _pick_tile(n1, 8))
    fmg = fmg.reshape(b, n, F_WIDTH)

    x1, q1, k1, v1, sg = _mid_call(fmg, ag, x, lat(gt, 0), w_out0, ng1, lat(sc, 1), lat(sh, 1), w_in1,
                                   tq1, tk1, tm)

    return _attn1_call(o_sink[0], q1, k1, v1, ck1, cv1, sg, x1, lat(gt, 1), w_out1, final_g.reshape(1, d))
```

```python
import functools
import math

import numpy as np
import jax
import jax.numpy as jnp
from jax import lax
from jax.experimental import pallas as pl
from jax.experimental.pallas import tpu as pltpu

F32 = jnp.float32
BF16 = jnp.bfloat16

D_MODEL = 1024
GRID_W = 64
EPS = 1e-6
ROPE_BASE = 10000.0
LANES = 128

F_GROUPS = 4
F_GROUP_DIM = 128
F_WIDTH = F_GROUPS * F_GROUP_DIM

MLA_HEADS = 8
MLA_NOPE = 64
MLA_ROPE = 32
MLA_V = 64
MLA_Q_RANK = 384
MLA_KV_RANK = 256
MLA_WIDTH = MLA_HEADS * MLA_V
MLA_SCALE = 1.0 / math.sqrt(MLA_NOPE + MLA_ROPE)
LOG2E = math.log2(math.e)
MLA_VT_ROWS = 80

GQA_HEADS = 16
GQA_KV_HEADS = 4
GQA_GROUP = GQA_HEADS // GQA_KV_HEADS
GQA_HEAD_DIM = 64
WINDOW = 128
BLOCK = 128
GQA_Q = GQA_HEADS * GQA_HEAD_DIM
GQA_KV = GQA_KV_HEADS * GQA_HEAD_DIM
GQA_SCALE = 1.0 / math.sqrt(GQA_HEAD_DIM)

E_FIN, E_FGATE, E_MGATE, E_QA, E_KVA, E_KPE, E_END = 0, 512, 1024, 1536, 1920, 2176, 2304
O_Q, O_K, O_V, O_G, O_END = 0, 1024, 1280, 1536, 2560

NEG_BIG = -1e30
VMEM_LIMIT = 56 * 1024 * 1024


def _cparams(sem):
    return pltpu.CompilerParams(dimension_semantics=sem, vmem_limit_bytes=VMEM_LIMIT)


def _dot(a, b):
    return jnp.dot(a, b, preferred_element_type=F32)


def _dot_nt(a, b):
    return lax.dot_general(a, b, (((1,), (1,)), ((), ())), preferred_element_type=F32)


def _rms(x, g):
    return x * lax.rsqrt(jnp.mean(x * x, axis=-1, keepdims=True) + EPS) * g


def _silu(x):
    return x * jax.nn.sigmoid(x)


def _rope(x, tab_ref, shift):
    up = pltpu.roll(x, LANES - shift, 1)
    dn = pltpu.roll(x, shift, 1)
    return x * tab_ref[0] + up * tab_ref[1] + dn * tab_ref[2]


def _full(shape):
    nd = len(shape)
    return pl.BlockSpec(shape, lambda *_: (0,) * nd)


def _axial_rope_tables(n, rot_dim):
    rows = n // GRID_W
    row = jnp.broadcast_to(jnp.arange(rows)[:, None], (rows, GRID_W)).reshape(-1).astype(F32)
    col = jnp.broadcast_to(jnp.arange(GRID_W)[None, :], (rows, GRID_W)).reshape(-1).astype(F32)
    nf = rot_dim // 4
    inv = ROPE_BASE ** (-jnp.arange(nf, dtype=F32) / nf)
    ang = jnp.concatenate([row[:, None] * inv, col[:, None] * inv], axis=-1)
    return jnp.cos(ang), jnp.sin(ang)


def _mla_rope_tab(cos, sin):
    n = cos.shape[0]
    z = lambda w: jnp.zeros((n, w), F32)
    cf = jnp.concatenate([jnp.ones((n, MLA_NOPE), F32), cos, cos, z(32)], axis=-1)
    s1 = jnp.concatenate([z(MLA_NOPE), -sin, z(16), z(32)], axis=-1)
    s2 = jnp.concatenate([z(MLA_NOPE), z(16), sin, z(32)], axis=-1)
    return jnp.stack([cf, s1, s2])


def _gqa_rope_tab(cos, sin, scale):
    z = jnp.zeros_like(sin)
    cf = jnp.concatenate([cos, cos, cos, cos], axis=-1)
    s1 = jnp.concatenate([-sin, z, -sin, z], axis=-1)
    s2 = jnp.concatenate([z, sin, z, sin], axis=-1)
    return jnp.stack([cf, s1, s2]) * scale


def _dft_cs(n):
    idx = np.arange(n, dtype=np.int64)
    ang = 2.0 * np.pi * ((idx[:, None] * idx[None, :]) % n).astype(np.float64) / n
    return np.cos(ang), np.sin(ang)


def _mod_kernel(s_ref, w_ref, b_ref, o_ref):
    s = _silu(s_ref[...])
    o_ref[0] = _dot(s.astype(BF16), w_ref[0].astype(BF16)) + b_ref[0]


def _mod_call(svec, w_mod, b_mod):
    depth, d, d3 = w_mod.shape
    tn = 768
    return pl.pallas_call(
        _mod_kernel,
        grid=(depth, d3 // tn),
        in_specs=[
            pl.BlockSpec((8, d), lambda l, j: (0, 0)),
            pl.BlockSpec((1, d, tn), lambda l, j: (l, 0, j)),
            pl.BlockSpec((1, 1, tn), lambda l, j: (l, 0, j)),
        ],
        out_specs=pl.BlockSpec((1, 8, tn), lambda l, j: (l, 0, j)),
        out_shape=jax.ShapeDtypeStruct((depth, 8, d3), F32),
        compiler_params=_cparams(("arbitrary", "arbitrary")),
        name="mod",
    )(svec, w_mod, b_mod.reshape(depth, 1, d3))


def _even_front(x, ng, sc, sh, w_in_ref, qn, w_qbt_ref, kvn, w_k_ref, w_vt_ref, vone, tqt_ref, tk_ref, fc_ref):
    m = x.shape[0]
    h = (_rms(x, ng) * (1.0 + sc) + sh).astype(BF16)
    f_in = _dot(h, w_in_ref[:, E_FIN:E_FGATE])
    f_gate = _dot(h, w_in_ref[:, E_FGATE:E_MGATE])
    m_gate = _dot(h, w_in_ref[:, E_MGATE:E_QA])
    q_a = _dot(h, w_in_ref[:, E_QA:E_KVA])
    kv_a = _dot(h, w_in_ref[:, E_KVA:E_KPE])
    kpe = _dot(h, w_in_ref[:, E_KPE:E_END])
    qh = _rms(q_a, qn).astype(BF16)
    ch = _rms(kv_a, kvn).astype(BF16)
    kpe_r = _rope(kpe, tk_ref, 16)
    qt = _dot_nt(w_qbt_ref[...], qh)
    cos = tqt_ref[0]
    sin = tqt_ref[1]
    pad = jnp.zeros((LANES - MLA_NOPE - MLA_ROPE, m), F32)
    qts, ks = [], []
    for hd in range(MLA_HEADS):
        lo = hd * LANES
        x1 = qt[lo + MLA_NOPE:lo + MLA_NOPE + 16]
        x2 = qt[lo + MLA_NOPE + 16:lo + MLA_NOPE + 32]
        qts.append(jnp.concatenate(
            [qt[lo:lo + MLA_NOPE] * (MLA_SCALE * LOG2E), x1 * cos - x2 * sin, x2 * cos + x1 * sin, pad], axis=0))
        ks.append(_dot(ch, w_k_ref[:, lo:lo + LANES]) + kpe_r)
    vt = _dot_nt(w_vt_ref[...], ch) + vone
    zr, zi = [], []
    fb = f_in.astype(BF16)
    for g in range(F_GROUPS):
        z = _dot(fb[:, g * LANES:(g + 1) * LANES], fc_ref[...])
        zr.append(z[:, :LANES])
        zi.append(z[:, LANES:])
    return qts, ks, vt, zr, zi, f_gate, m_gate


def _front0_kernel(x_ref, ng_ref, sc_ref, sh_ref, w_in_ref, qn_ref, w_qbt_ref, kvn_ref, w_k_ref, w_vt_ref,
                   vone_ref, tqt_ref, tk_ref, fc_ref, q_ref, k_ref, v_ref, zr_ref, zi_ref, gf_ref, gm_ref):
    qts, ks, vt, zr, zi, f_gate, m_gate = _even_front(
        x_ref[0], ng_ref[...], sc_ref[0], sh_ref[0], w_in_ref, qn_ref[...], w_qbt_ref, kvn_ref[...],
        w_k_ref, w_vt_ref, vone_ref[...], tqt_ref, tk_ref, fc_ref)
    for hd in range(MLA_HEADS):
        q_ref[0, hd] = qts[hd].astype(BF16)
        k_ref[0, hd] = ks[hd].astype(BF16)
        v_ref[0, hd, 0] = vt[hd * MLA_VT_ROWS:(hd + 1) * MLA_VT_ROWS].astype(BF16)
    for g in range(F_GROUPS):
        zr_ref[0, :, g * LANES:(g + 1) * LANES] = zr[g].astype(BF16)
        zi_ref[0, :, g * LANES:(g + 1) * LANES] = zi[g].astype(BF16)
    gf_ref[0] = _silu(f_gate).astype(BF16)
    gm_ref[0] = _silu(m_gate).astype(BF16)


def _front0_call(x, ng, sc, sh, w_in, qn, w_qbt, kvn, w_k, w_vt, vone, tqt, tk, fc, tm):
    b, n, d = x.shape
    fw = jax.ShapeDtypeStruct((b, n, F_WIDTH), BF16)
    row_spec = pl.BlockSpec((1, tm, F_WIDTH), lambda bi, i: (bi, i, 0))
    vec_spec = pl.BlockSpec((1, 1, d), lambda bi, i: (bi, 0, 0))
    return pl.pallas_call(
        _front0_kernel,
        grid=(b, n // tm),
        in_specs=[
            pl.BlockSpec((1, tm, d), lambda bi, i: (bi, i, 0)),
            _full(ng.shape), vec_spec, vec_spec,
            _full(w_in.shape), _full(qn.shape), _full(w_qbt.shape), _full(kvn.shape), _full(w_k.shape),
            _full(w_vt.shape), _full(vone.shape),
            pl.BlockSpec((2, 16, tm), lambda bi, i: (0, 0, i)),
            pl.BlockSpec((3, tm, LANES), lambda bi, i: (0, i, 0)),
            _full(fc.shape),
        ],
        out_specs=[
            pl.BlockSpec((1, MLA_HEADS, LANES, tm), lambda bi, i: (bi, 0, 0, i)),
            pl.BlockSpec((1, MLA_HEADS, tm, LANES), lambda bi, i: (bi, 0, i, 0)),
            pl.BlockSpec((1, MLA_HEADS, 1, MLA_VT_ROWS, tm), lambda bi, i: (bi, 0, i, 0, 0)),
            row_spec, row_spec, row_spec, row_spec],
        out_shape=[
            jax.ShapeDtypeStruct((b, MLA_HEADS, LANES, n), BF16),
            jax.ShapeDtypeStruct((b, MLA_HEADS, n, LANES), BF16),
            jax.ShapeDtypeStruct((b, MLA_HEADS, n // tm, MLA_VT_ROWS, tm), BF16),
            fw, fw, fw, fw],
        compiler_params=_cparams(("parallel", "arbitrary")),
        name="front0",
    )(x, ng, sc, sh, w_in, qn, w_qbt, kvn, w_k, w_vt, vone, tqt, tk, fc)


def _mla_chunk(qt, k, vt, m, acc):
    s = _dot(k, qt)
    m_new = jnp.maximum(m, jnp.max(s, axis=0, keepdims=True))
    alpha = jnp.exp2(m - m_new)
    p = jnp.exp2(s - m_new).astype(BF16)
    return m_new, alpha * acc + _dot(vt, p)


def _mla_pair_out(acc0, acc1):
    o0 = acc0[0:MLA_V] / acc0[MLA_V:MLA_V + 1]
    o1 = acc1[0:MLA_V] / acc1[MLA_V:MLA_V + 1]
    return jnp.concatenate([o0, o1], axis=0).T


def _ctx_kernel(ctx_scale, x_ref, ng0_ref, sc0_ref, sh0_ref, g0_ref, ng1_ref, sc1_ref, sh1_ref,
                w_in_ref, qn_ref, w_qbt_ref, kvn_ref, w_k_ref, w_vt_ref, vone_ref, tqt_ref, tk_ref, fc_ref,
                fn_ref, w_out_ref, w_kv1_ref,
                kc_ref, vc_ref, ck1_ref, cv1_ref):
    x = x_ref[0]
    c = x.shape[0]
    qts, ks, vt, zr, zi, f_gate, m_gate = _even_front(
        x, ng0_ref[...], sc0_ref[...], sh0_ref[...], w_in_ref, qn_ref[...], w_qbt_ref, kvn_ref[...],
        w_k_ref, w_vt_ref, vone_ref[...], tqt_ref, tk_ref, fc_ref)
    pairs = []
    for hp in range(MLA_HEADS // 2):
        accs = []
        for e in range(2):
            hd = 2 * hp + e
            kb = ks[hd].astype(BF16)
            vtb = vt[hd * MLA_VT_ROWS:(hd + 1) * MLA_VT_ROWS].astype(BF16)
            kc_ref[0, hd] = kb
            vc_ref[0, hd, 0] = vtb
            m0 = jnp.full((1, c), -jnp.inf, F32)
            acc0 = jnp.zeros((MLA_VT_ROWS, c), F32)
            accs.append(_mla_chunk(qts[hd].astype(BF16), kb, vtb, m0, acc0)[1])
        pairs.append(_mla_pair_out(accs[0], accs[1]))
    a = jnp.concatenate(pairs, axis=-1)
    z = jnp.concatenate([jnp.concatenate(zr, axis=-1), jnp.concatenate(zi, axis=-1)], axis=0).astype(BF16)
    fm = _dot(fn_ref[...], z) * ctx_scale
    yf = (fm * _silu(f_gate)).astype(BF16)
    ya = (a * _silu(m_gate)).astype(BF16)
    y = _dot(yf, w_out_ref[0:F_WIDTH, :]) + _dot(ya, w_out_ref[F_WIDTH:, :])
    x1 = x + g0_ref[...] * y
    h1 = (_rms(x1, ng1_ref[...]) * (1.0 + sc1_ref[...]) + sh1_ref[...]).astype(BF16)
    kv1 = _dot(h1, w_kv1_ref[...])
    ck1_ref[0] = kv1[:, :GQA_KV].astype(BF16)
    cv1_ref[0] = kv1[:, GQA_KV:].astype(BF16)


def _ctx_call(ctx, vecs0, vecs1, w_in, qn, w_qbt, kvn, w_k, w_vt, vone, tqt, tk, fc, fn, w_out, w_kv1):
    b, c, d = ctx.shape
    ng0, sc0, sh0, g0 = vecs0
    ng1, sc1, sh1 = vecs1
    ctx_scale = 1.0 / math.sqrt(c * F_GROUP_DIM)
    consts = [ng0, sc0, sh0, g0, ng1, sc1, sh1, w_in, qn, w_qbt, kvn, w_k, w_vt, vone, tqt, tk, fc, fn, w_out,
              w_kv1]
    kv1 = jax.ShapeDtypeStruct((b, c, GQA_KV), BF16)
    kv1_spec = pl.BlockSpec((1, c, GQA_KV), lambda bi: (bi, 0, 0))
    return pl.pallas_call(
        functools.partial(_ctx_kernel, ctx_scale),
        grid=(b,),
        in_specs=[pl.BlockSpec((1, c, d), lambda bi: (bi, 0, 0))] + [_full(a.shape) for a in consts],
        out_specs=[pl.BlockSpec((1, MLA_HEADS, c, LANES), lambda bi: (bi, 0, 0, 0)),
                   pl.BlockSpec((1, MLA_HEADS, 1, MLA_VT_ROWS, c), lambda bi: (bi, 0, 0, 0, 0)),
                   kv1_spec, kv1_spec],
        out_shape=[jax.ShapeDtypeStruct((b, MLA_HEADS, c, LANES), BF16),
                   jax.ShapeDtypeStruct((b, MLA_HEADS, 1, MLA_VT_ROWS, c), BF16), kv1, kv1],
        compiler_params=_cparams(("arbitrary",)),
        name="ctx",
    )(ctx, *consts)


def _attn0_kernel(qt_ref, kc_ref, vct_ref, kl_ref, vlt_ref, gm_ref, o_ref, sa_ref, sb_ref):
    tq = qt_ref.shape[-1]
    nch, _, tk = vlt_ref.shape[2:]

    def scores(e, ci, s_ref):
        k = kl_ref[0, e, pl.ds(pl.multiple_of(ci * tk, tk), tk), :]
        s = _dot(k, qt_ref[0, e])
        s_ref[e] = s
        return jnp.max(s, axis=0, keepdims=True)

    def update(e, ci, s_ref, mc, m, acc):
        m_new = jnp.maximum(m, mc)
        alpha = jnp.exp2(m - m_new)
        p = jnp.exp2(s_ref[e] - m_new).astype(BF16)
        return m_new, alpha * acc + _dot(vlt_ref[0, e, ci], p)

    state = []
    for e in range(2):
        m0 = jnp.full((1, tq), -jnp.inf, F32)
        acc0 = jnp.zeros((MLA_VT_ROWS, tq), F32)
        state += list(_mla_chunk(qt_ref[0, e], kc_ref[0, e], vct_ref[0, e, 0], m0, acc0))
    mca = [scores(e, 0, sa_ref) for e in range(2)]

    def body(i, carry):
        mca = list(carry[0:2])
        st = list(carry[2:6])
        c0 = 2 * i
        mcb = [scores(e, c0 + 1, sb_ref) for e in range(2)]
        for e in range(2):
            st[2 * e], st[2 * e + 1] = update(e, c0, sa_ref, mca[e], st[2 * e], st[2 * e + 1])
        c2 = jnp.minimum(c0 + 2, nch - 1)
        mca = [scores(e, c2, sa_ref) for e in range(2)]
        for e in range(2):
            st[2 * e], st[2 * e + 1] = update(e, c0 + 1, sb_ref, mcb[e], st[2 * e], st[2 * e + 1])
        return tuple(mca + st)

    carry = lax.fori_loop(0, nch // 2, body, tuple(mca + state))
    o_ref[0] = (_mla_pair_out(carry[3], carry[5]) * gm_ref[0].astype(F32)).astype(BF16)


def _attn0_call(qt, kc, vct, kl, vlt, gm, tq):
    b, h, _, n = qt.shape
    c = kc.shape[2]
    nch, _, tk = vlt.shape[2:]
    assert nch % 2 == 0
    o_spec = pl.BlockSpec((1, tq, LANES), lambda bi, hp, i: (bi, i, hp))
    return pl.pallas_call(
        _attn0_kernel,
        grid=(b, h // 2, n // tq),
        in_specs=[
            pl.BlockSpec((1, 2, LANES, tq), lambda bi, hp, i: (bi, hp, 0, i)),
            pl.BlockSpec((1, 2, c, LANES), lambda bi, hp, i: (bi, hp, 0, 0)),
            pl.BlockSpec((1, 2, 1, MLA_VT_ROWS, c), lambda bi, hp, i: (bi, hp, 0, 0, 0)),
            pl.BlockSpec((1, 2, n, LANES), lambda bi, hp, i: (bi, hp, 0, 0)),
            pl.BlockSpec((1, 2, nch, MLA_VT_ROWS, tk), lambda bi, hp, i: (bi, hp, 0, 0, 0)),
            o_spec],
        out_specs=o_spec,
        out_shape=jax.ShapeDtypeStruct((b, n, MLA_WIDTH), BF16),
        scratch_shapes=[pltpu.VMEM((2, tk, tq), F32), pltpu.VMEM((2, tk, tq), F32)],
        compiler_params=_cparams(("parallel", "parallel", "arbitrary")),
        name="attn0",
    )(qt, kc, vct, kl, vlt, gm)


def _fft_a_kernel(t2, n1, zr_ref, zi_ref, fa_ref, tw_ref, tr_ref, ti_ref):
    z = jnp.concatenate([zr_ref[0], zi_ref[0]], axis=0)
    a = _dot(fa_ref[...], z)
    for j in range(t2):
        c = tw_ref[0, j]
        s = tw_ref[1, j]
        for g in range(F_GROUPS):
            lo = j * F_WIDTH + g * LANES
            ar = a[0:n1, lo:lo + LANES]
            ai = a[n1:2 * n1, lo:lo + LANES]
            tr_ref[0, :, lo:lo + LANES] = (ar * c + ai * s).astype(BF16)
            ti_ref[0, :, lo:lo + LANES] = (ai * c - ar * s).astype(BF16)


def _fft_a_call(zr, zi, fa, tw, t2):
    b, n1, w = zr.shape
    spec = pl.BlockSpec((1, n1, t2 * F_WIDTH), lambda bi, j: (bi, 0, j))
    out = jax.ShapeDtypeStruct((b, n1, w), BF16)
    return pl.pallas_call(
        functools.partial(_fft_a_kernel, t2, n1),
        grid=(b, w // (t2 * F_WIDTH)),
        in_specs=[spec, spec, _full(fa.shape), pl.BlockSpec((2, t2, n1, LANES), lambda bi, j: (0, j, 0, 0))],
        out_specs=[spec, spec],
        out_shape=[out, out],
        compiler_params=_cparams(("parallel", "arbitrary")),
        name="fft_a",
    )(zr, zi, fa, tw)


def _fft_c_kernel(t1, tr_ref, ti_ref, fcs_ref, gf_ref, o_ref):
    for k in range(t1):
        rhs = jnp.concatenate([tr_ref[0, k], ti_ref[0, k]], axis=0)
        lo = k * F_WIDTH
        out = _dot(fcs_ref[...], rhs)
        o_ref[0, :, lo:lo + F_WIDTH] = (out * gf_ref[0, :, lo:lo + F_WIDTH].astype(F32)).astype(BF16)


def _fft_c_call(tr, ti, fcs, gf, t1):
    b, n1, r, w = tr.shape
    t_spec = pl.BlockSpec((1, t1, r, w), lambda bi, i: (bi, i, 0, 0))
    g_spec = pl.BlockSpec((1, r, t1 * w), lambda bi, i: (bi, 0, i))
    return pl.pallas_call(
        functools.partial(_fft_c_kernel, t1),
        grid=(b, n1 // t1),
        in_specs=[t_spec, t_spec, _full(fcs.shape), g_spec],
        out_specs=g_spec,
        out_shape=jax.ShapeDtypeStruct((b, r, n1 * w), BF16),
        compiler_params=_cparams(("parallel", "arbitrary")),
        name="fft_c",
    )(tr, ti, fcs, gf)


def _mid_kernel(fm_ref, ag_ref, x_ref, g0_ref, w_out_ref, ng_ref, sc_ref, sh_ref, w_in_ref, tq_ref, tk_ref,
                x1_ref, q_ref, k_ref, v_ref, sg_ref):
    y = _dot(fm_ref[0], w_out_ref[0:F_WIDTH, :]) + _dot(ag_ref[0], w_out_ref[F_WIDTH:, :])
    x1 = x_ref[0] + g0_ref[0] * y
    x1_ref[0] = x1
    h = (_rms(x1, ng_ref[...]) * (1.0 + sc_ref[0]) + sh_ref[0]).astype(BF16)
    for c in range(GQA_Q // LANES):
        lo = c * LANES
        q_ref[0, :, lo:lo + LANES] = _rope(_dot(h, w_in_ref[:, O_Q + lo:O_Q + lo + LANES]), tq_ref, 32).astype(BF16)
    for c in range(GQA_KV // LANES):
        lo = c * LANES
        k_ref[0, :, lo:lo + LANES] = _rope(_dot(h, w_in_ref[:, O_K + lo:O_K + lo + LANES]), tk_ref, 32).astype(BF16)
    v_ref[0] = _dot(h, w_in_ref[:, O_V:O_G]).astype(BF16)
    sg_ref[0] = _silu(_dot(h, w_in_ref[:, O_G:O_END])).astype(BF16)


def _mid_call(fm, ag, x, g0, w_out, ng, sc, sh, w_in, tq, tk, tm):
    b, n, d = x.shape
    row = lambda w: pl.BlockSpec((1, tm, w), lambda bi, i: (bi, i, 0))
    vec_spec = pl.BlockSpec((1, 1, d), lambda bi, i: (bi, 0, 0))
    tab_spec = pl.BlockSpec((3, tm, LANES), lambda bi, i: (0, i, 0))
    return pl.pallas_call(
        _mid_kernel,
        grid=(b, n // tm),
        in_specs=[row(F_WIDTH), row(MLA_WIDTH), row(d), vec_spec, _full(w_out.shape), _full(ng.shape),
                  vec_spec, vec_spec, _full(w_in.shape), tab_spec, tab_spec],
        out_specs=[row(d), row(GQA_Q), row(GQA_KV), row(GQA_KV), row(GQA_Q)],
        out_shape=[jax.ShapeDtypeStruct((b, n, d), F32), jax.ShapeDtypeStruct((b, n, GQA_Q), BF16),
                   jax.ShapeDtypeStruct((b, n, GQA_KV), BF16), jax.ShapeDtypeStruct((b, n, GQA_KV), BF16),
                   jax.ShapeDtypeStruct((b, n, GQA_Q), BF16)],
        compiler_params=_cparams(("parallel", "arbitrary")),
        name="mid",
    )(fm, ag, x, g0, w_out, ng, sc, sh, w_in, tq, tk)


def _attn1_kernel(nc, sink_ref, q_ref, kp_ref, kc_ref, kn_ref, vp_ref, vc_ref, vn_ref, ck_ref, cv_ref,
                  sg_ref, x1_ref, g1_ref, w_out_ref, fg_ref, o_ref):
    i = pl.program_id(1)
    last = pl.num_programs(1) - 1
    nk = nc + 3 * BLOCK
    rows = GQA_GROUP * BLOCK
    r = lax.broadcasted_iota(jnp.int32, (rows, nk), 0) & (BLOCK - 1)
    col = lax.broadcasted_iota(jnp.int32, (rows, nk), 1) - nc
    kpos = (i - 1) * BLOCK + col
    dist = BLOCK + r - col
    valid = (col < 0) | ((dist <= WINDOW) & (dist >= -WINDOW) & (kpos >= 0) & (kpos < (last + 1) * BLOCK))
    lane = lax.broadcasted_iota(jnp.int32, (BLOCK, LANES), 1)
    low = lane < GQA_HEAD_DIM

    k_all = jnp.concatenate([ck_ref[0], kp_ref[0], kc_ref[0], kn_ref[0]], axis=0)
    v_all = jnp.concatenate([cv_ref[0], vp_ref[0], vc_ref[0], vn_ref[0]], axis=0)
    chunks = [None] * (GQA_Q // LANES)
    for g in range(GQA_KV_HEADS):
        ch = g // 2
        kg = k_all[:, ch * LANES:(ch + 1) * LANES]
        vg = v_all[:, ch * LANES:(ch + 1) * LANES]
        g_low = (g % 2) == 0
        keep = low if g_low else jnp.logical_not(low)
        qs = []
        sinks = []
        for p2 in range(2):
            qp = q_ref[0, :, (2 * g + p2) * LANES:(2 * g + p2 + 1) * LANES].astype(F32)
            qsw = pltpu.roll(qp, GQA_HEAD_DIM, 1)
            q_lo, q_hi = (qp, qsw) if g_low else (qsw, qp)
            zero = jnp.zeros_like(qp)
            qs.append(jnp.where(keep, q_lo, zero).astype(BF16))
            qs.append(jnp.where(keep, q_hi, zero).astype(BF16))
            sinks.append(jnp.full((BLOCK, 1), sink_ref[4 * g + 2 * p2], F32))
            sinks.append(jnp.full((BLOCK, 1), sink_ref[4 * g + 2 * p2 + 1], F32))
        q4 = jnp.concatenate(qs, axis=0)
        sk = jnp.concatenate(sinks, axis=0)
        s = jnp.where(valid, _dot_nt(q4, kg), NEG_BIG)
        m = jnp.maximum(jnp.max(s, axis=-1, keepdims=True), sk)
        p = jnp.exp(s - m)
        l = jnp.sum(p, axis=-1, keepdims=True) + jnp.exp(sk - m)
        res = _dot(p.astype(BF16), vg) / l
        for p2 in range(2):
            r0 = res[(2 * p2) * BLOCK:(2 * p2 + 1) * BLOCK]
            r1 = res[(2 * p2 + 1) * BLOCK:(2 * p2 + 2) * BLOCK]
            if g_low:
                chunks[2 * g + p2] = jnp.where(low, r0, pltpu.roll(r1, GQA_HEAD_DIM, 1))
            else:
                chunks[2 * g + p2] = jnp.where(low, pltpu.roll(r0, GQA_HEAD_DIM, 1), r1)
    o = jnp.concatenate(chunks, axis=-1)
    og = (o * sg_ref[0].astype(F32)).astype(BF16)
    y = _dot(og, w_out_ref[...])
    x2 = x1_ref[0] + g1_ref[0] * y
    o_ref[0] = _rms(x2, fg_ref[...])


def _attn1_call(sink, q, k, v, ck, cv, sg, x1, g1, w_out, fg):
    b, n, d = x1.shape
    nc = ck.shape[1]
    nb = n // BLOCK
    row = lambda w: pl.BlockSpec((1, BLOCK, w), lambda bi, i: (bi, i, 0))
    prev = pl.BlockSpec((1, BLOCK, GQA_KV), lambda bi, i: (bi, jnp.maximum(i - 1, 0), 0))
    nxt = pl.BlockSpec((1, BLOCK, GQA_KV), lambda bi, i: (bi, jnp.minimum(i + 1, nb - 1), 0))
    c_spec = pl.BlockSpec((1, nc, GQA_KV), lambda bi, i: (bi, 0, 0))
    vec_spec = pl.BlockSpec((1, 1, d), lambda bi, i: (bi, 0, 0))
    return pl.pallas_call(
        functools.partial(_attn1_kernel, nc),
        grid=(b, nb),
        in_specs=[pl.BlockSpec(memory_space=pltpu.SMEM), row(GQA_Q), prev, row(GQA_KV), nxt,
                  prev, row(GQA_KV), nxt, c_spec, c_spec, row(GQA_Q), row(d), vec_spec,
                  _full(w_out.shape), _full(fg.shape)],
        out_specs=row(d),
        out_shape=jax.ShapeDtypeStruct((b, n, d), F32),
        compiler_params=_cparams(("parallel", "arbitrary")),
        name="attn1",
    )(sink, q, k, k, k, v, v, v, ck, cv, sg, x1, g1, w_out, fg)


def _prep_even_weights(w_in, w_qb, w_kvb):
    d = w_in.shape[0]
    f_in, f_gate, q_a, kv_a, k_pe, m_gate = jnp.split(
        w_in, np.cumsum([F_WIDTH, F_WIDTH, MLA_Q_RANK, MLA_KV_RANK, MLA_ROPE]).tolist(), axis=1)
    kpe_blk = jnp.concatenate([jnp.zeros((d, MLA_NOPE), F32), k_pe, jnp.zeros((d, 32), F32)], axis=1)
    w_in_p = jnp.concatenate([f_in, f_gate, m_gate, q_a, kv_a, kpe_blk], axis=1).astype(BF16)
    qb = w_qb.reshape(MLA_Q_RANK, MLA_HEADS, MLA_NOPE + MLA_ROPE)
    w_qbt = jnp.pad(qb, ((0, 0), (0, 0), (0, 32))).reshape(MLA_Q_RANK, MLA_HEADS * LANES).T.astype(BF16)
    kvb = w_kvb.reshape(MLA_KV_RANK, MLA_HEADS, MLA_NOPE + MLA_V)
    w_k = jnp.pad(kvb[:, :, :MLA_NOPE], ((0, 0), (0, 0), (0, 64))).reshape(MLA_KV_RANK, MLA_HEADS * LANES)
    w_vt = jnp.pad(kvb[:, :, MLA_NOPE:], ((0, 0), (0, 0), (0, MLA_VT_ROWS - MLA_V)))
    w_vt = w_vt.reshape(MLA_KV_RANK, MLA_HEADS * MLA_VT_ROWS).T.astype(BF16)
    vone = np.zeros((MLA_HEADS * MLA_VT_ROWS, 1), np.float32)
    vone[MLA_V::MLA_VT_ROWS] = 1.0
    return w_in_p, w_qbt, w_k.astype(BF16), w_vt, jnp.asarray(vone)


def _fft_consts(n, c):
    n1 = n // LANES
    c1, s1 = _dft_cs(n1)
    fa = np.block([[c1, s1], [-s1, c1]])
    c2, s2 = _dft_cs(LANES)
    fcs = np.concatenate([c2, s2], axis=1) / math.sqrt(n * F_GROUP_DIM)
    fc = np.concatenate([c2, -s2], axis=1)
    r = np.arange(LANES, dtype=np.int64)[:, None]
    k1 = np.arange(n1, dtype=np.int64)[None, :]
    ang = 2.0 * np.pi * ((r * k1) % n).astype(np.float64) / n
    tw = np.stack([np.cos(ang), np.sin(ang)])
    tw = np.broadcast_to(tw[..., None], (2, LANES, n1, LANES))
    cn, sn = _dft_cs(c)
    fn = np.concatenate([cn, sn], axis=1)
    as32 = lambda a: jnp.asarray(np.ascontiguousarray(a, dtype=np.float32))
    return as32(fa).astype(BF16), as32(fcs).astype(BF16), as32(fc).astype(BF16), as32(tw), as32(fn).astype(BF16)


def _pick_tile(n, pref):
    t = pref
    while n % t:
        t //= 2
    return t


def kernel(x, c, ctx, c_ctx, w_mod, b_mod, norm_g, e_w_in, e_q_norm, e_w_qb, e_kv_norm, e_w_kvb, e_w_out,
           o_w_in, o_sink, o_w_out, final_g):
    b, n, d = x.shape
    nc = ctx.shape[1]
    assert d == D_MODEL and n % (LANES * 8) == 0 and nc % LANES == 0 and b <= 7

    svec = jnp.zeros((8, d), F32).at[:b].set(c).at[b].set(c_ctx)
    mod = _mod_call(svec, w_mod, b_mod)
    sh, sc, gt = mod[:, :, :d], mod[:, :, d:2 * d], mod[:, :, 2 * d:]
    lat = lambda t, l: t[l, :b].reshape(b, 1, d)
    cvec = lambda t, l: t[l, b].reshape(1, d)
    ng0, ng1 = norm_g[0].reshape(1, d), norm_g[1].reshape(1, d)

    cos_m, sin_m = _axial_rope_tables(n, MLA_ROPE)
    qs = MLA_SCALE * LOG2E
    tqt0 = jnp.stack([cos_m.T, sin_m.T]) * qs
    tk0 = _mla_rope_tab(cos_m, sin_m)
    one_c, zero_c = jnp.ones((nc, MLA_ROPE // 2), F32), jnp.zeros((nc, MLA_ROPE // 2), F32)
    tqt0c = jnp.stack([one_c.T, zero_c.T]) * qs
    tk0c = _mla_rope_tab(one_c, zero_c)
    cos_g, sin_g = _axial_rope_tables(n, GQA_HEAD_DIM)
    tq1 = _gqa_rope_tab(cos_g, sin_g, GQA_SCALE)
    tk1 = _gqa_rope_tab(cos_g, sin_g, 1.0)
    fa, fcs, fc, tw, fn = _fft_consts(n, nc)

    w_in0, w_qbt0, w_k0, w_vt0, vone = _prep_even_weights(e_w_in[0], e_w_qb[0], e_w_kvb[0])
    qn0 = e_q_norm[0].reshape(1, MLA_Q_RANK)
    kvn0 = e_kv_norm[0].reshape(1, MLA_KV_RANK)
    w_out0 = e_w_out[0].astype(BF16)
    w_in1 = o_w_in[0].astype(BF16)
    w_out1 = o_w_out[0].astype(BF16)

    kc, vct, ck1, cv1 = _ctx_call(
        ctx, (ng0, cvec(sc, 0), cvec(sh, 0), cvec(gt, 0)), (ng1, cvec(sc, 1), cvec(sh, 1)),
        w_in0, qn0, w_qbt0, kvn0, w_k0, w_vt0, vone, tqt0c, tk0c, fc, fn, w_out0, w_in1[:, O_K:O_G])

    tm = _pick_tile(n, 512)
    qt0, k0, vt0, zr, zi, gf, gm = _front0_call(
        x, ng0, lat(sc, 0), lat(sh, 0), w_in0, qn0, w_qbt0, kvn0, w_k0, w_vt0, vone, tqt0, tk0, fc, tm)

    ag = _attn0_call(qt0, kc, vct, k0, vt0, gm, _pick_tile(n, 256))

    n1 = n // LANES
    tr, ti = _fft_a_call(zr.reshape(b, n1, LANES * F_WIDTH), zi.reshape(b, n1, LANES * F_WIDTH), fa, tw, 8)
    fmg = _fft_c_call(tr.reshape(b, n1, LANES, F_WIDTH), ti.reshape(b, n1, LANES, F_WIDTH), fcs,
                      gf.reshape(b, LANES, n1 * F_WIDTH), _pick_tile(n1, 8))
    fmg = fmg.reshape(b, n, F_WIDTH)

    tm1 = _pick_tile(n, 256)
    x1, q1, k1, v1, sg = _mid_call(fmg, ag, x, lat(gt, 0), w_out0, ng1, lat(sc, 1), lat(sh, 1), w_in1,
                                   tq1, tk1, tm1)

    return _attn1_call(o_sink[0], q1, k1, v1, ck1, cv1, sg, x1, lat(gt, 1), w_out1, final_g.reshape(1, d))
```

```python
import functools
import math

import numpy as np
import jax
import jax.numpy as jnp
from jax import lax
from jax.experimental import pallas as pl
from jax.experimental.pallas import tpu as pltpu

F32 = jnp.float32
BF16 = jnp.bfloat16

D_MODEL = 1024
GRID_W = 64
EPS = 1e-6
ROPE_BASE = 10000.0
LANES = 128

F_GROUPS = 4
F_GROUP_DIM = 128
F_WIDTH = F_GROUPS * F_GROUP_DIM

MLA_HEADS = 8
MLA_NOPE = 64
MLA_ROPE = 32
MLA_V = 64
MLA_Q_RANK = 384
MLA_KV_RANK = 256
MLA_WIDTH = MLA_HEADS * MLA_V
MLA_SCALE = 1.0 / math.sqrt(MLA_NOPE + MLA_ROPE)
LOG2E = math.log2(math.e)
MLA_VT_ROWS = 80

GQA_HEADS = 16
GQA_KV_HEADS = 4
GQA_GROUP = GQA_HEADS // GQA_KV_HEADS
GQA_HEAD_DIM = 64
WINDOW = 128
BLOCK = 128
GQA_Q = GQA_HEADS * GQA_HEAD_DIM
GQA_KV = GQA_KV_HEADS * GQA_HEAD_DIM
GQA_SCALE = 1.0 / math.sqrt(GQA_HEAD_DIM)
GQA_VT_ROWS = 80

E_FIN, E_FGATE, E_MGATE, E_QA, E_KVA, E_KPE, E_END = 0, 512, 1024, 1536, 1920, 2176, 2304
O_Q, O_K, O_V, O_G, O_END = 0, 1024, 1280, 1536, 2560

NEG_BIG = -1e30
VMEM_LIMIT = 56 * 1024 * 1024


def _cparams(sem):
    return pltpu.CompilerParams(dimension_semantics=sem, vmem_limit_bytes=VMEM_LIMIT)


def _dot(a, b):
    return jnp.dot(a, b, preferred_element_type=F32)


def _dot_nt(a, b):
    return lax.dot_general(a, b, (((1,), (1,)), ((), ())), preferred_element_type=F32)


def _rms(x, g):
    return x * lax.rsqrt(jnp.mean(x * x, axis=-1, keepdims=True) + EPS) * g


def _silu(x):
    return x * jax.nn.sigmoid(x)


def _rope(x, tab_ref, shift):
    up = pltpu.roll(x, LANES - shift, 1)
    dn = pltpu.roll(x, shift, 1)
    return x * tab_ref[0] + up * tab_ref[1] + dn * tab_ref[2]


def _full(shape):
    nd = len(shape)
    return pl.BlockSpec(shape, lambda *_: (0,) * nd)


def _axial_rope_tables(n, rot_dim):
    rows = n // GRID_W
    row = jnp.broadcast_to(jnp.arange(rows)[:, None], (rows, GRID_W)).reshape(-1).astype(F32)
    col = jnp.broadcast_to(jnp.arange(GRID_W)[None, :], (rows, GRID_W)).reshape(-1).astype(F32)
    nf = rot_dim // 4
    inv = ROPE_BASE ** (-jnp.arange(nf, dtype=F32) / nf)
    ang = jnp.concatenate([row[:, None] * inv, col[:, None] * inv], axis=-1)
    return jnp.cos(ang), jnp.sin(ang)


def _mla_rope_tab(cos, sin):
    n = cos.shape[0]
    z = lambda w: jnp.zeros((n, w), F32)
    cf = jnp.concatenate([jnp.ones((n, MLA_NOPE), F32), cos, cos, z(32)], axis=-1)
    s1 = jnp.concatenate([z(MLA_NOPE), -sin, z(16), z(32)], axis=-1)
    s2 = jnp.concatenate([z(MLA_NOPE), z(16), sin, z(32)], axis=-1)
    return jnp.stack([cf, s1, s2])


def _gqa_rope_tab(cos, sin):
    z = jnp.zeros_like(sin)
    cf = jnp.concatenate([cos, cos, z, z], axis=-1)
    s1 = jnp.concatenate([-sin, z, z, z], axis=-1)
    s2 = jnp.concatenate([z, sin, z, z], axis=-1)
    return jnp.stack([cf, s1, s2])


def _dft_cs(n):
    idx = np.arange(n, dtype=np.int64)
    ang = 2.0 * np.pi * ((idx[:, None] * idx[None, :]) % n).astype(np.float64) / n
    return np.cos(ang), np.sin(ang)


def _mod_kernel(s_ref, w_ref, b_ref, o_ref):
    s = _silu(s_ref[...])
    o_ref[0] = _dot(s.astype(BF16), w_ref[0].astype(BF16)) + b_ref[0]


def _mod_call(svec, w_mod, b_mod):
    depth, d, d3 = w_mod.shape
    tn = 768
    return pl.pallas_call(
        _mod_kernel,
        grid=(depth, d3 // tn),
        in_specs=[
            pl.BlockSpec((8, d), lambda l, j: (0, 0)),
            pl.BlockSpec((1, d, tn), lambda l, j: (l, 0, j)),
            pl.BlockSpec((1, 1, tn), lambda l, j: (l, 0, j)),
        ],
        out_specs=pl.BlockSpec((1, 8, tn), lambda l, j: (l, 0, j)),
        out_shape=jax.ShapeDtypeStruct((depth, 8, d3), F32),
        compiler_params=_cparams(("arbitrary", "arbitrary")),
        name="mod",
    )(svec, w_mod, b_mod.reshape(depth, 1, d3))


def _even_front(x, ng, sc, sh, w_in_ref, qn, w_qbt_ref, kvn, w_k_ref, w_vt_ref, vone, tqt_ref, tk_ref, fc_ref):
    m = x.shape[0]
    h = (_rms(x, ng) * (1.0 + sc) + sh).astype(BF16)
    f_in = _dot(h, w_in_ref[:, E_FIN:E_FGATE])
    f_gate = _dot(h, w_in_ref[:, E_FGATE:E_MGATE])
    m_gate = _dot(h, w_in_ref[:, E_MGATE:E_QA])
    q_a = _dot(h, w_in_ref[:, E_QA:E_KVA])
    kv_a = _dot(h, w_in_ref[:, E_KVA:E_KPE])
    kpe = _dot(h, w_in_ref[:, E_KPE:E_END])
    qh = _rms(q_a, qn).astype(BF16)
    ch = _rms(kv_a, kvn).astype(BF16)
    kpe_r = _rope(kpe, tk_ref, 16)
    qt = _dot_nt(w_qbt_ref[...], qh)
    cos = tqt_ref[0]
    sin = tqt_ref[1]
    pad = jnp.zeros((LANES - MLA_NOPE - MLA_ROPE, m), F32)
    qts, ks = [], []
    for hd in range(MLA_HEADS):
        lo = hd * LANES
        x1 = qt[lo + MLA_NOPE:lo + MLA_NOPE + 16]
        x2 = qt[lo + MLA_NOPE + 16:lo + MLA_NOPE + 32]
        qts.append(jnp.concatenate(
            [qt[lo:lo + MLA_NOPE] * (MLA_SCALE * LOG2E), x1 * cos - x2 * sin, x2 * cos + x1 * sin, pad], axis=0))
        ks.append(_dot(ch, w_k_ref[:, lo:lo + LANES]) + kpe_r)
    vt = _dot_nt(w_vt_ref[...], ch) + vone
    zr, zi = [], []
    fb = f_in.astype(BF16)
    for g in range(F_GROUPS):
        z = _dot(fb[:, g * LANES:(g + 1) * LANES], fc_ref[...])
        zr.append(z[:, :LANES])
        zi.append(z[:, LANES:])
    return qts, ks, vt, zr, zi, f_gate, m_gate


def _front0_kernel(x_ref, ng_ref, sc_ref, sh_ref, w_in_ref, qn_ref, w_qbt_ref, kvn_ref, w_k_ref, w_vt_ref,
                   vone_ref, tqt_ref, tk_ref, fc_ref, q_ref, k_ref, v_ref, zr_ref, zi_ref, gf_ref, gm_ref):
    qts, ks, vt, zr, zi, f_gate, m_gate = _even_front(
        x_ref[0], ng_ref[...], sc_ref[0], sh_ref[0], w_in_ref, qn_ref[...], w_qbt_ref, kvn_ref[...],
        w_k_ref, w_vt_ref, vone_ref[...], tqt_ref, tk_ref, fc_ref)
    for hd in range(MLA_HEADS):
        q_ref[0, hd] = qts[hd].astype(BF16)
        k_ref[0, hd] = ks[hd].astype(BF16)
        v_ref[0, hd, 0] = vt[hd * MLA_VT_ROWS:(hd + 1) * MLA_VT_ROWS].astype(BF16)
    for g in range(F_GROUPS):
        zr_ref[0, :, g * LANES:(g + 1) * LANES] = zr[g].astype(BF16)
        zi_ref[0, :, g * LANES:(g + 1) * LANES] = zi[g].astype(BF16)
    gf_ref[0] = _silu(f_gate).astype(BF16)
    gm_ref[0] = _silu(m_gate).astype(BF16)


def _front0_call(x, ng, sc, sh, w_in, qn, w_qbt, kvn, w_k, w_vt, vone, tqt, tk, fc, tm):
    b, n, d = x.shape
    fw = jax.ShapeDtypeStruct((b, n, F_WIDTH), BF16)
    row_spec = pl.BlockSpec((1, tm, F_WIDTH), lambda bi, i: (bi, i, 0))
    vec_spec = pl.BlockSpec((1, 1, d), lambda bi, i: (bi, 0, 0))
    return pl.pallas_call(
        _front0_kernel,
        grid=(b, n // tm),
        in_specs=[
            pl.BlockSpec((1, tm, d), lambda bi, i: (bi, i, 0)),
            _full(ng.shape), vec_spec, vec_spec,
            _full(w_in.shape), _full(qn.shape), _full(w_qbt.shape), _full(kvn.shape), _full(w_k.shape),
            _full(w_vt.shape), _full(vone.shape),
            pl.BlockSpec((2, 16, tm), lambda bi, i: (0, 0, i)),
            pl.BlockSpec((3, tm, LANES), lambda bi, i: (0, i, 0)),
            _full(fc.shape),
        ],
        out_specs=[
            pl.BlockSpec((1, MLA_HEADS, LANES, tm), lambda bi, i: (bi, 0, 0, i)),
            pl.BlockSpec((1, MLA_HEADS, tm, LANES), lambda bi, i: (bi, 0, i, 0)),
            pl.BlockSpec((1, MLA_HEADS, 1, MLA_VT_ROWS, tm), lambda bi, i: (bi, 0, i, 0, 0)),
            row_spec, row_spec, row_spec, row_spec],
        out_shape=[
            jax.ShapeDtypeStruct((b, MLA_HEADS, LANES, n), BF16),
            jax.ShapeDtypeStruct((b, MLA_HEADS, n, LANES), BF16),
            jax.ShapeDtypeStruct((b, MLA_HEADS, n // tm, MLA_VT_ROWS, tm), BF16),
            fw, fw, fw, fw],
        compiler_params=_cparams(("parallel", "arbitrary")),
        name="front0",
    )(x, ng, sc, sh, w_in, qn, w_qbt, kvn, w_k, w_vt, vone, tqt, tk, fc)


def _mla_chunk(qt, k, vt, m, acc):
    s = _dot(k, qt)
    m_new = jnp.maximum(m, jnp.max(s, axis=0, keepdims=True))
    alpha = jnp.exp2(m - m_new)
    p = jnp.exp2(s - m_new).astype(BF16)
    return m_new, alpha * acc + _dot(vt, p)


def _mla_pair_out(acc0, acc1):
    o0 = acc0[0:MLA_V] / acc0[MLA_V:MLA_V + 1]
    o1 = acc1[0:MLA_V] / acc1[MLA_V:MLA_V + 1]
    return jnp.concatenate([o0, o1], axis=0).T


def _ctx_kernel(ctx_scale, x_ref, ng0_ref, sc0_ref, sh0_ref, g0_ref, ng1_ref, sc1_ref, sh1_ref,
                w_in_ref, qn_ref, w_qbt_ref, kvn_ref, w_k_ref, w_vt_ref, vone_ref, tqt_ref, tk_ref, fc_ref,
                fn_ref, w_out_ref, w_k1_ref, w_vt1_ref, vone1_ref,
                kc_ref, vc_ref, ck1_ref, cv1_ref):
    x = x_ref[0]
    c = x.shape[0]
    qts, ks, vt, zr, zi, f_gate, m_gate = _even_front(
        x, ng0_ref[...], sc0_ref[...], sh0_ref[...], w_in_ref, qn_ref[...], w_qbt_ref, kvn_ref[...],
        w_k_ref, w_vt_ref, vone_ref[...], tqt_ref, tk_ref, fc_ref)
    pairs = []
    for hp in range(MLA_HEADS // 2):
        accs = []
        for e in range(2):
            hd = 2 * hp + e
            kb = ks[hd].astype(BF16)
            vtb = vt[hd * MLA_VT_ROWS:(hd + 1) * MLA_VT_ROWS].astype(BF16)
            kc_ref[0, hd] = kb
            vc_ref[0, hd, 0] = vtb
            m0 = jnp.full((1, c), -jnp.inf, F32)
            acc0 = jnp.zeros((MLA_VT_ROWS, c), F32)
            accs.append(_mla_chunk(qts[hd].astype(BF16), kb, vtb, m0, acc0)[1])
        pairs.append(_mla_pair_out(accs[0], accs[1]))
    a = jnp.concatenate(pairs, axis=-1)
    z = jnp.concatenate([jnp.concatenate(zr, axis=-1), jnp.concatenate(zi, axis=-1)], axis=0).astype(BF16)
    fm = _dot(fn_ref[...], z) * ctx_scale
    yf = (fm * _silu(f_gate)).astype(BF16)
    ya = (a * _silu(m_gate)).astype(BF16)
    y = _dot(yf, w_out_ref[0:F_WIDTH, :]) + _dot(ya, w_out_ref[F_WIDTH:, :])
    x1 = x + g0_ref[...] * y
    h1 = (_rms(x1, ng1_ref[...]) * (1.0 + sc1_ref[...]) + sh1_ref[...]).astype(BF16)
    vt1 = _dot_nt(w_vt1_ref[...], h1) + vone1_ref[...]
    for g in range(GQA_KV_HEADS):
        ck1_ref[0, g] = _dot(h1, w_k1_ref[:, g * LANES:(g + 1) * LANES]).astype(BF16)
        cv1_ref[0, g] = vt1[g * GQA_VT_ROWS:(g + 1) * GQA_VT_ROWS].astype(BF16)


def _ctx_call(ctx, vecs0, vecs1, w_in, qn, w_qbt, kvn, w_k, w_vt, vone, tqt, tk, fc, fn, w_out, w_k1, w_vt1,
              vone1):
    b, c, d = ctx.shape
    ng0, sc0, sh0, g0 = vecs0
    ng1, sc1, sh1 = vecs1
    ctx_scale = 1.0 / math.sqrt(c * F_GROUP_DIM)
    consts = [ng0, sc0, sh0, g0, ng1, sc1, sh1, w_in, qn, w_qbt, kvn, w_k, w_vt, vone, tqt, tk, fc, fn, w_out,
              w_k1, w_vt1, vone1]
    return pl.pallas_call(
        functools.partial(_ctx_kernel, ctx_scale),
        grid=(b,),
        in_specs=[pl.BlockSpec((1, c, d), lambda bi: (bi, 0, 0))] + [_full(a.shape) for a in consts],
        out_specs=[pl.BlockSpec((1, MLA_HEADS, c, LANES), lambda bi: (bi, 0, 0, 0)),
                   pl.BlockSpec((1, MLA_HEADS, 1, MLA_VT_ROWS, c), lambda bi: (bi, 0, 0, 0, 0)),
                   pl.BlockSpec((1, GQA_KV_HEADS, c, LANES), lambda bi: (bi, 0, 0, 0)),
                   pl.BlockSpec((1, GQA_KV_HEADS, GQA_VT_ROWS, c), lambda bi: (bi, 0, 0, 0))],
        out_shape=[jax.ShapeDtypeStruct((b, MLA_HEADS, c, LANES), BF16),
                   jax.ShapeDtypeStruct((b, MLA_HEADS, 1, MLA_VT_ROWS, c), BF16),
                   jax.ShapeDtypeStruct((b, GQA_KV_HEADS, c, LANES), BF16),
                   jax.ShapeDtypeStruct((b, GQA_KV_HEADS, GQA_VT_ROWS, c), BF16)],
        compiler_params=_cparams(("arbitrary",)),
        name="ctx",
    )(ctx, *consts)


def _attn0_kernel(qt_ref, kc_ref, vct_ref, kl_ref, vlt_ref, gm_ref, o_ref, sa_ref, sb_ref):
    tq = qt_ref.shape[-1]
    nch, _, tk = vlt_ref.shape[2:]

    def scores(e, ci, s_ref):
        k = kl_ref[0, e, pl.ds(pl.multiple_of(ci * tk, tk), tk), :]
        s = _dot(k, qt_ref[0, e])
        s_ref[e] = s
        return jnp.max(s, axis=0, keepdims=True)

    def update(e, ci, s_ref, mc, m, acc):
        m_new = jnp.maximum(m, mc)
        alpha = jnp.exp2(m - m_new)
        p = jnp.exp2(s_ref[e] - m_new).astype(BF16)
        return m_new, alpha * acc + _dot(vlt_ref[0, e, ci], p)

    state = []
    for e in range(2):
        m0 = jnp.full((1, tq), -jnp.inf, F32)
        acc0 = jnp.zeros((MLA_VT_ROWS, tq), F32)
        state += list(_mla_chunk(qt_ref[0, e], kc_ref[0, e], vct_ref[0, e, 0], m0, acc0))
    mca = [scores(e, 0, sa_ref) for e in range(2)]

    def body(i, carry):
        mca = list(carry[0:2])
        st = list(carry[2:6])
        c0 = 2 * i
        mcb = [scores(e, c0 + 1, sb_ref) for e in range(2)]
        for e in range(2):
            st[2 * e], st[2 * e + 1] = update(e, c0, sa_ref, mca[e], st[2 * e], st[2 * e + 1])
        c2 = jnp.minimum(c0 + 2, nch - 1)
        mca = [scores(e, c2, sa_ref) for e in range(2)]
        for e in range(2):
            st[2 * e], st[2 * e + 1] = update(e, c0 + 1, sb_ref, mcb[e], st[2 * e], st[2 * e + 1])
        return tuple(mca + st)

    carry = lax.fori_loop(0, nch // 2, body, tuple(mca + state))
    o_ref[0] = (_mla_pair_out(carry[3], carry[5]) * gm_ref[0].astype(F32)).astype(BF16)


def _attn0_call(qt, kc, vct, kl, vlt, gm, tq):
    b, h, _, n = qt.shape
    c = kc.shape[2]
    nch, _, tk = vlt.shape[2:]
    assert nch % 2 == 0
    o_spec = pl.BlockSpec((1, tq, LANES), lambda bi, hp, i: (bi, i, hp))
    return pl.pallas_call(
        _attn0_kernel,
        grid=(b, h // 2, n // tq),
        in_specs=[
            pl.BlockSpec((1, 2, LANES, tq), lambda bi, hp, i: (bi, hp, 0, i)),
            pl.BlockSpec((1, 2, c, LANES), lambda bi, hp, i: (bi, hp, 0, 0)),
            pl.BlockSpec((1, 2, 1, MLA_VT_ROWS, c), lambda bi, hp, i: (bi, hp, 0, 0, 0)),
            pl.BlockSpec((1, 2, n, LANES), lambda bi, hp, i: (bi, hp, 0, 0)),
            pl.BlockSpec((1, 2, nch, MLA_VT_ROWS, tk), lambda bi, hp, i: (bi, hp, 0, 0, 0)),
            o_spec],
        out_specs=o_spec,
        out_shape=jax.ShapeDtypeStruct((b, n, MLA_WIDTH), BF16),
        scratch_shapes=[pltpu.VMEM((2, tk, tq), F32), pltpu.VMEM((2, tk, tq), F32)],
        compiler_params=_cparams(("parallel", "parallel", "arbitrary")),
        name="attn0",
    )(qt, kc, vct, kl, vlt, gm)


def _fft_a_kernel(t2, n1, zr_ref, zi_ref, fa_ref, tw_ref, tr_ref, ti_ref):
    z = jnp.concatenate([zr_ref[0], zi_ref[0]], axis=0)
    a = _dot(fa_ref[...], z)
    for j in range(t2):
        c = tw_ref[0, j]
        s = tw_ref[1, j]
        for g in range(F_GROUPS):
            lo = j * F_WIDTH + g * LANES
            ar = a[0:n1, lo:lo + LANES]
            ai = a[n1:2 * n1, lo:lo + LANES]
            tr_ref[0, :, lo:lo + LANES] = (ar * c + ai * s).astype(BF16)
            ti_ref[0, :, lo:lo + LANES] = (ai * c - ar * s).astype(BF16)


def _fft_a_call(zr, zi, fa, tw, t2):
    b, n1, w = zr.shape
    spec = pl.BlockSpec((1, n1, t2 * F_WIDTH), lambda bi, j: (bi, 0, j))
    out = jax.ShapeDtypeStruct((b, n1, w), BF16)
    return pl.pallas_call(
        functools.partial(_fft_a_kernel, t2, n1),
        grid=(b, w // (t2 * F_WIDTH)),
        in_specs=[spec, spec, _full(fa.shape), pl.BlockSpec((2, t2, n1, LANES), lambda bi, j: (0, j, 0, 0))],
        out_specs=[spec, spec],
        out_shape=[out, out],
        compiler_params=_cparams(("parallel", "arbitrary")),
        name="fft_a",
    )(zr, zi, fa, tw)


def _fft_c_kernel(t1, tr_ref, ti_ref, fcs_ref, gf_ref, o_ref):
    for k in range(t1):
        rhs = jnp.concatenate([tr_ref[0, k], ti_ref[0, k]], axis=0)
        lo = k * F_WIDTH
        out = _dot(fcs_ref[...], rhs)
        o_ref[0, :, lo:lo + F_WIDTH] = (out * gf_ref[0, :, lo:lo + F_WIDTH].astype(F32)).astype(BF16)


def _fft_c_call(tr, ti, fcs, gf, t1):
    b, n1, r, w = tr.shape
    t_spec = pl.BlockSpec((1, t1, r, w), lambda bi, i: (bi, i, 0, 0))
    g_spec = pl.BlockSpec((1, r, t1 * w), lambda bi, i: (bi, 0, i))
    return pl.pallas_call(
        functools.partial(_fft_c_kernel, t1),
        grid=(b, n1 // t1),
        in_specs=[t_spec, t_spec, _full(fcs.shape), g_spec],
        out_specs=g_spec,
        out_shape=jax.ShapeDtypeStruct((b, r, n1 * w), BF16),
        compiler_params=_cparams(("parallel", "arbitrary")),
        name="fft_c",
    )(tr, ti, fcs, gf)


def _mid_kernel(fm_ref, ag_ref, x_ref, g0_ref, w_out_ref, ng_ref, sc_ref, sh_ref, w_qt_ref, w_k_ref, w_vt_ref,
                vone_ref, w_g_ref, tqt_ref, tk_ref, x1_ref, q_ref, k_ref, v_ref, sg_ref):
    tm = x_ref.shape[1]
    y = _dot(fm_ref[0], w_out_ref[0:F_WIDTH, :]) + _dot(ag_ref[0], w_out_ref[F_WIDTH:, :])
    x1 = x_ref[0] + g0_ref[0] * y
    x1_ref[0] = x1
    h = (_rms(x1, ng_ref[...]) * (1.0 + sc_ref[0]) + sh_ref[0]).astype(BF16)
    qt = _dot_nt(w_qt_ref[...], h)
    cos = tqt_ref[0]
    sin = tqt_ref[1]
    half = GQA_HEAD_DIM // 2
    for hd in range(GQA_HEADS):
        lo = hd * GQA_HEAD_DIM
        x1r = qt[lo:lo + half]
        x2r = qt[lo + half:lo + GQA_HEAD_DIM]
        qh = jnp.concatenate([x1r * cos - x2r * sin, x2r * cos + x1r * sin], axis=0).astype(BF16)
        g, j = divmod(hd, GQA_GROUP)
        for blk in range(tm // BLOCK):
            q_ref[0, blk, g, :, j * BLOCK:(j + 1) * BLOCK] = qh[:, blk * BLOCK:(blk + 1) * BLOCK]
    vt = _dot_nt(w_vt_ref[...], h) + vone_ref[...]
    for g in range(GQA_KV_HEADS):
        k_ref[0, g] = _rope(_dot(h, w_k_ref[:, g * LANES:(g + 1) * LANES]), tk_ref, 32).astype(BF16)
        v_ref[0, g] = vt[g * GQA_VT_ROWS:(g + 1) * GQA_VT_ROWS].astype(BF16)
    sg_ref[0] = _silu(_dot(h, w_g_ref[...])).astype(BF16)


def _mid_call(fm, ag, x, g0, w_out, ng, sc, sh, w_qt, w_k, w_vt, vone, w_g, tqt, tk, tm):
    b, n, d = x.shape
    row = lambda w: pl.BlockSpec((1, tm, w), lambda bi, i: (bi, i, 0))
    vec_spec = pl.BlockSpec((1, 1, d), lambda bi, i: (bi, 0, 0))
    nblk = tm // BLOCK
    return pl.pallas_call(
        _mid_kernel,
        grid=(b, n // tm),
        in_specs=[row(F_WIDTH), row(MLA_WIDTH), row(d), vec_spec, _full(w_out.shape), _full(ng.shape),
                  vec_spec, vec_spec, _full(w_qt.shape), _full(w_k.shape), _full(w_vt.shape), _full(vone.shape),
                  _full(w_g.shape),
                  pl.BlockSpec((2, GQA_HEAD_DIM // 2, tm), lambda bi, i: (0, 0, i)),
                  pl.BlockSpec((3, tm, LANES), lambda bi, i: (0, i, 0))],
        out_specs=[row(d),
                   pl.BlockSpec((1, nblk, GQA_KV_HEADS, GQA_HEAD_DIM, GQA_GROUP * BLOCK),
                                lambda bi, i: (bi, i, 0, 0, 0)),
                   pl.BlockSpec((1, GQA_KV_HEADS, tm, LANES), lambda bi, i: (bi, 0, i, 0)),
                   pl.BlockSpec((1, GQA_KV_HEADS, GQA_VT_ROWS, tm), lambda bi, i: (bi, 0, 0, i)),
                   row(GQA_Q)],
        out_shape=[jax.ShapeDtypeStruct((b, n, d), F32),
                   jax.ShapeDtypeStruct((b, n // BLOCK, GQA_KV_HEADS, GQA_HEAD_DIM, GQA_GROUP * BLOCK), BF16),
                   jax.ShapeDtypeStruct((b, GQA_KV_HEADS, n, LANES), BF16),
                   jax.ShapeDtypeStruct((b, GQA_KV_HEADS, GQA_VT_ROWS, n), BF16),
                   jax.ShapeDtypeStruct((b, n, GQA_Q), BF16)],
        compiler_params=_cparams(("parallel", "arbitrary")),
        name="mid",
    )(fm, ag, x, g0, w_out, ng, sc, sh, w_qt, w_k, w_vt, vone, w_g, tqt, tk)


def _band_bias(nc):
    col = np.arange(3 * BLOCK)[:, None]
    q = np.arange(BLOCK)[None, :]
    dist = BLOCK + q - col
    band = np.abs(dist) <= WINDOW
    variants = [band & (col >= BLOCK), band, band & (col < 2 * BLOCK)]
    out = np.zeros((3, nc + 3 * BLOCK, BLOCK), np.float32)
    for v, ok in enumerate(variants):
        out[v, nc:] = np.where(ok, 0.0, NEG_BIG)
    return jnp.asarray(out)


def _attn1_kernel(sink_ref, q_ref, kp_ref, kc_ref, kn_ref, vp_ref, vc_ref, vn_ref, ck_ref, cv_ref, bias_ref,
                  sg_ref, x1_ref, g1_ref, w_out_ref, fg_ref, o_ref):
    i = pl.program_id(1)
    last = pl.num_programs(1) - 1
    sel = jnp.where(i == 0, 0, jnp.where(i == last, 2, 1))
    bias = bias_ref[sel]
    bias4 = jnp.concatenate([bias] * GQA_GROUP, axis=1)
    chunks = [None] * (GQA_Q // LANES)
    for g in range(GQA_KV_HEADS):
        kb = jnp.concatenate([ck_ref[0, g], kp_ref[0, g], kc_ref[0, g], kn_ref[0, g]], axis=0)[:, :GQA_HEAD_DIM]
        vt = jnp.concatenate([cv_ref[0, g], vp_ref[0, g], vc_ref[0, g], vn_ref[0, g]], axis=1)
        s = _dot(kb, q_ref[0, 0, g]) + bias4
        sink = jnp.concatenate(
            [jnp.full((1, BLOCK), sink_ref[GQA_GROUP * g + j] * LOG2E, F32) for j in range(GQA_GROUP)], axis=1)
        m = jnp.maximum(jnp.max(s, axis=0, keepdims=True), sink)
        p = jnp.exp2(s - m).astype(BF16)
        acc = _dot(vt, p)
        l = acc[GQA_HEAD_DIM:GQA_HEAD_DIM + 1] + jnp.exp2(sink - m)
        ot = acc[0:GQA_HEAD_DIM] / l
        for p2 in range(2):
            lo = 2 * p2 * BLOCK
            pair = jnp.concatenate([ot[:, lo:lo + BLOCK], ot[:, lo + BLOCK:lo + 2 * BLOCK]], axis=0)
            chunks[2 * g + p2] = pair.T
    o = jnp.concatenate(chunks, axis=-1)
    og = (o * sg_ref[0].astype(F32)).astype(BF16)
    y = _dot(og, w_out_ref[...])
    x2 = x1_ref[0] + g1_ref[0] * y
    o_ref[0] = _rms(x2, fg_ref[...])


def _attn1_call(sink, qt, k, vt, ck, cvt, bias, sg, x1, g1, w_out, fg):
    b, n, d = x1.shape
    nc = ck.shape[2]
    nb = n // BLOCK
    row = lambda w: pl.BlockSpec((1, BLOCK, w), lambda bi, i: (bi, i, 0))
    prv = lambda i: jnp.maximum(i - 1, 0)
    nxt = lambda i: jnp.minimum(i + 1, nb - 1)
    k_spec = lambda f: pl.BlockSpec((1, GQA_KV_HEADS, BLOCK, LANES), lambda bi, i: (bi, 0, f(i), 0))
    v_spec = lambda f: pl.BlockSpec((1, GQA_KV_HEADS, GQA_VT_ROWS, BLOCK), lambda bi, i: (bi, 0, 0, f(i)))
    cur = lambda i: i
    vec_spec = pl.BlockSpec((1, 1, d), lambda bi, i: (bi, 0, 0))
    return pl.pallas_call(
        _attn1_kernel,
        grid=(b, nb),
        in_specs=[pl.BlockSpec(memory_space=pltpu.SMEM),
                  pl.BlockSpec((1, 1, GQA_KV_HEADS, GQA_HEAD_DIM, GQA_GROUP * BLOCK), lambda bi, i: (bi, i, 0, 0, 0)),
                  k_spec(prv), k_spec(cur), k_spec(nxt), v_spec(prv), v_spec(cur), v_spec(nxt),
                  pl.BlockSpec((1, GQA_KV_HEADS, nc, LANES), lambda bi, i: (bi, 0, 0, 0)),
                  pl.BlockSpec((1, GQA_KV_HEADS, GQA_VT_ROWS, nc), lambda bi, i: (bi, 0, 0, 0)),
                  _full(bias.shape), row(GQA_Q), row(d), vec_spec, _full(w_out.shape), _full(fg.shape)],
        out_specs=row(d),
        out_shape=jax.ShapeDtypeStruct((b, n, d), F32),
        compiler_params=_cparams(("parallel", "arbitrary")),
        name="attn1",
    )(sink, qt, k, k, k, vt, vt, vt, ck, cvt, bias, sg, x1, g1, w_out, fg)


def _prep_even_weights(w_in, w_qb, w_kvb):
    d = w_in.shape[0]
    f_in, f_gate, q_a, kv_a, k_pe, m_gate = jnp.split(
        w_in, np.cumsum([F_WIDTH, F_WIDTH, MLA_Q_RANK, MLA_KV_RANK, MLA_ROPE]).tolist(), axis=1)
    kpe_blk = jnp.concatenate([jnp.zeros((d, MLA_NOPE), F32), k_pe, jnp.zeros((d, 32), F32)], axis=1)
    w_in_p = jnp.concatenate([f_in, f_gate, m_gate, q_a, kv_a, kpe_blk], axis=1).astype(BF16)
    qb = w_qb.reshape(MLA_Q_RANK, MLA_HEADS, MLA_NOPE + MLA_ROPE)
    w_qbt = jnp.pad(qb, ((0, 0), (0, 0), (0, 32))).reshape(MLA_Q_RANK, MLA_HEADS * LANES).T.astype(BF16)
    kvb = w_kvb.reshape(MLA_KV_RANK, MLA_HEADS, MLA_NOPE + MLA_V)
    w_k = jnp.pad(kvb[:, :, :MLA_NOPE], ((0, 0), (0, 0), (0, 64))).reshape(MLA_KV_RANK, MLA_HEADS * LANES)
    w_vt = jnp.pad(kvb[:, :, MLA_NOPE:], ((0, 0), (0, 0), (0, MLA_VT_ROWS - MLA_V)))
    w_vt = w_vt.reshape(MLA_KV_RANK, MLA_HEADS * MLA_VT_ROWS).T.astype(BF16)
    vone = np.zeros((MLA_HEADS * MLA_VT_ROWS, 1), np.float32)
    vone[MLA_V::MLA_VT_ROWS] = 1.0
    return w_in_p, w_qbt, w_k.astype(BF16), w_vt, jnp.asarray(vone)


def _prep_odd_weights(w_in):
    d = w_in.shape[0]
    w_qt = w_in[:, O_Q:O_K].T.astype(BF16)
    kw = w_in[:, O_K:O_V].reshape(d, GQA_KV_HEADS, GQA_HEAD_DIM)
    w_k = jnp.pad(kw, ((0, 0), (0, 0), (0, LANES - GQA_HEAD_DIM))).reshape(d, GQA_KV_HEADS * LANES).astype(BF16)
    vw = w_in[:, O_V:O_G].reshape(d, GQA_KV_HEADS, GQA_HEAD_DIM)
    w_vt = jnp.pad(vw, ((0, 0), (0, 0), (0, GQA_VT_ROWS - GQA_HEAD_DIM)))
    w_vt = w_vt.reshape(d, GQA_KV_HEADS * GQA_VT_ROWS).T.astype(BF16)
    vone = np.zeros((GQA_KV_HEADS * GQA_VT_ROWS, 1), np.float32)
    vone[GQA_HEAD_DIM::GQA_VT_ROWS] = 1.0
    return w_qt, w_k, w_vt, jnp.asarray(vone), w_in[:, O_G:O_END].astype(BF16)


def _fft_consts(n, c):
    n1 = n // LANES
    c1, s1 = _dft_cs(n1)
    fa = np.block([[c1, s1], [-s1, c1]])
    c2, s2 = _dft_cs(LANES)
    fcs = np.concatenate([c2, s2], axis=1) / math.sqrt(n * F_GROUP_DIM)
    fc = np.concatenate([c2, -s2], axis=1)
    r = np.arange(LANES, dtype=np.int64)[:, None]
    k1 = np.arange(n1, dtype=np.int64)[None, :]
    ang = 2.0 * np.pi * ((r * k1) % n).astype(np.float64) / n
    tw = np.stack([np.cos(ang), np.sin(ang)])
    tw = np.broadcast_to(tw[..., None], (2, LANES, n1, LANES))
    cn, sn = _dft_cs(c)
    fn = np.concatenate([cn, sn], axis=1)
    as32 = lambda a: jnp.asarray(np.ascontiguousarray(a, dtype=np.float32))
    return as32(fa).astype(BF16), as32(fcs).astype(BF16), as32(fc).astype(BF16), as32(tw), as32(fn).astype(BF16)


def _pick_tile(n, pref):
    t = pref
    while n % t:
        t //= 2
    return t


def kernel(x, c, ctx, c_ctx, w_mod, b_mod, norm_g, e_w_in, e_q_norm, e_w_qb, e_kv_norm, e_w_kvb, e_w_out,
           o_w_in, o_sink, o_w_out, final_g):
    b, n, d = x.shape
    nc = ctx.shape[1]
    assert d == D_MODEL and n % (LANES * 8) == 0 and nc % LANES == 0 and b <= 7

    svec = jnp.zeros((8, d), F32).at[:b].set(c).at[b].set(c_ctx)
    mod = _mod_call(svec, w_mod, b_mod)
    sh, sc, gt = mod[:, :, :d], mod[:, :, d:2 * d], mod[:, :, 2 * d:]
    lat = lambda t, l: t[l, :b].reshape(b, 1, d)
    cvec = lambda t, l: t[l, b].reshape(1, d)
    ng0, ng1 = norm_g[0].reshape(1, d), norm_g[1].reshape(1, d)

    cos_m, sin_m = _axial_rope_tables(n, MLA_ROPE)
    qs = MLA_SCALE * LOG2E
    tqt0 = jnp.stack([cos_m.T, sin_m.T]) * qs
    tk0 = _mla_rope_tab(cos_m, sin_m)
    one_c, zero_c = jnp.ones((nc, MLA_ROPE // 2), F32), jnp.zeros((nc, MLA_ROPE // 2), F32)
    tqt0c = jnp.stack([one_c.T, zero_c.T]) * qs
    tk0c = _mla_rope_tab(one_c, zero_c)
    cos_g, sin_g = _axial_rope_tables(n, GQA_HEAD_DIM)
    tqt1 = jnp.stack([cos_g.T, sin_g.T]) * (GQA_SCALE * LOG2E)
    tk1 = _gqa_rope_tab(cos_g, sin_g)
    fa, fcs, fc, tw, fn = _fft_consts(n, nc)
    bias1 = _band_bias(nc)

    w_in0, w_qbt0, w_k0, w_vt0, vone = _prep_even_weights(e_w_in[0], e_w_qb[0], e_w_kvb[0])
    qn0 = e_q_norm[0].reshape(1, MLA_Q_RANK)
    kvn0 = e_kv_norm[0].reshape(1, MLA_KV_RANK)
    w_out0 = e_w_out[0].astype(BF16)
    w_qt1, w_k1, w_vt1, vone1, w_g1 = _prep_odd_weights(o_w_in[0])
    w_out1 = o_w_out[0].astype(BF16)

    kc, vct, ck1, cvt1 = _ctx_call(
        ctx, (ng0, cvec(sc, 0), cvec(sh, 0), cvec(gt, 0)), (ng1, cvec(sc, 1), cvec(sh, 1)),
        w_in0, qn0, w_qbt0, kvn0, w_k0, w_vt0, vone, tqt0c, tk0c, fc, fn, w_out0, w_k1, w_vt1, vone1)

    tm = _pick_tile(n, 512)
    qt0, k0, vt0, zr, zi, gf, gm = _front0_call(
        x, ng0, lat(sc, 0), lat(sh, 0), w_in0, qn0, w_qbt0, kvn0, w_k0, w_vt0, vone, tqt0, tk0, fc, tm)

    ag = _attn0_call(qt0, kc, vct, k0, vt0, gm, _pick_tile(n, 256))

    n1 = n // LANES
    tr, ti = _fft_a_call(zr.reshape(b, n1, LANES * F_WIDTH), zi.reshape(b, n1, LANES * F_WIDTH), fa, tw, 8)
    fmg = _fft_c_call(tr.reshape(b, n1, LANES, F_WIDTH), ti.reshape(b, n1, LANES, F_WIDTH), fcs,
                      gf.reshape(b, LANES, n1 * F_WIDTH), _pick_tile(n1, 8))
    fmg = fmg.reshape(b, n, F_WIDTH)

    tm1 = _pick_tile(n, 256)
    x1, qt1, k1, vt1, sg = _mid_call(fmg, ag, x, lat(gt, 0), w_out0, ng1, lat(sc, 1), lat(sh, 1),
                                     w_qt1, w_k1, w_vt1, vone1, w_g1, tqt1, tk1, tm1)

    return _attn1_call(o_sink[0], qt1, k1, vt1, ck1, cvt1, bias1, sg, x1, lat(gt, 1), w_out1,
                       final_g.reshape(1, d))
```

```python
import functools
import math

import numpy as np
import jax
import jax.numpy as jnp
from jax import lax
from jax.experimental import pallas as pl
from jax.experimental.pallas import tpu as pltpu

F32 = jnp.float32
BF16 = jnp.bfloat16

D_MODEL = 1024
GRID_W = 64
EPS = 1e-6
ROPE_BASE = 10000.0
LANES = 128

F_GROUPS = 4
F_GROUP_DIM = 128
F_WIDTH = F_GROUPS * F_GROUP_DIM

MLA_HEADS = 8
MLA_NOPE = 64
MLA_ROPE = 32
MLA_V = 64
MLA_Q_RANK = 384
MLA_KV_RANK = 256
MLA_WIDTH = MLA_HEADS * MLA_V
MLA_SCALE = 1.0 / math.sqrt(MLA_NOPE + MLA_ROPE)
LOG2E = math.log2(math.e)
MLA_VT_ROWS = 80
PV_KEYS = 256
MLA_TQ = 256
SOFTMAX_GROUP = 2
PIPE_SKEW = 3
SCORE_SLOTS = 4
GATE_LAG = 2

GQA_HEADS = 16
GQA_KV_HEADS = 4
GQA_GROUP = GQA_HEADS // GQA_KV_HEADS
GQA_HEAD_DIM = 64
WINDOW = 128
BLOCK = 128
GQA_Q = GQA_HEADS * GQA_HEAD_DIM
GQA_KV = GQA_KV_HEADS * GQA_HEAD_DIM
GQA_SCALE = 1.0 / math.sqrt(GQA_HEAD_DIM)
GQA_VT_ROWS = 80

E_FIN, E_FGATE, E_MGATE, E_QA, E_KVA, E_KPE, E_END = 0, 512, 1024, 1536, 1920, 2176, 2304
O_Q, O_K, O_V, O_G, O_END = 0, 1024, 1280, 1536, 2560

NEG_BIG = -1e30
VMEM_LIMIT = 56 * 1024 * 1024


def _cparams(sem):
    return pltpu.CompilerParams(dimension_semantics=sem, vmem_limit_bytes=VMEM_LIMIT)


def _dot(a, b):
    return jnp.dot(a, b, preferred_element_type=F32)


def _dot_nt(a, b):
    return lax.dot_general(a, b, (((1,), (1,)), ((), ())), preferred_element_type=F32)


def _rms(x, g):
    return x * lax.rsqrt(jnp.mean(x * x, axis=-1, keepdims=True) + EPS) * g


def _silu(x):
    return x * jax.nn.sigmoid(x)


def _rope(x, tab_ref, shift):
    up = pltpu.roll(x, LANES - shift, 1)
    dn = pltpu.roll(x, shift, 1)
    return x * tab_ref[0] + up * tab_ref[1] + dn * tab_ref[2]


def _full(shape):
    nd = len(shape)
    return pl.BlockSpec(shape, lambda *_: (0,) * nd)


def _axial_rope_tables(n, rot_dim):
    rows = n // GRID_W
    row = jnp.broadcast_to(jnp.arange(rows)[:, None], (rows, GRID_W)).reshape(-1).astype(F32)
    col = jnp.broadcast_to(jnp.arange(GRID_W)[None, :], (rows, GRID_W)).reshape(-1).astype(F32)
    nf = rot_dim // 4
    inv = ROPE_BASE ** (-jnp.arange(nf, dtype=F32) / nf)
    ang = jnp.concatenate([row[:, None] * inv, col[:, None] * inv], axis=-1)
    return jnp.cos(ang), jnp.sin(ang)


def _mla_rope_tab(cos, sin):
    n = cos.shape[0]
    z = lambda w: jnp.zeros((n, w), F32)
    cf = jnp.concatenate([jnp.ones((n, MLA_NOPE), F32), cos, cos, z(32)], axis=-1)
    s1 = jnp.concatenate([z(MLA_NOPE), -sin, z(16), z(32)], axis=-1)
    s2 = jnp.concatenate([z(MLA_NOPE), z(16), sin, z(32)], axis=-1)
    return jnp.stack([cf, s1, s2])


def _gqa_rope_tab(cos, sin):
    z = jnp.zeros_like(sin)
    cf = jnp.concatenate([cos, cos, z, z], axis=-1)
    s1 = jnp.concatenate([-sin, z, z, z], axis=-1)
    s2 = jnp.concatenate([z, sin, z, z], axis=-1)
    return jnp.stack([cf, s1, s2])


def _dft_cs(n):
    idx = np.arange(n, dtype=np.int64)
    ang = 2.0 * np.pi * ((idx[:, None] * idx[None, :]) % n).astype(np.float64) / n
    return np.cos(ang), np.sin(ang)


def _mod_kernel(s_ref, w_ref, b_ref, o_ref):
    s = _silu(s_ref[...])
    o_ref[0] = _dot(s.astype(BF16), w_ref[0].astype(BF16)) + b_ref[0]


def _mod_call(svec, w_mod, b_mod):
    depth, d, d3 = w_mod.shape
    tn = 768
    return pl.pallas_call(
        _mod_kernel,
        grid=(depth, d3 // tn),
        in_specs=[
            pl.BlockSpec((8, d), lambda l, j: (0, 0)),
            pl.BlockSpec((1, d, tn), lambda l, j: (l, 0, j)),
            pl.BlockSpec((1, 1, tn), lambda l, j: (l, 0, j)),
        ],
        out_specs=pl.BlockSpec((1, 8, tn), lambda l, j: (l, 0, j)),
        out_shape=jax.ShapeDtypeStruct((depth, 8, d3), F32),
        compiler_params=_cparams(("arbitrary", "arbitrary")),
        name="mod",
    )(svec, w_mod, b_mod.reshape(depth, 1, d3))


def _even_front(x, ng, sc, sh, w_in_ref, qn, w_qbt_ref, kvn, w_k_ref, w_vt_ref, vone, tqt_ref, tk_ref, fc_ref):
    m = x.shape[0]
    h = (_rms(x, ng) * (1.0 + sc) + sh).astype(BF16)
    f_in = _dot(h, w_in_ref[:, E_FIN:E_FGATE])
    f_gate = _dot(h, w_in_ref[:, E_FGATE:E_MGATE])
    m_gate = _dot(h, w_in_ref[:, E_MGATE:E_QA])
    q_a = _dot(h, w_in_ref[:, E_QA:E_KVA])
    kv_a = _dot(h, w_in_ref[:, E_KVA:E_KPE])
    kpe = _dot(h, w_in_ref[:, E_KPE:E_END])
    qh = _rms(q_a, qn).astype(BF16)
    ch = _rms(kv_a, kvn).astype(BF16)
    kpe_r = _rope(kpe, tk_ref, 16)
    qt = _dot_nt(w_qbt_ref[...], qh)
    cos = tqt_ref[0]
    sin = tqt_ref[1]
    pad = jnp.zeros((LANES - MLA_NOPE - MLA_ROPE, m), F32)
    qts, ks = [], []
    for hd in range(MLA_HEADS):
        lo = hd * LANES
        x1 = qt[lo + MLA_NOPE:lo + MLA_NOPE + 16]
        x2 = qt[lo + MLA_NOPE + 16:lo + MLA_NOPE + 32]
        qts.append(jnp.concatenate(
            [qt[lo:lo + MLA_NOPE] * (MLA_SCALE * LOG2E), x1 * cos - x2 * sin, x2 * cos + x1 * sin, pad], axis=0))
        ks.append(_dot(ch, w_k_ref[:, lo:lo + LANES]) + kpe_r)
    vt = _dot_nt(w_vt_ref[...], ch) + vone
    zr, zi = [], []
    fb = f_in.astype(BF16)
    for g in range(F_GROUPS):
        z = _dot(fb[:, g * LANES:(g + 1) * LANES], fc_ref[...])
        zr.append(z[:, :LANES])
        zi.append(z[:, LANES:])
    return qts, ks, vt, zr, zi, f_gate, m_gate


def _front0_kernel(x_ref, ng_ref, sc_ref, sh_ref, w_in_ref, qn_ref, w_qbt_ref, kvn_ref, w_k_ref, w_vt_ref,
                   vone_ref, tqt_ref, tk_ref, fc_ref, q_ref, k_ref, v_ref, zr_ref, zi_ref, gf_ref, gm_ref):
    qts, ks, vt, zr, zi, f_gate, m_gate = _even_front(
        x_ref[0], ng_ref[...], sc_ref[0], sh_ref[0], w_in_ref, qn_ref[...], w_qbt_ref, kvn_ref[...],
        w_k_ref, w_vt_ref, vone_ref[...], tqt_ref, tk_ref, fc_ref)
    for hd in range(MLA_HEADS):
        q_ref[0, hd] = qts[hd].astype(BF16)
        k_ref[0, hd] = ks[hd].astype(BF16)
        v_ref[0, hd, 0] = vt[hd * MLA_VT_ROWS:(hd + 1) * MLA_VT_ROWS].astype(BF16)
    for g in range(F_GROUPS):
        zr_ref[0, :, g * LANES:(g + 1) * LANES] = zr[g].astype(BF16)
        zi_ref[0, :, g * LANES:(g + 1) * LANES] = zi[g].astype(BF16)
    gf_ref[0] = _silu(f_gate).astype(BF16)
    gm_ref[0] = _silu(m_gate).astype(BF16)


def _front0_call(x, ng, sc, sh, w_in, qn, w_qbt, kvn, w_k, w_vt, vone, tqt, tk, fc, tm):
    b, n, d = x.shape
    fw = jax.ShapeDtypeStruct((b, n, F_WIDTH), BF16)
    row_spec = pl.BlockSpec((1, tm, F_WIDTH), lambda bi, i: (bi, i, 0))
    vec_spec = pl.BlockSpec((1, 1, d), lambda bi, i: (bi, 0, 0))
    return pl.pallas_call(
        _front0_kernel,
        grid=(b, n // tm),
        in_specs=[
            pl.BlockSpec((1, tm, d), lambda bi, i: (bi, i, 0)),
            _full(ng.shape), vec_spec, vec_spec,
            _full(w_in.shape), _full(qn.shape), _full(w_qbt.shape), _full(kvn.shape), _full(w_k.shape),
            _full(w_vt.shape), _full(vone.shape),
            pl.BlockSpec((2, 16, tm), lambda bi, i: (0, 0, i)),
            pl.BlockSpec((3, tm, LANES), lambda bi, i: (0, i, 0)),
            _full(fc.shape),
        ],
        out_specs=[
            pl.BlockSpec((1, MLA_HEADS, LANES, tm), lambda bi, i: (bi, 0, 0, i)),
            pl.BlockSpec((1, MLA_HEADS, tm, LANES), lambda bi, i: (bi, 0, i, 0)),
            pl.BlockSpec((1, MLA_HEADS, 1, MLA_VT_ROWS, tm), lambda bi, i: (bi, 0, i, 0, 0)),
            row_spec, row_spec, row_spec, row_spec],
        out_shape=[
            jax.ShapeDtypeStruct((b, MLA_HEADS, LANES, n), BF16),
            jax.ShapeDtypeStruct((b, MLA_HEADS, n, LANES), BF16),
            jax.ShapeDtypeStruct((b, MLA_HEADS, n // tm, MLA_VT_ROWS, tm), BF16),
            fw, fw, fw, fw],
        compiler_params=_cparams(("parallel", "arbitrary")),
        name="front0",
    )(x, ng, sc, sh, w_in, qn, w_qbt, kvn, w_k, w_vt, vone, tqt, tk, fc)


def _zero_after(x):
    u = lax.bitcast_convert_type(x, jnp.uint32)
    z = lax.shift_right_logical(lax.shift_right_logical(u, jnp.uint32(16)), jnp.uint32(16))
    return lax.bitcast_convert_type(z, F32)


def _mla_chunk(qt, k, vt, m, acc):
    s = _dot(k, qt)
    m_new = jnp.maximum(m, jnp.max(s, axis=0, keepdims=True))
    alpha = jnp.exp2(m - m_new)
    p = jnp.exp2(s - m_new).astype(BF16)
    return m_new, alpha * acc + _dot(vt, p)


def _mla_pair_out(acc0, acc1):
    o0 = acc0[0:MLA_V] / acc0[MLA_V:MLA_V + 1]
    o1 = acc1[0:MLA_V] / acc1[MLA_V:MLA_V + 1]
    return jnp.concatenate([o0, o1], axis=0).T


def _ctx_kernel(ctx_scale, x_ref, ng0_ref, sc0_ref, sh0_ref, g0_ref, ng1_ref, sc1_ref, sh1_ref,
                w_in_ref, qn_ref, w_qbt_ref, kvn_ref, w_k_ref, w_vt_ref, vone_ref, tqt_ref, tk_ref, fc_ref,
                fn_ref, w_out_ref, w_k1_ref, w_vt1_ref, vone1_ref,
                kc_ref, vc_ref, ck1_ref, cv1_ref):
    x = x_ref[0]
    c = x.shape[0]
    qts, ks, vt, zr, zi, f_gate, m_gate = _even_front(
        x, ng0_ref[...], sc0_ref[...], sh0_ref[...], w_in_ref, qn_ref[...], w_qbt_ref, kvn_ref[...],
        w_k_ref, w_vt_ref, vone_ref[...], tqt_ref, tk_ref, fc_ref)
    pairs = []
    for hp in range(MLA_HEADS // 2):
        accs = []
        for e in range(2):
            hd = 2 * hp + e
            kb = ks[hd].astype(BF16)
            vtb = vt[hd * MLA_VT_ROWS:(hd + 1) * MLA_VT_ROWS].astype(BF16)
            kc_ref[0, hd] = kb
            vc_ref[0, hd, 0] = vtb
            m0 = jnp.full((1, c), -jnp.inf, F32)
            acc0 = jnp.zeros((MLA_VT_ROWS, c), F32)
            accs.append(_mla_chunk(qts[hd].astype(BF16), kb, vtb, m0, acc0)[1])
        pairs.append(_mla_pair_out(accs[0], accs[1]))
    a = jnp.concatenate(pairs, axis=-1)
    z = jnp.concatenate([jnp.concatenate(zr, axis=-1), jnp.concatenate(zi, axis=-1)], axis=0).astype(BF16)
    fm = _dot(fn_ref[...], z) * ctx_scale
    yf = (fm * _silu(f_gate)).astype(BF16)
    ya = (a * _silu(m_gate)).astype(BF16)
    y = _dot(yf, w_out_ref[0:F_WIDTH, :]) + _dot(ya, w_out_ref[F_WIDTH:, :])
    x1 = x + g0_ref[...] * y
    h1 = (_rms(x1, ng1_ref[...]) * (1.0 + sc1_ref[...]) + sh1_ref[...]).astype(BF16)
    vt1 = _dot_nt(w_vt1_ref[...], h1) + vone1_ref[...]
    for g in range(GQA_KV_HEADS):
        ck1_ref[0, g] = _dot(h1, w_k1_ref[:, g * LANES:(g + 1) * LANES]).astype(BF16)
        cv1_ref[0, g] = vt1[g * GQA_VT_ROWS:(g + 1) * GQA_VT_ROWS].astype(BF16)


def _ctx_call(ctx, vecs0, vecs1, w_in, qn, w_qbt, kvn, w_k, w_vt, vone, tqt, tk, fc, fn, w_out, w_k1, w_vt1,
              vone1):
    b, c, d = ctx.shape
    ng0, sc0, sh0, g0 = vecs0
    ng1, sc1, sh1 = vecs1
    ctx_scale = 1.0 / math.sqrt(c * F_GROUP_DIM)
    consts = [ng0, sc0, sh0, g0, ng1, sc1, sh1, w_in, qn, w_qbt, kvn, w_k, w_vt, vone, tqt, tk, fc, fn, w_out,
              w_k1, w_vt1, vone1]
    return pl.pallas_call(
        functools.partial(_ctx_kernel, ctx_scale),
        grid=(b,),
        in_specs=[pl.BlockSpec((1, c, d), lambda bi: (bi, 0, 0))] + [_full(a.shape) for a in consts],
        out_specs=[pl.BlockSpec((1, MLA_HEADS, c, LANES), lambda bi: (bi, 0, 0, 0)),
                   pl.BlockSpec((1, MLA_HEADS, 1, MLA_VT_ROWS, c), lambda bi: (bi, 0, 0, 0, 0)),
                   pl.BlockSpec((1, GQA_KV_HEADS, c, LANES), lambda bi: (bi, 0, 0, 0)),
                   pl.BlockSpec((1, GQA_KV_HEADS, GQA_VT_ROWS, c), lambda bi: (bi, 0, 0, 0))],
        out_shape=[jax.ShapeDtypeStruct((b, MLA_HEADS, c, LANES), BF16),
                   jax.ShapeDtypeStruct((b, MLA_HEADS, 1, MLA_VT_ROWS, c), BF16),
                   jax.ShapeDtypeStruct((b, GQA_KV_HEADS, c, LANES), BF16),
                   jax.ShapeDtypeStruct((b, GQA_KV_HEADS, GQA_VT_ROWS, c), BF16)],
        compiler_params=_cparams(("arbitrary",)),
        name="ctx",
    )(ctx, *consts)


def _attn0_kernel(qt_ref, kc_ref, vct_ref, kl_ref, vlt_ref, gm_ref, o_ref, s_ref):
    tq = qt_ref.shape[-1]
    nch, _, tk = vlt_ref.shape[2:]
    nc = kc_ref.shape[2]
    n_ctx = nc // PV_KEYS
    n_sub = n_ctx + nch * tk // PV_KEYS
    group = [0] * n_ctx + [1 + j // SOFTMAX_GROUP for j in range(n_sub - n_ctx)]
    first = [t for t in range(n_sub) if t == 0 or group[t] != group[t - 1]]
    last = [t for t in range(n_sub) if t == n_sub - 1 or group[t] != group[t + 1]]

    def keys(e, t):
        if t < n_ctx:
            return kc_ref[0, e, t * PV_KEYS:(t + 1) * PV_KEYS, :]
        lo = (t - n_ctx) * PV_KEYS
        return kl_ref[0, e, lo:lo + PV_KEYS, :]

    def values_t(e, t):
        if t < n_ctx:
            return vct_ref[0, e, 0, :, t * PV_KEYS:(t + 1) * PV_KEYS]
        j, off = divmod((t - n_ctx) * PV_KEYS, tk)
        return vlt_ref[0, e, j, :, off:off + PV_KEYS]

    def slot(i):
        r = i % SCORE_SLOTS
        return slice(r * PV_KEYS, (r + 1) * PV_KEYS)

    tasks = [(qi, t) for qi in range(tq // MLA_TQ) for t in range(n_sub)]
    colmax = []
    for step in range(len(tasks) + PIPE_SKEW):
        if step < len(tasks):
            qi, t = tasks[step]
            cols = slice(qi * MLA_TQ, (qi + 1) * MLA_TQ)
            cm = []
            for e in range(2):
                s = _dot(keys(e, t), qt_ref[0, e, :, cols])
                s_ref[e, slot(step)] = s
                cm.append(jnp.max(s, axis=0, keepdims=True))
            colmax.append(cm)
        if step < PIPE_SKEW:
            continue
        qi, u = tasks[step - PIPE_SKEW]
        if u == 0:
            m = [jnp.full((1, MLA_TQ), -jnp.inf, F32) for _ in range(2)]
            acc = [jnp.zeros((MLA_VT_ROWS, MLA_TQ), F32) for _ in range(2)]
        if u in first:
            members = [qi * n_sub + t for t in range(n_sub) if group[t] == group[u]]
            assert members[-1] <= step
            m_new, alpha = [], []
            for e in range(2):
                mc = functools.reduce(jnp.maximum, [colmax[i][e] for i in members])
                m_new.append(jnp.maximum(m[e], mc))
                alpha.append(jnp.exp2(m[e] - m_new[-1]))
            pv = [None, None]
        gate = min(step - GATE_LAG, len(tasks) - 1)
        for e in range(2):
            m_use = m_new[e] + _zero_after(colmax[gate][e])
            p = jnp.exp2(s_ref[e, slot(step - PIPE_SKEW)] - m_use).astype(BF16)
            d = _dot(values_t(e, u), p)
            pv[e] = d if pv[e] is None else pv[e] + d
        if u in last:
            for e in range(2):
                acc[e] = alpha[e] * acc[e] + pv[e]
                m[e] = m_new[e]
        if u == n_sub - 1:
            rows = slice(qi * MLA_TQ, (qi + 1) * MLA_TQ)
            o_ref[0, rows] = (_mla_pair_out(acc[0], acc[1]) * gm_ref[0, rows].astype(F32)).astype(BF16)


def _attn0_call(qt, kc, vct, kl, vlt, gm, tq):
    b, h, _, n = qt.shape
    c = kc.shape[2]
    nch, _, tk = vlt.shape[2:]
    assert c % PV_KEYS == 0 and tk % PV_KEYS == 0 and SCORE_SLOTS > PIPE_SKEW >= SOFTMAX_GROUP - 1
    assert tq % MLA_TQ == 0
    o_spec = pl.BlockSpec((1, tq, LANES), lambda bi, hp, i: (bi, i, hp))
    return pl.pallas_call(
        _attn0_kernel,
        grid=(b, h // 2, n // tq),
        in_specs=[
            pl.BlockSpec((1, 2, LANES, tq), lambda bi, hp, i: (bi, hp, 0, i)),
            pl.BlockSpec((1, 2, c, LANES), lambda bi, hp, i: (bi, hp, 0, 0)),
            pl.BlockSpec((1, 2, 1, MLA_VT_ROWS, c), lambda bi, hp, i: (bi, hp, 0, 0, 0)),
            pl.BlockSpec((1, 2, n, LANES), lambda bi, hp, i: (bi, hp, 0, 0)),
            pl.BlockSpec((1, 2, nch, MLA_VT_ROWS, tk), lambda bi, hp, i: (bi, hp, 0, 0, 0)),
            o_spec],
        out_specs=o_spec,
        out_shape=jax.ShapeDtypeStruct((b, n, MLA_WIDTH), BF16),
        scratch_shapes=[pltpu.VMEM((2, SCORE_SLOTS * PV_KEYS, MLA_TQ), F32)],
        compiler_params=_cparams(("parallel", "parallel", "arbitrary")),
        name="attn0",
    )(qt, kc, vct, kl, vlt, gm)


def _fft_a_kernel(t2, n1, zr_ref, zi_ref, fa_ref, tw_ref, tr_ref, ti_ref):
    z = jnp.concatenate([zr_ref[0], zi_ref[0]], axis=0)
    a = _dot(fa_ref[...], z)
    for j in range(t2):
        c = tw_ref[0, j]
        s = tw_ref[1, j]
        for g in range(F_GROUPS):
            lo = j * F_WIDTH + g * LANES
            ar = a[0:n1, lo:lo + LANES]
            ai = a[n1:2 * n1, lo:lo + LANES]
            tr_ref[0, :, lo:lo + LANES] = (ar * c + ai * s).astype(BF16)
            ti_ref[0, :, lo:lo + LANES] = (ai * c - ar * s).astype(BF16)


def _fft_a_call(zr, zi, fa, tw, t2):
    b, n1, w = zr.shape
    spec = pl.BlockSpec((1, n1, t2 * F_WIDTH), lambda bi, j: (bi, 0, j))
    out = jax.ShapeDtypeStruct((b, n1, w), BF16)
    return pl.pallas_call(
        functools.partial(_fft_a_kernel, t2, n1),
        grid=(b, w // (t2 * F_WIDTH)),
        in_specs=[spec, spec, _full(fa.shape), pl.BlockSpec((2, t2, n1, LANES), lambda bi, j: (0, j, 0, 0))],
        out_specs=[spec, spec],
        out_shape=[out, out],
        compiler_params=_cparams(("parallel", "arbitrary")),
        name="fft_a",
    )(zr, zi, fa, tw)


def _fft_c_kernel(t1, tr_ref, ti_ref, fcs_ref, gf_ref, o_ref):
    for k in range(t1):
        rhs = jnp.concatenate([tr_ref[0, k], ti_ref[0, k]], axis=0)
        lo = k * F_WIDTH
        out = _dot(fcs_ref[...], rhs)
        o_ref[0, :, lo:lo + F_WIDTH] = (out * gf_ref[0, :, lo:lo + F_WIDTH].astype(F32)).astype(BF16)


def _fft_c_call(tr, ti, fcs, gf, t1):
    b, n1, r, w = tr.shape
    t_spec = pl.BlockSpec((1, t1, r, w), lambda bi, i: (bi, i, 0, 0))
    g_spec = pl.BlockSpec((1, r, t1 * w), lambda bi, i: (bi, 0, i))
    return pl.pallas_call(
        functools.partial(_fft_c_kernel, t1),
        grid=(b, n1 // t1),
        in_specs=[t_spec, t_spec, _full(fcs.shape), g_spec],
        out_specs=g_spec,
        out_shape=jax.ShapeDtypeStruct((b, r, n1 * w), BF16),
        compiler_params=_cparams(("parallel", "arbitrary")),
        name="fft_c",
    )(tr, ti, fcs, gf)


def _mid_kernel(fm_ref, ag_ref, x_ref, g0_ref, w_out_ref, ng_ref, sc_ref, sh_ref, w_qt_ref, w_k_ref, w_vt_ref,
                vone_ref, w_g_ref, tqt_ref, tk_ref, x1_ref, q_ref, k_ref, v_ref, sg_ref):
    tm = x_ref.shape[1]
    y = _dot(fm_ref[0], w_out_ref[0:F_WIDTH, :]) + _dot(ag_ref[0], w_out_ref[F_WIDTH:, :])
    x1 = x_ref[0] + g0_ref[0] * y
    x1_ref[0] = x1
    h = (_rms(x1, ng_ref[...]) * (1.0 + sc_ref[0]) + sh_ref[0]).astype(BF16)
    qt = _dot_nt(w_qt_ref[...], h)
    cos = tqt_ref[0]
    sin = tqt_ref[1]
    half = GQA_HEAD_DIM // 2
    for hd in range(GQA_HEADS):
        lo = hd * GQA_HEAD_DIM
        x1r = qt[lo:lo + half]
        x2r = qt[lo + half:lo + GQA_HEAD_DIM]
        qh = jnp.concatenate([x1r * cos - x2r * sin, x2r * cos + x1r * sin], axis=0).astype(BF16)
        g, j = divmod(hd, GQA_GROUP)
        for blk in range(tm // BLOCK):
            q_ref[0, blk, g, :, j * BLOCK:(j + 1) * BLOCK] = qh[:, blk * BLOCK:(blk + 1) * BLOCK]
    vt = _dot_nt(w_vt_ref[...], h) + vone_ref[...]
    for g in range(GQA_KV_HEADS):
        k_ref[0, g] = _rope(_dot(h, w_k_ref[:, g * LANES:(g + 1) * LANES]), tk_ref, 32).astype(BF16)
        v_ref[0, g] = vt[g * GQA_VT_ROWS:(g + 1) * GQA_VT_ROWS].astype(BF16)
    sg_ref[0] = _silu(_dot(h, w_g_ref[...])).astype(BF16)


def _mid_call(fm, ag, x, g0, w_out, ng, sc, sh, w_qt, w_k, w_vt, vone, w_g, tqt, tk, tm):
    b, n, d = x.shape
    row = lambda w: pl.BlockSpec((1, tm, w), lambda bi, i: (bi, i, 0))
    vec_spec = pl.BlockSpec((1, 1, d), lambda bi, i: (bi, 0, 0))
    nblk = tm // BLOCK
    return pl.pallas_call(
        _mid_kernel,
        grid=(b, n // tm),
        in_specs=[row(F_WIDTH), row(MLA_WIDTH), row(d), vec_spec, _full(w_out.shape), _full(ng.shape),
                  vec_spec, vec_spec, _full(w_qt.shape), _full(w_k.shape), _full(w_vt.shape), _full(vone.shape),
                  _full(w_g.shape),
                  pl.BlockSpec((2, GQA_HEAD_DIM // 2, tm), lambda bi, i: (0, 0, i)),
                  pl.BlockSpec((3, tm, LANES), lambda bi, i: (0, i, 0))],
        out_specs=[row(d),
                   pl.BlockSpec((1, nblk, GQA_KV_HEADS, GQA_HEAD_DIM, GQA_GROUP * BLOCK),
                                lambda bi, i: (bi, i, 0, 0, 0)),
                   pl.BlockSpec((1, GQA_KV_HEADS, tm, LANES), lambda bi, i: (bi, 0, i, 0)),
                   pl.BlockSpec((1, GQA_KV_HEADS, GQA_VT_ROWS, tm), lambda bi, i: (bi, 0, 0, i)),
                   row(GQA_Q)],
        out_shape=[jax.ShapeDtypeStruct((b, n, d), F32),
                   jax.ShapeDtypeStruct((b, n // BLOCK, GQA_KV_HEADS, GQA_HEAD_DIM, GQA_GROUP * BLOCK), BF16),
                   jax.ShapeDtypeStruct((b, GQA_KV_HEADS, n, LANES), BF16),
                   jax.ShapeDtypeStruct((b, GQA_KV_HEADS, GQA_VT_ROWS, n), BF16),
                   jax.ShapeDtypeStruct((b, n, GQA_Q), BF16)],
        compiler_params=_cparams(("parallel", "arbitrary")),
        name="mid",
    )(fm, ag, x, g0, w_out, ng, sc, sh, w_qt, w_k, w_vt, vone, w_g, tqt, tk)


def _band_bias(nc):
    col = np.arange(3 * BLOCK)[:, None]
    q = np.arange(BLOCK)[None, :]
    dist = BLOCK + q - col
    band = np.abs(dist) <= WINDOW
    variants = [band & (col >= BLOCK), band, band & (col < 2 * BLOCK)]
    out = np.zeros((3, nc + 3 * BLOCK, BLOCK), np.float32)
    for v, ok in enumerate(variants):
        out[v, nc:] = np.where(ok, 0.0, NEG_BIG)
    return jnp.asarray(out)


def _attn1_kernel(sink_ref, q_ref, kp_ref, kc_ref, kn_ref, vp_ref, vc_ref, vn_ref, ck_ref, cv_ref, bias_ref,
                  sg_ref, x1_ref, g1_ref, w_out_ref, fg_ref, o_ref):
    i = pl.program_id(1)
    last = pl.num_programs(1) - 1
    sel = jnp.where(i == 0, 0, jnp.where(i == last, 2, 1))
    bias = bias_ref[sel]
    bias4 = jnp.concatenate([bias] * GQA_GROUP, axis=1)
    chunks = [None] * (GQA_Q // LANES)
    for g in range(GQA_KV_HEADS):
        kb = jnp.concatenate([ck_ref[0, g], kp_ref[0, g], kc_ref[0, g], kn_ref[0, g]], axis=0)[:, :GQA_HEAD_DIM]
        vt = jnp.concatenate([cv_ref[0, g], vp_ref[0, g], vc_ref[0, g], vn_ref[0, g]], axis=1)
        s = _dot(kb, q_ref[0, 0, g]) + bias4
        sink = jnp.concatenate(
            [jnp.full((1, BLOCK), sink_ref[GQA_GROUP * g + j] * LOG2E, F32) for j in range(GQA_GROUP)], axis=1)
        m = jnp.maximum(jnp.max(s, axis=0, keepdims=True), sink)
        p = jnp.exp2(s - m).astype(BF16)
        acc = _dot(vt, p)
        l = acc[GQA_HEAD_DIM:GQA_HEAD_DIM + 1] + jnp.exp2(sink - m)
        ot = acc[0:GQA_HEAD_DIM] / l
        for p2 in range(2):
            lo = 2 * p2 * BLOCK
            pair = jnp.concatenate([ot[:, lo:lo + BLOCK], ot[:, lo + BLOCK:lo + 2 * BLOCK]], axis=0)
            chunks[2 * g + p2] = pair.T
    o = jnp.concatenate(chunks, axis=-1)
    og = (o * sg_ref[0].astype(F32)).astype(BF16)
    y = _dot(og, w_out_ref[...])
    x2 = x1_ref[0] + g1_ref[0] * y
    o_ref[0] = _rms(x2, fg_ref[...])


def _attn1_call(sink, qt, k, vt, ck, cvt, bias, sg, x1, g1, w_out, fg):
    b, n, d = x1.shape
    nc = ck.shape[2]
    nb = n // BLOCK
    row = lambda w: pl.BlockSpec((1, BLOCK, w), lambda bi, i: (bi, i, 0))
    prv = lambda i: jnp.maximum(i - 1, 0)
    nxt = lambda i: jnp.minimum(i + 1, nb - 1)
    k_spec = lambda f: pl.BlockSpec((1, GQA_KV_HEADS, BLOCK, LANES), lambda bi, i: (bi, 0, f(i), 0))
    v_spec = lambda f: pl.BlockSpec((1, GQA_KV_HEADS, GQA_VT_ROWS, BLOCK), lambda bi, i: (bi, 0, 0, f(i)))
    cur = lambda i: i
    vec_spec = pl.BlockSpec((1, 1, d), lambda bi, i: (bi, 0, 0))
    return pl.pallas_call(
        _attn1_kernel,
        grid=(b, nb),
        in_specs=[pl.BlockSpec(memory_space=pltpu.SMEM),
                  pl.BlockSpec((1, 1, GQA_KV_HEADS, GQA_HEAD_DIM, GQA_GROUP * BLOCK), lambda bi, i: (bi, i, 0, 0, 0)),
                  k_spec(prv), k_spec(cur), k_spec(nxt), v_spec(prv), v_spec(cur), v_spec(nxt),
                  pl.BlockSpec((1, GQA_KV_HEADS, nc, LANES), lambda bi, i: (bi, 0, 0, 0)),
                  pl.BlockSpec((1, GQA_KV_HEADS, GQA_VT_ROWS, nc), lambda bi, i: (bi, 0, 0, 0)),
                  _full(bias.shape), row(GQA_Q), row(d), vec_spec, _full(w_out.shape), _full(fg.shape)],
        out_specs=row(d),
        out_shape=jax.ShapeDtypeStruct((b, n, d), F32),
        compiler_params=_cparams(("parallel", "arbitrary")),
        name="attn1",
    )(sink, qt, k, k, k, vt, vt, vt, ck, cvt, bias, sg, x1, g1, w_out, fg)


def _prep_even_weights(w_in, w_qb, w_kvb):
    d = w_in.shape[0]
    f_in, f_gate, q_a, kv_a, k_pe, m_gate = jnp.split(
        w_in, np.cumsum([F_WIDTH, F_WIDTH, MLA_Q_RANK, MLA_KV_RANK, MLA_ROPE]).tolist(), axis=1)
    kpe_blk = jnp.concatenate([jnp.zeros((d, MLA_NOPE), F32), k_pe, jnp.zeros((d, 32), F32)], axis=1)
    w_in_p = jnp.concatenate([f_in, f_gate, m_gate, q_a, kv_a, kpe_blk], axis=1).astype(BF16)
    qb = w_qb.reshape(MLA_Q_RANK, MLA_HEADS, MLA_NOPE + MLA_ROPE)
    w_qbt = jnp.pad(qb, ((0, 0), (0, 0), (0, 32))).reshape(MLA_Q_RANK, MLA_HEADS * LANES).T.astype(BF16)
    kvb = w_kvb.reshape(MLA_KV_RANK, MLA_HEADS, MLA_NOPE + MLA_V)
    w_k = jnp.pad(kvb[:, :, :MLA_NOPE], ((0, 0), (0, 0), (0, 64))).reshape(MLA_KV_RANK, MLA_HEADS * LANES)
    w_vt = jnp.pad(kvb[:, :, MLA_NOPE:], ((0, 0), (0, 0), (0, MLA_VT_ROWS - MLA_V)))
    w_vt = w_vt.reshape(MLA_KV_RANK, MLA_HEADS * MLA_VT_ROWS).T.astype(BF16)
    vone = np.zeros((MLA_HEADS * MLA_VT_ROWS, 1), np.float32)
    vone[MLA_V::MLA_VT_ROWS] = 1.0
    return w_in_p, w_qbt, w_k.astype(BF16), w_vt, jnp.asarray(vone)


def _prep_odd_weights(w_in):
    d = w_in.shape[0]
    w_qt = w_in[:, O_Q:O_K].T.astype(BF16)
    kw = w_in[:, O_K:O_V].reshape(d, GQA_KV_HEADS, GQA_HEAD_DIM)
    w_k = jnp.pad(kw, ((0, 0), (0, 0), (0, LANES - GQA_HEAD_DIM))).reshape(d, GQA_KV_HEADS * LANES).astype(BF16)
    vw = w_in[:, O_V:O_G].reshape(d, GQA_KV_HEADS, GQA_HEAD_DIM)
    w_vt = jnp.pad(vw, ((0, 0), (0, 0), (0, GQA_VT_ROWS - GQA_HEAD_DIM)))
    w_vt = w_vt.reshape(d, GQA_KV_HEADS * GQA_VT_ROWS).T.astype(BF16)
    vone = np.zeros((GQA_KV_HEADS * GQA_VT_ROWS, 1), np.float32)
    vone[GQA_HEAD_DIM::GQA_VT_ROWS] = 1.0
    return w_qt, w_k, w_vt, jnp.asarray(vone), w_in[:, O_G:O_END].astype(BF16)


def _fft_consts(n, c):
    n1 = n // LANES
    c1, s1 = _dft_cs(n1)
    fa = np.block([[c1, s1], [-s1, c1]])
    c2, s2 = _dft_cs(LANES)
    fcs = np.concatenate([c2, s2], axis=1) / math.sqrt(n * F_GROUP_DIM)
    fc = np.concatenate([c2, -s2], axis=1)
    r = np.arange(LANES, dtype=np.int64)[:, None]
    k1 = np.arange(n1, dtype=np.int64)[None, :]
    ang = 2.0 * np.pi * ((r * k1) % n).astype(np.float64) / n
    tw = np.stack([np.cos(ang), np.sin(ang)])
    tw = np.broadcast_to(tw[..., None], (2, LANES, n1, LANES))
    cn, sn = _dft_cs(c)
    fn = np.concatenate([cn, sn], axis=1)
    as32 = lambda a: jnp.asarray(np.ascontiguousarray(a, dtype=np.float32))
    return as32(fa).astype(BF16), as32(fcs).astype(BF16), as32(fc).astype(BF16), as32(tw), as32(fn).astype(BF16)


def _pick_tile(n, pref):
    t = pref
    while n % t:
        t //= 2
    return t


def kernel(x, c, ctx, c_ctx, w_mod, b_mod, norm_g, e_w_in, e_q_norm, e_w_qb, e_kv_norm, e_w_kvb, e_w_out,
           o_w_in, o_sink, o_w_out, final_g):
    b, n, d = x.shape
    nc = ctx.shape[1]
    assert d == D_MODEL and n % (LANES * 8) == 0 and nc % LANES == 0 and b <= 7

    svec = jnp.zeros((8, d), F32).at[:b].set(c).at[b].set(c_ctx)
    mod = _mod_call(svec, w_mod, b_mod)
    sh, sc, gt = mod[:, :, :d], mod[:, :, d:2 * d], mod[:, :, 2 * d:]
    lat = lambda t, l: t[l, :b].reshape(b, 1, d)
    cvec = lambda t, l: t[l, b].reshape(1, d)
    ng0, ng1 = norm_g[0].reshape(1, d), norm_g[1].reshape(1, d)

    cos_m, sin_m = _axial_rope_tables(n, MLA_ROPE)
    qs = MLA_SCALE * LOG2E
    tqt0 = jnp.stack([cos_m.T, sin_m.T]) * qs
    tk0 = _mla_rope_tab(cos_m, sin_m)
    one_c, zero_c = jnp.ones((nc, MLA_ROPE // 2), F32), jnp.zeros((nc, MLA_ROPE // 2), F32)
    tqt0c = jnp.stack([one_c.T, zero_c.T]) * qs
    tk0c = _mla_rope_tab(one_c, zero_c)
    cos_g, sin_g = _axial_rope_tables(n, GQA_HEAD_DIM)
    tqt1 = jnp.stack([cos_g.T, sin_g.T]) * (GQA_SCALE * LOG2E)
    tk1 = _gqa_rope_tab(cos_g, sin_g)
    fa, fcs, fc, tw, fn = _fft_consts(n, nc)
    bias1 = _band_bias(nc)

    w_in0, w_qbt0, w_k0, w_vt0, vone = _prep_even_weights(e_w_in[0], e_w_qb[0], e_w_kvb[0])
    qn0 = e_q_norm[0].reshape(1, MLA_Q_RANK)
    kvn0 = e_kv_norm[0].reshape(1, MLA_KV_RANK)
    w_out0 = e_w_out[0].astype(BF16)
    w_qt1, w_k1, w_vt1, vone1, w_g1 = _prep_odd_weights(o_w_in[0])
    w_out1 = o_w_out[0].astype(BF16)

    kc, vct, ck1, cvt1 = _ctx_call(
        ctx, (ng0, cvec(sc, 0), cvec(sh, 0), cvec(gt, 0)), (ng1, cvec(sc, 1), cvec(sh, 1)),
        w_in0, qn0, w_qbt0, kvn0, w_k0, w_vt0, vone, tqt0c, tk0c, fc, fn, w_out0, w_k1, w_vt1, vone1)

    tm = _pick_tile(n, 512)
    qt0, k0, vt0, zr, zi, gf, gm = _front0_call(
        x, ng0, lat(sc, 0), lat(sh, 0), w_in0, qn0, w_qbt0, kvn0, w_k0, w_vt0, vone, tqt0, tk0, fc, tm)

    ag = _attn0_call(qt0, kc, vct, k0, vt0, gm, _pick_tile(n, 512))

    n1 = n // LANES
    tr, ti = _fft_a_call(zr.reshape(b, n1, LANES * F_WIDTH), zi.reshape(b, n1, LANES * F_WIDTH), fa, tw, 8)
    fmg = _fft_c_call(tr.reshape(b, n1, LANES, F_WIDTH), ti.reshape(b, n1, LANES, F_WIDTH), fcs,
                      gf.reshape(b, LANES, n1 * F_WIDTH), _pick_tile(n1, 8))
    fmg = fmg.reshape(b, n, F_WIDTH)

    tm1 = _pick_tile(n, 256)
    x1, qt1, k1, vt1, sg = _mid_call(fmg, ag, x, lat(gt, 0), w_out0, ng1, lat(sc, 1), lat(sh, 1),
                                     w_qt1, w_k1, w_vt1, vone1, w_g1, tqt1, tk1, tm1)

    return _attn1_call(o_sink[0], qt1, k1, vt1, ck1, cvt1, bias1, sg, x1, lat(gt, 1), w_out1,
                       final_g.reshape(1, d))
```

```python
import functools
import math

import numpy as np
import jax
import jax.numpy as jnp
from jax import lax
from jax.experimental import pallas as pl
from jax.experimental.pallas import tpu as pltpu

F32 = jnp.float32
BF16 = jnp.bfloat16

D_MODEL = 1024
GRID_W = 64
EPS = 1e-6
ROPE_BASE = 10000.0
LANES = 128

F_GROUPS = 4
F_GROUP_DIM = 128
F_WIDTH = F_GROUPS * F_GROUP_DIM

MLA_HEADS = 8
MLA_NOPE = 64
MLA_ROPE = 32
MLA_V = 64
MLA_Q_RANK = 384
MLA_KV_RANK = 256
MLA_WIDTH = MLA_HEADS * MLA_V
MLA_SCALE = 1.0 / math.sqrt(MLA_NOPE + MLA_ROPE)
LOG2E = math.log2(math.e)
MLA_VT_ROWS = 80
PV_KEYS = 256
MLA_TQ = 256
SOFTMAX_GROUP = 2
PIPE_SKEW = 3
SCORE_SLOTS = 4
GATE_LAG = 2

GQA_HEADS = 16
GQA_KV_HEADS = 4
GQA_GROUP = GQA_HEADS // GQA_KV_HEADS
GQA_HEAD_DIM = 64
WINDOW = 128
BLOCK = 128
GQA_Q = GQA_HEADS * GQA_HEAD_DIM
GQA_KV = GQA_KV_HEADS * GQA_HEAD_DIM
GQA_SCALE = 1.0 / math.sqrt(GQA_HEAD_DIM)
GQA_VT_ROWS = 80
A1_BLOCKS = 4
A1_SKEW = 5
A1_SLOTS = 6
A1_GATE_LAG = 3

E_FIN, E_FGATE, E_MGATE, E_QA, E_KVA, E_KPE, E_END = 0, 512, 1024, 1536, 1920, 2176, 2304
O_Q, O_K, O_V, O_G, O_END = 0, 1024, 1280, 1536, 2560

NEG_BIG = -1e30
VMEM_LIMIT = 56 * 1024 * 1024


def _cparams(sem):
    return pltpu.CompilerParams(dimension_semantics=sem, vmem_limit_bytes=VMEM_LIMIT)


def _dot(a, b):
    return jnp.dot(a, b, preferred_element_type=F32)


def _dot_nt(a, b):
    return lax.dot_general(a, b, (((1,), (1,)), ((), ())), preferred_element_type=F32)


def _rms(x, g):
    return x * lax.rsqrt(jnp.mean(x * x, axis=-1, keepdims=True) + EPS) * g


def _silu(x):
    return x * jax.nn.sigmoid(x)


def _rope(x, tab_ref, shift):
    up = pltpu.roll(x, LANES - shift, 1)
    dn = pltpu.roll(x, shift, 1)
    return x * tab_ref[0] + up * tab_ref[1] + dn * tab_ref[2]


def _full(shape):
    nd = len(shape)
    return pl.BlockSpec(shape, lambda *_: (0,) * nd)


def _axial_rope_tables(n, rot_dim):
    rows = n // GRID_W
    row = np.repeat(np.arange(rows), GRID_W).astype(np.float64)
    col = np.tile(np.arange(GRID_W), rows).astype(np.float64)
    nf = rot_dim // 4
    inv = (np.float32(ROPE_BASE) ** (-np.arange(nf, dtype=np.float32) / np.float32(nf))).astype(np.float64)
    ang = np.concatenate([row[:, None] * inv, col[:, None] * inv], axis=-1)
    return np.cos(ang), np.sin(ang)


def _mla_rope_tab(cos, sin):
    n = cos.shape[0]
    z = lambda w: np.zeros((n, w))
    cf = np.concatenate([np.ones((n, MLA_NOPE)), cos, cos, z(32)], axis=-1)
    s1 = np.concatenate([z(MLA_NOPE), -sin, z(16), z(32)], axis=-1)
    s2 = np.concatenate([z(MLA_NOPE), z(16), sin, z(32)], axis=-1)
    return _const(np.stack([cf, s1, s2]))


def _gqa_rope_tab(cos, sin):
    z = np.zeros_like(sin)
    cf = np.concatenate([cos, cos, z, z], axis=-1)
    s1 = np.concatenate([-sin, z, z, z], axis=-1)
    s2 = np.concatenate([z, sin, z, z], axis=-1)
    return _const(np.stack([cf, s1, s2]))


def _const(a):
    return jnp.asarray(np.ascontiguousarray(a, dtype=np.float32))


def _dft_cs(n):
    idx = np.arange(n, dtype=np.int64)
    ang = 2.0 * np.pi * ((idx[:, None] * idx[None, :]) % n).astype(np.float64) / n
    return np.cos(ang), np.sin(ang)


def _mod_kernel(s_ref, w_ref, b_ref, o_ref):
    s = _silu(s_ref[...])
    o_ref[0] = _dot(s.astype(BF16), w_ref[0].astype(BF16)) + b_ref[0]


def _mod_call(svec, w_mod, b_mod):
    depth, d, d3 = w_mod.shape
    tn = 768
    return pl.pallas_call(
        _mod_kernel,
        grid=(depth, d3 // tn),
        in_specs=[
            pl.BlockSpec((8, d), lambda l, j: (0, 0)),
            pl.BlockSpec((1, d, tn), lambda l, j: (l, 0, j)),
            pl.BlockSpec((1, 1, tn), lambda l, j: (l, 0, j)),
        ],
        out_specs=pl.BlockSpec((1, 8, tn), lambda l, j: (l, 0, j)),
        out_shape=jax.ShapeDtypeStruct((depth, 8, d3), F32),
        compiler_params=_cparams(("arbitrary", "arbitrary")),
        name="mod",
    )(svec, w_mod, b_mod.reshape(depth, 1, d3))


def _even_front(x, ng, sc, sh, w_in_ref, qn, w_qbt_ref, kvn, w_k_ref, w_vt_ref, vone, tqt_ref, tk_ref, fc_ref):
    m = x.shape[0]
    h = (_rms(x, ng) * (1.0 + sc) + sh).astype(BF16)
    f_in = _dot(h, w_in_ref[:, E_FIN:E_FGATE])
    f_gate = _dot(h, w_in_ref[:, E_FGATE:E_MGATE])
    m_gate = _dot(h, w_in_ref[:, E_MGATE:E_QA])
    q_a = _dot(h, w_in_ref[:, E_QA:E_KVA])
    kv_a = _dot(h, w_in_ref[:, E_KVA:E_KPE])
    kpe = _dot(h, w_in_ref[:, E_KPE:E_END])
    qh = _rms(q_a, qn).astype(BF16)
    ch = _rms(kv_a, kvn).astype(BF16)
    kpe_r = _rope(kpe, tk_ref, 16)
    qt = _dot_nt(w_qbt_ref[...], qh)
    cos = tqt_ref[0]
    sin = tqt_ref[1]
    pad = jnp.zeros((LANES - MLA_NOPE - MLA_ROPE, m), F32)
    qts, ks = [], []
    for hd in range(MLA_HEADS):
        lo = hd * LANES
        x1 = qt[lo + MLA_NOPE:lo + MLA_NOPE + 16]
        x2 = qt[lo + MLA_NOPE + 16:lo + MLA_NOPE + 32]
        qts.append(jnp.concatenate(
            [qt[lo:lo + MLA_NOPE] * (MLA_SCALE * LOG2E), x1 * cos - x2 * sin, x2 * cos + x1 * sin, pad], axis=0))
        ks.append(_dot(ch, w_k_ref[:, lo:lo + LANES]) + kpe_r)
    vt = _dot_nt(w_vt_ref[...], ch) + vone
    zr, zi = [], []
    fb = f_in.astype(BF16)
    for g in range(F_GROUPS):
        z = _dot(fb[:, g * LANES:(g + 1) * LANES], fc_ref[...])
        zr.append(z[:, :LANES])
        zi.append(z[:, LANES:])
    return qts, ks, vt, zr, zi, f_gate, m_gate


def _front0_kernel(x_ref, ng_ref, sc_ref, sh_ref, w_in_ref, qn_ref, w_qbt_ref, kvn_ref, w_k_ref, w_vt_ref,
                   vone_ref, tqt_ref, tk_ref, fc_ref, q_ref, k_ref, v_ref, zr_ref, zi_ref, gf_ref, gm_ref):
    qts, ks, vt, zr, zi, f_gate, m_gate = _even_front(
        x_ref[0], ng_ref[...], sc_ref[0], sh_ref[0], w_in_ref, qn_ref[...], w_qbt_ref, kvn_ref[...],
        w_k_ref, w_vt_ref, vone_ref[...], tqt_ref, tk_ref, fc_ref)
    for hd in range(MLA_HEADS):
        q_ref[0, hd] = qts[hd].astype(BF16)
        k_ref[0, hd] = ks[hd].astype(BF16)
        v_ref[0, hd, 0] = vt[hd * MLA_VT_ROWS:(hd + 1) * MLA_VT_ROWS].astype(BF16)
    for g in range(F_GROUPS):
        zr_ref[0, :, g * LANES:(g + 1) * LANES] = zr[g].astype(BF16)
        zi_ref[0, :, g * LANES:(g + 1) * LANES] = zi[g].astype(BF16)
    gf_ref[0] = _silu(f_gate).astype(BF16)
    gm_ref[0] = _silu(m_gate).astype(BF16)


def _front0_call(x, ng, sc, sh, w_in, qn, w_qbt, kvn, w_k, w_vt, vone, tqt, tk, fc, tm):
    b, n, d = x.shape
    fw = jax.ShapeDtypeStruct((b, n, F_WIDTH), BF16)
    row_spec = pl.BlockSpec((1, tm, F_WIDTH), lambda bi, i: (bi, i, 0))
    vec_spec = pl.BlockSpec((1, 1, d), lambda bi, i: (bi, 0, 0))
    return pl.pallas_call(
        _front0_kernel,
        grid=(b, n // tm),
        in_specs=[
            pl.BlockSpec((1, tm, d), lambda bi, i: (bi, i, 0)),
            _full(ng.shape), vec_spec, vec_spec,
            _full(w_in.shape), _full(qn.shape), _full(w_qbt.shape), _full(kvn.shape), _full(w_k.shape),
            _full(w_vt.shape), _full(vone.shape),
            pl.BlockSpec((2, 16, tm), lambda bi, i: (0, 0, i)),
            pl.BlockSpec((3, tm, LANES), lambda bi, i: (0, i, 0)),
            _full(fc.shape),
        ],
        out_specs=[
            pl.BlockSpec((1, MLA_HEADS, LANES, tm), lambda bi, i: (bi, 0, 0, i)),
            pl.BlockSpec((1, MLA_HEADS, tm, LANES), lambda bi, i: (bi, 0, i, 0)),
            pl.BlockSpec((1, MLA_HEADS, 1, MLA_VT_ROWS, tm), lambda bi, i: (bi, 0, i, 0, 0)),
            row_spec, row_spec, row_spec, row_spec],
        out_shape=[
            jax.ShapeDtypeStruct((b, MLA_HEADS, LANES, n), BF16),
            jax.ShapeDtypeStruct((b, MLA_HEADS, n, LANES), BF16),
            jax.ShapeDtypeStruct((b, MLA_HEADS, n // tm, MLA_VT_ROWS, tm), BF16),
            fw, fw, fw, fw],
        compiler_params=_cparams(("parallel", "arbitrary")),
        name="front0",
    )(x, ng, sc, sh, w_in, qn, w_qbt, kvn, w_k, w_vt, vone, tqt, tk, fc)


def _zero_after(x):
    u = lax.bitcast_convert_type(x, jnp.uint32)
    z = lax.shift_right_logical(lax.shift_right_logical(u, jnp.uint32(16)), jnp.uint32(16))
    return lax.bitcast_convert_type(z, F32)


def _mla_chunk(qt, k, vt, m, acc):
    s = _dot(k, qt)
    m_new = jnp.maximum(m, jnp.max(s, axis=0, keepdims=True))
    alpha = jnp.exp2(m - m_new)
    p = jnp.exp2(s - m_new).astype(BF16)
    return m_new, alpha * acc + _dot(vt, p)


def _mla_pair_out(acc0, acc1):
    o0 = acc0[0:MLA_V] / acc0[MLA_V:MLA_V + 1]
    o1 = acc1[0:MLA_V] / acc1[MLA_V:MLA_V + 1]
    return jnp.concatenate([o0, o1], axis=0).T


def _ctx_kernel(ctx_scale, x_ref, ng0_ref, sc0_ref, sh0_ref, g0_ref, ng1_ref, sc1_ref, sh1_ref,
                w_in_ref, qn_ref, w_qbt_ref, kvn_ref, w_k_ref, w_vt_ref, vone_ref, tqt_ref, tk_ref, fc_ref,
                fn_ref, w_out_ref, w_k1_ref, w_vt1_ref, vone1_ref,
                kc_ref, vc_ref, ck1_ref, cv1_ref):
    x = x_ref[0]
    c = x.shape[0]
    qts, ks, vt, zr, zi, f_gate, m_gate = _even_front(
        x, ng0_ref[...], sc0_ref[...], sh0_ref[...], w_in_ref, qn_ref[...], w_qbt_ref, kvn_ref[...],
        w_k_ref, w_vt_ref, vone_ref[...], tqt_ref, tk_ref, fc_ref)
    pairs = []
    for hp in range(MLA_HEADS // 2):
        accs = []
        for e in range(2):
            hd = 2 * hp + e
            kb = ks[hd].astype(BF16)
            vtb = vt[hd * MLA_VT_ROWS:(hd + 1) * MLA_VT_ROWS].astype(BF16)
            kc_ref[0, hd] = kb
            vc_ref[0, hd, 0] = vtb
            m0 = jnp.full((1, c), -jnp.inf, F32)
            acc0 = jnp.zeros((MLA_VT_ROWS, c), F32)
            accs.append(_mla_chunk(qts[hd].astype(BF16), kb, vtb, m0, acc0)[1])
        pairs.append(_mla_pair_out(accs[0], accs[1]))
    a = jnp.concatenate(pairs, axis=-1)
    z = jnp.concatenate([jnp.concatenate(zr, axis=-1), jnp.concatenate(zi, axis=-1)], axis=0).astype(BF16)
    fm = _dot(fn_ref[...], z) * ctx_scale
    yf = (fm * _silu(f_gate)).astype(BF16)
    ya = (a * _silu(m_gate)).astype(BF16)
    y = _dot(yf, w_out_ref[0:F_WIDTH, :]) + _dot(ya, w_out_ref[F_WIDTH:, :])
    x1 = x + g0_ref[...] * y
    h1 = (_rms(x1, ng1_ref[...]) * (1.0 + sc1_ref[...]) + sh1_ref[...]).astype(BF16)
    vt1 = _dot_nt(w_vt1_ref[...], h1) + vone1_ref[...]
    for g in range(GQA_KV_HEADS):
        ck1_ref[0, g] = _dot(h1, w_k1_ref[:, g * LANES:(g + 1) * LANES]).astype(BF16)
        cv1_ref[0, g] = vt1[g * GQA_VT_ROWS:(g + 1) * GQA_VT_ROWS].astype(BF16)


def _ctx_call(ctx, vecs0, vecs1, w_in, qn, w_qbt, kvn, w_k, w_vt, vone, tqt, tk, fc, fn, w_out, w_k1, w_vt1,
              vone1):
    b, c, d = ctx.shape
    ng0, sc0, sh0, g0 = vecs0
    ng1, sc1, sh1 = vecs1
    ctx_scale = 1.0 / math.sqrt(c * F_GROUP_DIM)
    consts = [ng0, sc0, sh0, g0, ng1, sc1, sh1, w_in, qn, w_qbt, kvn, w_k, w_vt, vone, tqt, tk, fc, fn, w_out,
              w_k1, w_vt1, vone1]
    return pl.pallas_call(
        functools.partial(_ctx_kernel, ctx_scale),
        grid=(b,),
        in_specs=[pl.BlockSpec((1, c, d), lambda bi: (bi, 0, 0))] + [_full(a.shape) for a in consts],
        out_specs=[pl.BlockSpec((1, MLA_HEADS, c, LANES), lambda bi: (bi, 0, 0, 0)),
                   pl.BlockSpec((1, MLA_HEADS, 1, MLA_VT_ROWS, c), lambda bi: (bi, 0, 0, 0, 0)),
                   pl.BlockSpec((1, GQA_KV_HEADS, c, LANES), lambda bi: (bi, 0, 0, 0)),
                   pl.BlockSpec((1, GQA_KV_HEADS, GQA_VT_ROWS, c), lambda bi: (bi, 0, 0, 0))],
        out_shape=[jax.ShapeDtypeStruct((b, MLA_HEADS, c, LANES), BF16),
                   jax.ShapeDtypeStruct((b, MLA_HEADS, 1, MLA_VT_ROWS, c), BF16),
                   jax.ShapeDtypeStruct((b, GQA_KV_HEADS, c, LANES), BF16),
                   jax.ShapeDtypeStruct((b, GQA_KV_HEADS, GQA_VT_ROWS, c), BF16)],
        compiler_params=_cparams(("arbitrary",)),
        name="ctx",
    )(ctx, *consts)


def _attn0_kernel(qt_ref, kc_ref, vct_ref, kl_ref, vlt_ref, gm_ref, o_ref, s_ref):
    tq = qt_ref.shape[-1]
    nch, _, tk = vlt_ref.shape[2:]
    nc = kc_ref.shape[2]
    n_ctx = nc // PV_KEYS
    n_sub = n_ctx + nch * tk // PV_KEYS
    group = [0] * n_ctx + [1 + j // SOFTMAX_GROUP for j in range(n_sub - n_ctx)]
    first = [t for t in range(n_sub) if t == 0 or group[t] != group[t - 1]]
    last = [t for t in range(n_sub) if t == n_sub - 1 or group[t] != group[t + 1]]

    def keys(e, t):
        if t < n_ctx:
            return kc_ref[0, e, t * PV_KEYS:(t + 1) * PV_KEYS, :]
        lo = (t - n_ctx) * PV_KEYS
        return kl_ref[0, e, lo:lo + PV_KEYS, :]

    def values_t(e, t):
        if t < n_ctx:
            return vct_ref[0, e, 0, :, t * PV_KEYS:(t + 1) * PV_KEYS]
        j, off = divmod((t - n_ctx) * PV_KEYS, tk)
        return vlt_ref[0, e, j, :, off:off + PV_KEYS]

    def slot(i):
        r = i % SCORE_SLOTS
        return slice(r * PV_KEYS, (r + 1) * PV_KEYS)

    tasks = [(qi, t) for qi in range(tq // MLA_TQ) for t in range(n_sub)]
    colmax = []
    for step in range(len(tasks) + PIPE_SKEW):
        if step < len(tasks):
            qi, t = tasks[step]
            cols = slice(qi * MLA_TQ, (qi + 1) * MLA_TQ)
            cm = []
            for e in range(2):
                s = _dot(keys(e, t), qt_ref[0, e, :, cols])
                s_ref[e, slot(step)] = s
                cm.append(jnp.max(s, axis=0, keepdims=True))
            colmax.append(cm)
        if step < PIPE_SKEW:
            continue
        qi, u = tasks[step - PIPE_SKEW]
        if u == 0:
            m = [jnp.full((1, MLA_TQ), -jnp.inf, F32) for _ in range(2)]
            acc = [jnp.zeros((MLA_VT_ROWS, MLA_TQ), F32) for _ in range(2)]
        if u in first:
            members = [qi * n_sub + t for t in range(n_sub) if group[t] == group[u]]
            assert members[-1] <= step
            m_new, alpha = [], []
            for e in range(2):
                mc = functools.reduce(jnp.maximum, [colmax[i][e] for i in members])
                m_new.append(jnp.maximum(m[e], mc))
                alpha.append(jnp.exp2(m[e] - m_new[-1]))
            pv = [None, None]
        gate = min(step - GATE_LAG, len(tasks) - 1)
        for e in range(2):
            m_use = m_new[e] + _zero_after(colmax[gate][e])
            p = jnp.exp2(s_ref[e, slot(step - PIPE_SKEW)] - m_use).astype(BF16)
            d = _dot(values_t(e, u), p)
            pv[e] = d if pv[e] is None else pv[e] + d
        if u in last:
            for e in range(2):
                acc[e] = alpha[e] * acc[e] + pv[e]
                m[e] = m_new[e]
        if u == n_sub - 1:
            rows = slice(qi * MLA_TQ, (qi + 1) * MLA_TQ)
            o_ref[0, rows] = (_mla_pair_out(acc[0], acc[1]) * gm_ref[0, rows].astype(F32)).astype(BF16)


def _attn0_call(qt, kc, vct, kl, vlt, gm, tq):
    b, h, _, n = qt.shape
    c = kc.shape[2]
    nch, _, tk = vlt.shape[2:]
    assert c % PV_KEYS == 0 and tk % PV_KEYS == 0 and SCORE_SLOTS > PIPE_SKEW >= SOFTMAX_GROUP - 1
    assert tq % MLA_TQ == 0
    o_spec = pl.BlockSpec((1, tq, LANES), lambda bi, hp, i: (bi, i, hp))
    return pl.pallas_call(
        _attn0_kernel,
        grid=(b, h // 2, n // tq),
        in_specs=[
            pl.BlockSpec((1, 2, LANES, tq), lambda bi, hp, i: (bi, hp, 0, i)),
            pl.BlockSpec((1, 2, c, LANES), lambda bi, hp, i: (bi, hp, 0, 0)),
            pl.BlockSpec((1, 2, 1, MLA_VT_ROWS, c), lambda bi, hp, i: (bi, hp, 0, 0, 0)),
            pl.BlockSpec((1, 2, n, LANES), lambda bi, hp, i: (bi, hp, 0, 0)),
            pl.BlockSpec((1, 2, nch, MLA_VT_ROWS, tk), lambda bi, hp, i: (bi, hp, 0, 0, 0)),
            o_spec],
        out_specs=o_spec,
        out_shape=jax.ShapeDtypeStruct((b, n, MLA_WIDTH), BF16),
        scratch_shapes=[pltpu.VMEM((2, SCORE_SLOTS * PV_KEYS, MLA_TQ), F32)],
        compiler_params=_cparams(("parallel", "parallel", "arbitrary")),
        name="attn0",
    )(qt, kc, vct, kl, vlt, gm)


def _fft_a_kernel(t2, n1, zr_ref, zi_ref, fa_ref, tw_ref, tr_ref, ti_ref):
    z = jnp.concatenate([zr_ref[0], zi_ref[0]], axis=0)
    a = _dot(fa_ref[...], z)
    for j in range(t2):
        c = tw_ref[0, j]
        s = tw_ref[1, j]
        for g in range(F_GROUPS):
            lo = j * F_WIDTH + g * LANES
            ar = a[0:n1, lo:lo + LANES]
            ai = a[n1:2 * n1, lo:lo + LANES]
            tr_ref[0, :, lo:lo + LANES] = (ar * c + ai * s).astype(BF16)
            ti_ref[0, :, lo:lo + LANES] = (ai * c - ar * s).astype(BF16)


def _fft_a_call(zr, zi, fa, tw, t2):
    b, n1, w = zr.shape
    spec = pl.BlockSpec((1, n1, t2 * F_WIDTH), lambda bi, j: (bi, 0, j))
    out = jax.ShapeDtypeStruct((b, n1, w), BF16)
    return pl.pallas_call(
        functools.partial(_fft_a_kernel, t2, n1),
        grid=(b, w // (t2 * F_WIDTH)),
        in_specs=[spec, spec, _full(fa.shape), pl.BlockSpec((2, t2, n1, LANES), lambda bi, j: (0, j, 0, 0))],
        out_specs=[spec, spec],
        out_shape=[out, out],
        compiler_params=_cparams(("parallel", "arbitrary")),
        name="fft_a",
    )(zr, zi, fa, tw)


def _fft_c_kernel(t1, tr_ref, ti_ref, fcs_ref, gf_ref, o_ref):
    for k in range(t1):
        rhs = jnp.concatenate([tr_ref[0, k], ti_ref[0, k]], axis=0)
        lo = k * F_WIDTH
        out = _dot(fcs_ref[...], rhs)
        o_ref[0, :, lo:lo + F_WIDTH] = (out * gf_ref[0, :, lo:lo + F_WIDTH].astype(F32)).astype(BF16)


def _fft_c_call(tr, ti, fcs, gf, t1):
    b, n1, r, w = tr.shape
    t_spec = pl.BlockSpec((1, t1, r, w), lambda bi, i: (bi, i, 0, 0))
    g_spec = pl.BlockSpec((1, r, t1 * w), lambda bi, i: (bi, 0, i))
    return pl.pallas_call(
        functools.partial(_fft_c_kernel, t1),
        grid=(b, n1 // t1),
        in_specs=[t_spec, t_spec, _full(fcs.shape), g_spec],
        out_specs=g_spec,
        out_shape=jax.ShapeDtypeStruct((b, r, n1 * w), BF16),
        compiler_params=_cparams(("parallel", "arbitrary")),
        name="fft_c",
    )(tr, ti, fcs, gf)


def _mid_kernel(fm_ref, ag_ref, x_ref, g0_ref, w_out_ref, ng_ref, sc_ref, sh_ref, w_qt_ref, w_k_ref, w_vt_ref,
                vone_ref, w_g_ref, tqt_ref, tk_ref, x1_ref, q_ref, k_ref, v_ref, sg_ref):
    tm = x_ref.shape[1]
    y = _dot(fm_ref[0], w_out_ref[0:F_WIDTH, :]) + _dot(ag_ref[0], w_out_ref[F_WIDTH:, :])
    x1 = x_ref[0] + g0_ref[0] * y
    x1_ref[0] = x1
    h = (_rms(x1, ng_ref[...]) * (1.0 + sc_ref[0]) + sh_ref[0]).astype(BF16)
    qt = _dot_nt(w_qt_ref[...], h)
    cos = tqt_ref[0]
    sin = tqt_ref[1]
    half = GQA_HEAD_DIM // 2
    for hd in range(GQA_HEADS):
        lo = hd * GQA_HEAD_DIM
        x1r = qt[lo:lo + half]
        x2r = qt[lo + half:lo + GQA_HEAD_DIM]
        qh = jnp.concatenate([x1r * cos - x2r * sin, x2r * cos + x1r * sin], axis=0).astype(BF16)
        g, j = divmod(hd, GQA_GROUP)
        for blk in range(tm // BLOCK):
            q_ref[0, blk, g, :, j * BLOCK:(j + 1) * BLOCK] = qh[:, blk * BLOCK:(blk + 1) * BLOCK]
    vt = _dot_nt(w_vt_ref[...], h) + vone_ref[...]
    for g in range(GQA_KV_HEADS):
        k_ref[0, g] = _rope(_dot(h, w_k_ref[:, g * LANES:(g + 1) * LANES]), tk_ref, 32).astype(BF16)
        v_ref[0, g] = vt[g * GQA_VT_ROWS:(g + 1) * GQA_VT_ROWS].astype(BF16)
    sg_ref[0] = _silu(_dot(h, w_g_ref[...])).astype(BF16)


def _mid_call(fm, ag, x, g0, w_out, ng, sc, sh, w_qt, w_k, w_vt, vone, w_g, tqt, tk, tm):
    b, n, d = x.shape
    row = lambda w: pl.BlockSpec((1, tm, w), lambda bi, i: (bi, i, 0))
    vec_spec = pl.BlockSpec((1, 1, d), lambda bi, i: (bi, 0, 0))
    nblk = tm // BLOCK
    return pl.pallas_call(
        _mid_kernel,
        grid=(b, n // tm),
        in_specs=[row(F_WIDTH), row(MLA_WIDTH), row(d), vec_spec, _full(w_out.shape), _full(ng.shape),
                  vec_spec, vec_spec, _full(w_qt.shape), _full(w_k.shape), _full(w_vt.shape), _full(vone.shape),
                  _full(w_g.shape),
                  pl.BlockSpec((2, GQA_HEAD_DIM // 2, tm), lambda bi, i: (0, 0, i)),
                  pl.BlockSpec((3, tm, LANES), lambda bi, i: (0, i, 0))],
        out_specs=[row(d),
                   pl.BlockSpec((1, nblk, GQA_KV_HEADS, GQA_HEAD_DIM, GQA_GROUP * BLOCK),
                                lambda bi, i: (bi, i, 0, 0, 0)),
                   pl.BlockSpec((1, GQA_KV_HEADS, tm, LANES), lambda bi, i: (bi, 0, i, 0)),
                   pl.BlockSpec((1, GQA_KV_HEADS, GQA_VT_ROWS, tm), lambda bi, i: (bi, 0, 0, i)),
                   row(GQA_Q)],
        out_shape=[jax.ShapeDtypeStruct((b, n, d), F32),
                   jax.ShapeDtypeStruct((b, n // BLOCK, GQA_KV_HEADS, GQA_HEAD_DIM, GQA_GROUP * BLOCK), BF16),
                   jax.ShapeDtypeStruct((b, GQA_KV_HEADS, n, LANES), BF16),
                   jax.ShapeDtypeStruct((b, GQA_KV_HEADS, GQA_VT_ROWS, n), BF16),
                   jax.ShapeDtypeStruct((b, n, GQA_Q), BF16)],
        compiler_params=_cparams(("parallel", "arbitrary")),
        name="mid",
    )(fm, ag, x, g0, w_out, ng, sc, sh, w_qt, w_k, w_vt, vone, w_g, tqt, tk)


def _band_bias(nc):
    col = np.arange(3 * BLOCK)[:, None]
    q = np.arange(BLOCK)[None, :]
    dist = BLOCK + q - col
    band = np.abs(dist) <= WINDOW
    variants = [band & (col >= BLOCK), band, band & (col < 2 * BLOCK)]
    out = np.zeros((3, nc + 3 * BLOCK, BLOCK), np.float32)
    for v, ok in enumerate(variants):
        out[v, nc:] = np.where(ok, 0.0, NEG_BIG)
    return jnp.asarray(out)


def _attn1_kernel(sink_ref, q_ref, kp_ref, kc_ref, kn_ref, vp_ref, vc_ref, vn_ref, ck_ref, cv_ref, bias_ref,
                  sg_ref, x1_ref, g1_ref, w_out_ref, fg_ref, o_ref, s_ref):
    i = pl.program_id(1)
    last = pl.num_programs(1) - 1
    nc = ck_ref.shape[2]
    sel = []
    for blk in range(A1_BLOCKS):
        v = 1
        if blk == 0:
            v = jnp.where(i == 0, 0, v)
        if blk == A1_BLOCKS - 1:
            v = jnp.where(i == last, 2, v)
        sel.append(v)

    def band_keys(g, blk, j):
        pos = blk + j - 1
        if pos < 0:
            return kp_ref[0, g], vp_ref[0, g]
        if pos >= A1_BLOCKS:
            return kn_ref[0, g], vn_ref[0, g]
        return (kc_ref[0, g, pos * BLOCK:(pos + 1) * BLOCK, :], vc_ref[0, g, :, pos * BLOCK:(pos + 1) * BLOCK])

    def operands(blk, g, sb):
        if sb == 0:
            return ck_ref[0, g][:, :GQA_HEAD_DIM], cv_ref[0, g], None
        if sb == 1:
            (k0, v0), (k1, v1) = band_keys(g, blk, 0), band_keys(g, blk, 1)
            return (jnp.concatenate([k0, k1], axis=0)[:, :GQA_HEAD_DIM], jnp.concatenate([v0, v1], axis=1),
                    bias_ref[sel[blk], nc:nc + 2 * BLOCK])
        k2, v2 = band_keys(g, blk, 2)
        return k2[:, :GQA_HEAD_DIM], v2, bias_ref[sel[blk], nc + 2 * BLOCK:nc + 3 * BLOCK]

    def slot(idx, rows):
        r = idx % A1_SLOTS
        return slice(r * 2 * BLOCK, r * 2 * BLOCK + rows)

    tasks = [(blk, g, sb) for blk in range(A1_BLOCKS) for g in range(GQA_KV_HEADS) for sb in range(3)]
    colmax = []
    chunks = {}
    for step in range(len(tasks) + A1_SKEW):
        if step < len(tasks):
            blk, g, sb = tasks[step]
            kb, _, bias = operands(blk, g, sb)
            s = _dot(kb, q_ref[0, blk, g])
            if bias is not None:
                s = s + jnp.concatenate([bias] * GQA_GROUP, axis=1)
            s_ref[slot(step, s.shape[0])] = s
            colmax.append(jnp.max(s, axis=0, keepdims=True))
        if step < A1_SKEW:
            continue
        idx = step - A1_SKEW
        blk, g, sb = tasks[idx]
        _, vt, _ = operands(blk, g, sb)
        if sb == 0:
            assert idx + 2 <= step
            sink = jnp.concatenate(
                [jnp.full((1, BLOCK), sink_ref[GQA_GROUP * g + j] * LOG2E, F32) for j in range(GQA_GROUP)], axis=1)
            m = functools.reduce(jnp.maximum, [colmax[idx], colmax[idx + 1], colmax[idx + 2], sink])
            acc = None
        gate = min(step - A1_GATE_LAG, len(tasks) - 1)
        m_use = m + _zero_after(colmax[gate])
        p = jnp.exp2(s_ref[slot(idx, vt.shape[1])] - m_use).astype(BF16)
        d = _dot(vt, p)
        acc = d if acc is None else acc + d
        if sb == 2:
            l = acc[GQA_HEAD_DIM:GQA_HEAD_DIM + 1] + jnp.exp2(sink - m)
            ot = acc[0:GQA_HEAD_DIM] / l
            for p2 in range(2):
                lo = 2 * p2 * BLOCK
                pair = jnp.concatenate([ot[:, lo:lo + BLOCK], ot[:, lo + BLOCK:lo + 2 * BLOCK]], axis=0)
                chunks[(blk, 2 * g + p2)] = pair.T
            if g == GQA_KV_HEADS - 1:
                rows = slice(blk * BLOCK, (blk + 1) * BLOCK)
                o = jnp.concatenate([chunks[(blk, c)] for c in range(GQA_Q // LANES)], axis=-1)
                og = (o * sg_ref[0, rows].astype(F32)).astype(BF16)
                y = _dot(og, w_out_ref[...])
                x2 = x1_ref[0, rows] + g1_ref[0] * y
                o_ref[0, rows] = _rms(x2, fg_ref[...])


def _attn1_call(sink, qt, k, vt, ck, cvt, bias, sg, x1, g1, w_out, fg):
    b, n, d = x1.shape
    nc = ck.shape[2]
    nb = n // BLOCK
    ns = nb // A1_BLOCKS
    assert nb % A1_BLOCKS == 0 and nc == 2 * BLOCK and A1_SLOTS > A1_SKEW >= 2
    tr = A1_BLOCKS * BLOCK
    row = lambda w: pl.BlockSpec((1, tr, w), lambda bi, i: (bi, i, 0))
    prv = lambda i: jnp.maximum(i * A1_BLOCKS - 1, 0)
    nxt = lambda i: jnp.minimum((i + 1) * A1_BLOCKS, nb - 1)
    k_spec = lambda f: pl.BlockSpec((1, GQA_KV_HEADS, BLOCK, LANES), lambda bi, i: (bi, 0, f(i), 0))
    v_spec = lambda f: pl.BlockSpec((1, GQA_KV_HEADS, GQA_VT_ROWS, BLOCK), lambda bi, i: (bi, 0, 0, f(i)))
    vec_spec = pl.BlockSpec((1, 1, d), lambda bi, i: (bi, 0, 0))
    return pl.pallas_call(
        _attn1_kernel,
        grid=(b, ns),
        in_specs=[pl.BlockSpec(memory_space=pltpu.SMEM),
                  pl.BlockSpec((1, A1_BLOCKS, GQA_KV_HEADS, GQA_HEAD_DIM, GQA_GROUP * BLOCK),
                               lambda bi, i: (bi, i, 0, 0, 0)),
                  k_spec(prv), pl.BlockSpec((1, GQA_KV_HEADS, tr, LANES), lambda bi, i: (bi, 0, i, 0)), k_spec(nxt),
                  v_spec(prv), pl.BlockSpec((1, GQA_KV_HEADS, GQA_VT_ROWS, tr), lambda bi, i: (bi, 0, 0, i)),
                  v_spec(nxt),
                  pl.BlockSpec((1, GQA_KV_HEADS, nc, LANES), lambda bi, i: (bi, 0, 0, 0)),
                  pl.BlockSpec((1, GQA_KV_HEADS, GQA_VT_ROWS, nc), lambda bi, i: (bi, 0, 0, 0)),
                  _full(bias.shape), row(GQA_Q), row(d), vec_spec, _full(w_out.shape), _full(fg.shape)],
        out_specs=row(d),
        out_shape=jax.ShapeDtypeStruct((b, n, d), F32),
        scratch_shapes=[pltpu.VMEM((A1_SLOTS * 2 * BLOCK, GQA_GROUP * BLOCK), F32)],
        compiler_params=_cparams(("parallel", "arbitrary")),
        name="attn1",
    )(sink, qt, k, k, k, vt, vt, vt, ck, cvt, bias, sg, x1, g1, w_out, fg)


def _prep_even_weights(w_in, w_qb, w_kvb):
    d = w_in.shape[0]
    f_in, f_gate, q_a, kv_a, k_pe, m_gate = jnp.split(
        w_in, np.cumsum([F_WIDTH, F_WIDTH, MLA_Q_RANK, MLA_KV_RANK, MLA_ROPE]).tolist(), axis=1)
    kpe_blk = jnp.concatenate([jnp.zeros((d, MLA_NOPE), F32), k_pe, jnp.zeros((d, 32), F32)], axis=1)
    w_in_p = jnp.concatenate([f_in, f_gate, m_gate, q_a, kv_a, kpe_blk], axis=1).astype(BF16)
    qb = w_qb.reshape(MLA_Q_RANK, MLA_HEADS, MLA_NOPE + MLA_ROPE)
    w_qbt = jnp.pad(qb, ((0, 0), (0, 0), (0, 32))).reshape(MLA_Q_RANK, MLA_HEADS * LANES).T.astype(BF16)
    kvb = w_kvb.reshape(MLA_KV_RANK, MLA_HEADS, MLA_NOPE + MLA_V)
    w_k = jnp.pad(kvb[:, :, :MLA_NOPE], ((0, 0), (0, 0), (0, 64))).reshape(MLA_KV_RANK, MLA_HEADS * LANES)
    w_vt = jnp.pad(kvb[:, :, MLA_NOPE:], ((0, 0), (0, 0), (0, MLA_VT_ROWS - MLA_V)))
    w_vt = w_vt.reshape(MLA_KV_RANK, MLA_HEADS * MLA_VT_ROWS).T.astype(BF16)
    vone = np.zeros((MLA_HEADS * MLA_VT_ROWS, 1), np.float32)
    vone[MLA_V::MLA_VT_ROWS] = 1.0
    return w_in_p, w_qbt, w_k.astype(BF16), w_vt, jnp.asarray(vone)


def _prep_odd_weights(w_in):
    d = w_in.shape[0]
    w_qt = w_in[:, O_Q:O_K].T.astype(BF16)
    kw = w_in[:, O_K:O_V].reshape(d, GQA_KV_HEADS, GQA_HEAD_DIM)
    w_k = jnp.pad(kw, ((0, 0), (0, 0), (0, LANES - GQA_HEAD_DIM))).reshape(d, GQA_KV_HEADS * LANES).astype(BF16)
    vw = w_in[:, O_V:O_G].reshape(d, GQA_KV_HEADS, GQA_HEAD_DIM)
    w_vt = jnp.pad(vw, ((0, 0), (0, 0), (0, GQA_VT_ROWS - GQA_HEAD_DIM)))
    w_vt = w_vt.reshape(d, GQA_KV_HEADS * GQA_VT_ROWS).T.astype(BF16)
    vone = np.zeros((GQA_KV_HEADS * GQA_VT_ROWS, 1), np.float32)
    vone[GQA_HEAD_DIM::GQA_VT_ROWS] = 1.0
    return w_qt, w_k, w_vt, jnp.asarray(vone), w_in[:, O_G:O_END].astype(BF16)


def _fft_consts(n, c):
    n1 = n // LANES
    c1, s1 = _dft_cs(n1)
    fa = np.block([[c1, s1], [-s1, c1]])
    c2, s2 = _dft_cs(LANES)
    fcs = np.concatenate([c2, s2], axis=1) / math.sqrt(n * F_GROUP_DIM)
    fc = np.concatenate([c2, -s2], axis=1)
    r = np.arange(LANES, dtype=np.int64)[:, None]
    k1 = np.arange(n1, dtype=np.int64)[None, :]
    ang = 2.0 * np.pi * ((r * k1) % n).astype(np.float64) / n
    tw = np.stack([np.cos(ang), np.sin(ang)])
    tw = np.broadcast_to(tw[..., None], (2, LANES, n1, LANES))
    cn, sn = _dft_cs(c)
    fn = np.concatenate([cn, sn], axis=1)
    as32 = lambda a: jnp.asarray(np.ascontiguousarray(a, dtype=np.float32))
    return as32(fa).astype(BF16), as32(fcs).astype(BF16), as32(fc).astype(BF16), as32(tw), as32(fn).astype(BF16)


def _pick_tile(n, pref):
    t = pref
    while n % t:
        t //= 2
    return t


def kernel(x, c, ctx, c_ctx, w_mod, b_mod, norm_g, e_w_in, e_q_norm, e_w_qb, e_kv_norm, e_w_kvb, e_w_out,
           o_w_in, o_sink, o_w_out, final_g):
    b, n, d = x.shape
    nc = ctx.shape[1]
    assert d == D_MODEL and n % (LANES * 8) == 0 and nc % LANES == 0 and b <= 7

    svec = jnp.zeros((8, d), F32).at[:b].set(c).at[b].set(c_ctx)
    mod = _mod_call(svec, w_mod, b_mod)
    sh, sc, gt = mod[:, :, :d], mod[:, :, d:2 * d], mod[:, :, 2 * d:]
    lat = lambda t, l: t[l, :b].reshape(b, 1, d)
    cvec = lambda t, l: t[l, b].reshape(1, d)
    ng0, ng1 = norm_g[0].reshape(1, d), norm_g[1].reshape(1, d)

    cos_m, sin_m = _axial_rope_tables(n, MLA_ROPE)
    qs = MLA_SCALE * LOG2E
    tqt0 = _const(np.stack([cos_m.T, sin_m.T]) * qs)
    tk0 = _mla_rope_tab(cos_m, sin_m)
    one_c, zero_c = np.ones((nc, MLA_ROPE // 2)), np.zeros((nc, MLA_ROPE // 2))
    tqt0c = _const(np.stack([one_c.T, zero_c.T]) * qs)
    tk0c = _mla_rope_tab(one_c, zero_c)
    cos_g, sin_g = _axial_rope_tables(n, GQA_HEAD_DIM)
    tqt1 = _const(np.stack([cos_g.T, sin_g.T]) * (GQA_SCALE * LOG2E))
    tk1 = _gqa_rope_tab(cos_g, sin_g)
    fa, fcs, fc, tw, fn = _fft_consts(n, nc)
    bias1 = _band_bias(nc)

    w_in0, w_qbt0, w_k0, w_vt0, vone = _prep_even_weights(e_w_in[0], e_w_qb[0], e_w_kvb[0])
    qn0 = e_q_norm[0].reshape(1, MLA_Q_RANK)
    kvn0 = e_kv_norm[0].reshape(1, MLA_KV_RANK)
    w_out0 = e_w_out[0].astype(BF16)
    w_qt1, w_k1, w_vt1, vone1, w_g1 = _prep_odd_weights(o_w_in[0])
    w_out1 = o_w_out[0].astype(BF16)

    kc, vct, ck1, cvt1 = _ctx_call(
        ctx, (ng0, cvec(sc, 0), cvec(sh, 0), cvec(gt, 0)), (ng1, cvec(sc, 1), cvec(sh, 1)),
        w_in0, qn0, w_qbt0, kvn0, w_k0, w_vt0, vone, tqt0c, tk0c, fc, fn, w_out0, w_k1, w_vt1, vone1)

    tm = _pick_tile(n, 512)
    qt0, k0, vt0, zr, zi, gf, gm = _front0_call(
        x, ng0, lat(sc, 0), lat(sh, 0), w_in0, qn0, w_qbt0, kvn0, w_k0, w_vt0, vone, tqt0, tk0, fc, tm)

    ag = _attn0_call(qt0, kc, vct, k0, vt0, gm, _pick_tile(n, 512))

    n1 = n // LANES
    tr, ti = _fft_a_call(zr.reshape(b, n1, LANES * F_WIDTH), zi.reshape(b, n1, LANES * F_WIDTH), fa, tw, 8)
    fmg = _fft_c_call(tr.reshape(b, n1, LANES, F_WIDTH), ti.reshape(b, n1, LANES, F_WIDTH), fcs,
                      gf.reshape(b, LANES, n1 * F_WIDTH), _pick_tile(n1, 8))
    fmg = fmg.reshape(b, n, F_WIDTH)

    tm1 = _pick_tile(n, 256)
    x1, qt1, k1, vt1, sg = _mid_call(fmg, ag, x, lat(gt, 0), w_out0, ng1, lat(sc, 1), lat(sh, 1),
                                     w_qt1, w_k1, w_vt1, vone1, w_g1, tqt1, tk1, tm1)

    return _attn1_call(o_sink[0], qt1, k1, vt1, ck1, cvt1, bias1, sg, x1, lat(gt, 1), w_out1,
                       final_g.reshape(1, d))
```

```python
import functools
import math

import numpy as np
import jax
import jax.numpy as jnp
from jax import lax
from jax.experimental import pallas as pl
from jax.experimental.pallas import tpu as pltpu

F32 = jnp.float32
BF16 = jnp.bfloat16

D_MODEL = 1024
GRID_W = 64
EPS = 1e-6
ROPE_BASE = 10000.0
LANES = 128

F_GROUPS = 4
F_GROUP_DIM = 128
F_WIDTH = F_GROUPS * F_GROUP_DIM

MLA_HEADS = 8
MLA_NOPE = 64
MLA_ROPE = 32
MLA_V = 64
MLA_Q_RANK = 384
MLA_KV_RANK = 256
MLA_WIDTH = MLA_HEADS * MLA_V
MLA_SCALE = 1.0 / math.sqrt(MLA_NOPE + MLA_ROPE)
LOG2E = math.log2(math.e)
MLA_VT_ROWS = 80
PV_KEYS = 256
MLA_TQ = 256
SOFTMAX_GROUP = 2
PIPE_SKEW = 3
SCORE_SLOTS = 4
GATE_LAG = 2

GQA_HEADS = 16
GQA_KV_HEADS = 4
GQA_GROUP = GQA_HEADS // GQA_KV_HEADS
GQA_HEAD_DIM = 64
WINDOW = 128
BLOCK = 128
GQA_Q = GQA_HEADS * GQA_HEAD_DIM
GQA_KV = GQA_KV_HEADS * GQA_HEAD_DIM
GQA_SCALE = 1.0 / math.sqrt(GQA_HEAD_DIM)
GQA_VT_ROWS = 80
A1_BLOCKS = 4
A1_SKEW = 5
A1_SLOTS = 6
A1_GATE_LAG = 3

E_FIN, E_FGATE, E_MGATE, E_QA, E_KVA, E_KPE, E_END = 0, 512, 1024, 1536, 1920, 2176, 2304
O_Q, O_K, O_V, O_G, O_END = 0, 1024, 1280, 1536, 2560

NEG_BIG = -1e30
VMEM_LIMIT = 56 * 1024 * 1024


def _cparams(sem):
    return pltpu.CompilerParams(dimension_semantics=sem, vmem_limit_bytes=VMEM_LIMIT)


def _dot(a, b):
    return jnp.dot(a, b, preferred_element_type=F32)


def _dot_nt(a, b):
    return lax.dot_general(a, b, (((1,), (1,)), ((), ())), preferred_element_type=F32)


def _rms(x, g):
    return x * lax.rsqrt(jnp.mean(x * x, axis=-1, keepdims=True) + EPS) * g


def _silu(x):
    return x * jax.nn.sigmoid(x)


def _rope(x, tab_ref, shift):
    up = pltpu.roll(x, LANES - shift, 1)
    dn = pltpu.roll(x, shift, 1)
    return x * tab_ref[0] + up * tab_ref[1] + dn * tab_ref[2]


def _full(shape):
    nd = len(shape)
    return pl.BlockSpec(shape, lambda *_: (0,) * nd)


def _axial_rope_tables(n, rot_dim):
    rows = n // GRID_W
    row = np.repeat(np.arange(rows), GRID_W).astype(np.float64)
    col = np.tile(np.arange(GRID_W), rows).astype(np.float64)
    nf = rot_dim // 4
    inv = (np.float32(ROPE_BASE) ** (-np.arange(nf, dtype=np.float32) / np.float32(nf))).astype(np.float64)
    ang = np.concatenate([row[:, None] * inv, col[:, None] * inv], axis=-1)
    return np.cos(ang), np.sin(ang)


def _mla_rope_tab(cos, sin):
    n = cos.shape[0]
    z = lambda w: np.zeros((n, w))
    cf = np.concatenate([np.ones((n, MLA_NOPE)), cos, cos, z(32)], axis=-1)
    s1 = np.concatenate([z(MLA_NOPE), -sin, z(16), z(32)], axis=-1)
    s2 = np.concatenate([z(MLA_NOPE), z(16), sin, z(32)], axis=-1)
    return _const(np.stack([cf, s1, s2]))


def _gqa_rope_tab(cos, sin):
    z = np.zeros_like(sin)
    cf = np.concatenate([cos, cos, z, z], axis=-1)
    s1 = np.concatenate([-sin, z, z, z], axis=-1)
    s2 = np.concatenate([z, sin, z, z], axis=-1)
    return _const(np.stack([cf, s1, s2]))


def _const(a):
    return jnp.asarray(np.ascontiguousarray(a, dtype=np.float32))


def _dft_cs(n):
    idx = np.arange(n, dtype=np.int64)
    ang = 2.0 * np.pi * ((idx[:, None] * idx[None, :]) % n).astype(np.float64) / n
    return np.cos(ang), np.sin(ang)


def _mod_kernel(s_ref, w_ref, b_ref, o_ref):
    s = _silu(s_ref[...])
    o_ref[0] = _dot(s.astype(BF16), w_ref[0].astype(BF16)) + b_ref[0]


def _mod_call(svec, w_mod, b_mod):
    depth, d, d3 = w_mod.shape
    tn = 768
    return pl.pallas_call(
        _mod_kernel,
        grid=(depth, d3 // tn),
        in_specs=[
            pl.BlockSpec((8, d), lambda l, j: (0, 0)),
            pl.BlockSpec((1, d, tn), lambda l, j: (l, 0, j)),
            pl.BlockSpec((1, 1, tn), lambda l, j: (l, 0, j)),
        ],
        out_specs=pl.BlockSpec((1, 8, tn), lambda l, j: (l, 0, j)),
        out_shape=jax.ShapeDtypeStruct((depth, 8, d3), F32),
        compiler_params=_cparams(("arbitrary", "arbitrary")),
        name="mod",
    )(svec, w_mod, b_mod.reshape(depth, 1, d3))


def _even_front(x, ng, sc, sh, w_in_ref, qn, w_qbt_ref, kvn, w_k_ref, w_vt_ref, vone, tqt_ref, tk_ref, fc_ref):
    m = x.shape[0]
    h = (_rms(x, ng) * (1.0 + sc) + sh).astype(BF16)
    f_in = _dot(h, w_in_ref[:, E_FIN:E_FGATE])
    f_gate = _dot(h, w_in_ref[:, E_FGATE:E_MGATE])
    m_gate = _dot(h, w_in_ref[:, E_MGATE:E_QA])
    low_rank = _dot(h, w_in_ref[:, E_QA:E_END])
    q_a = low_rank[:, 0:E_KVA - E_QA]
    kv_a = low_rank[:, E_KVA - E_QA:E_KPE - E_QA]
    kpe = low_rank[:, E_KPE - E_QA:E_END - E_QA]
    qh = _rms(q_a, qn).astype(BF16)
    ch = _rms(kv_a, kvn).astype(BF16)
    kpe_r = _rope(kpe, tk_ref, 16)
    qt = _dot_nt(w_qbt_ref[...], qh)
    k_all = _dot(ch, w_k_ref[...])
    cos = tqt_ref[0]
    sin = tqt_ref[1]
    pad = jnp.zeros((LANES - MLA_NOPE - MLA_ROPE, m), F32)
    qts, ks = [], []
    for hd in range(MLA_HEADS):
        lo = hd * LANES
        ql = hd * (MLA_NOPE + MLA_ROPE)
        x1 = qt[ql + MLA_NOPE:ql + MLA_NOPE + 16]
        x2 = qt[ql + MLA_NOPE + 16:ql + MLA_NOPE + 32]
        qts.append(jnp.concatenate(
            [qt[ql:ql + MLA_NOPE] * (MLA_SCALE * LOG2E), x1 * cos - x2 * sin, x2 * cos + x1 * sin, pad], axis=0))
        ks.append(k_all[:, lo:lo + LANES] + kpe_r)
    vt = _dot_nt(w_vt_ref[...], ch) + vone
    zr, zi = [], []
    fb = f_in.astype(BF16)
    for g in range(F_GROUPS):
        z = _dot(fb[:, g * LANES:(g + 1) * LANES], fc_ref[...])
        zr.append(z[:, :LANES])
        zi.append(z[:, LANES:])
    return qts, ks, vt, zr, zi, f_gate, m_gate


def _front0_kernel(x_ref, ng_ref, sc_ref, sh_ref, w_in_ref, qn_ref, w_qbt_ref, kvn_ref, w_k_ref, w_vt_ref,
                   vone_ref, tqt_ref, tk_ref, fc_ref, q_ref, k_ref, v_ref, zr_ref, zi_ref, gf_ref, gm_ref):
    qts, ks, vt, zr, zi, f_gate, m_gate = _even_front(
        x_ref[0], ng_ref[...], sc_ref[0], sh_ref[0], w_in_ref, qn_ref[...], w_qbt_ref, kvn_ref[...],
        w_k_ref, w_vt_ref, vone_ref[...], tqt_ref, tk_ref, fc_ref)
    for hd in range(MLA_HEADS):
        q_ref[0, hd] = qts[hd].astype(BF16)
        k_ref[0, hd] = ks[hd].astype(BF16)
        v_ref[0, hd, 0] = vt[hd * MLA_VT_ROWS:(hd + 1) * MLA_VT_ROWS].astype(BF16)
    for g in range(F_GROUPS):
        zr_ref[0, :, g * LANES:(g + 1) * LANES] = zr[g].astype(BF16)
        zi_ref[0, :, g * LANES:(g + 1) * LANES] = zi[g].astype(BF16)
    gf_ref[0] = _silu(f_gate).astype(BF16)
    gm_ref[0] = _silu(m_gate).astype(BF16)


def _front0_call(x, ng, sc, sh, w_in, qn, w_qbt, kvn, w_k, w_vt, vone, tqt, tk, fc, tm):
    b, n, d = x.shape
    fw = jax.ShapeDtypeStruct((b, n, F_WIDTH), BF16)
    row_spec = pl.BlockSpec((1, tm, F_WIDTH), lambda bi, i: (bi, i, 0))
    vec_spec = pl.BlockSpec((1, 1, d), lambda bi, i: (bi, 0, 0))
    return pl.pallas_call(
        _front0_kernel,
        grid=(b, n // tm),
        in_specs=[
            pl.BlockSpec((1, tm, d), lambda bi, i: (bi, i, 0)),
            _full(ng.shape), vec_spec, vec_spec,
            _full(w_in.shape), _full(qn.shape), _full(w_qbt.shape), _full(kvn.shape), _full(w_k.shape),
            _full(w_vt.shape), _full(vone.shape),
            pl.BlockSpec((2, 16, tm), lambda bi, i: (0, 0, i)),
            pl.BlockSpec((3, tm, LANES), lambda bi, i: (0, i, 0)),
            _full(fc.shape),
        ],
        out_specs=[
            pl.BlockSpec((1, MLA_HEADS, LANES, tm), lambda bi, i: (bi, 0, 0, i)),
            pl.BlockSpec((1, MLA_HEADS, tm, LANES), lambda bi, i: (bi, 0, i, 0)),
            pl.BlockSpec((1, MLA_HEADS, 1, MLA_VT_ROWS, tm), lambda bi, i: (bi, 0, i, 0, 0)),
            row_spec, row_spec, row_spec, row_spec],
        out_shape=[
            jax.ShapeDtypeStruct((b, MLA_HEADS, LANES, n), BF16),
            jax.ShapeDtypeStruct((b, MLA_HEADS, n, LANES), BF16),
            jax.ShapeDtypeStruct((b, MLA_HEADS, n // tm, MLA_VT_ROWS, tm), BF16),
            fw, fw, fw, fw],
        compiler_params=_cparams(("parallel", "arbitrary")),
        name="front0",
    )(x, ng, sc, sh, w_in, qn, w_qbt, kvn, w_k, w_vt, vone, tqt, tk, fc)


def _zero_after(x):
    u = lax.bitcast_convert_type(x, jnp.uint32)
    z = lax.shift_right_logical(lax.shift_right_logical(u, jnp.uint32(16)), jnp.uint32(16))
    return lax.bitcast_convert_type(z, F32)


def _mla_chunk(qt, k, vt, m, acc):
    s = _dot(k, qt)
    m_new = jnp.maximum(m, jnp.max(s, axis=0, keepdims=True))
    alpha = jnp.exp2(m - m_new)
    p = jnp.exp2(s - m_new).astype(BF16)
    return m_new, alpha * acc + _dot(vt, p)


def _mla_pair_out(acc0, acc1):
    o0 = acc0[0:MLA_V] / acc0[MLA_V:MLA_V + 1]
    o1 = acc1[0:MLA_V] / acc1[MLA_V:MLA_V + 1]
    return jnp.concatenate([o0, o1], axis=0).T


def _ctx_kernel(ctx_scale, x_ref, ng0_ref, sc0_ref, sh0_ref, g0_ref, ng1_ref, sc1_ref, sh1_ref,
                w_in_ref, qn_ref, w_qbt_ref, kvn_ref, w_k_ref, w_vt_ref, vone_ref, tqt_ref, tk_ref, fc_ref,
                fn_ref, w_out_ref, w_k1_ref, w_vt1_ref, vone1_ref,
                kc_ref, vc_ref, ck1_ref, cv1_ref):
    x = x_ref[0]
    c = x.shape[0]
    qts, ks, vt, zr, zi, f_gate, m_gate = _even_front(
        x, ng0_ref[...], sc0_ref[...], sh0_ref[...], w_in_ref, qn_ref[...], w_qbt_ref, kvn_ref[...],
        w_k_ref, w_vt_ref, vone_ref[...], tqt_ref, tk_ref, fc_ref)
    pairs = []
    for hp in range(MLA_HEADS // 2):
        accs = []
        for e in range(2):
            hd = 2 * hp + e
            kb = ks[hd].astype(BF16)
            vtb = vt[hd * MLA_VT_ROWS:(hd + 1) * MLA_VT_ROWS].astype(BF16)
            kc_ref[0, hd] = kb
            vc_ref[0, hd, 0] = vtb
            m0 = jnp.full((1, c), -jnp.inf, F32)
            acc0 = jnp.zeros((MLA_VT_ROWS, c), F32)
            accs.append(_mla_chunk(qts[hd].astype(BF16), kb, vtb, m0, acc0)[1])
        pairs.append(_mla_pair_out(accs[0], accs[1]))
    a = jnp.concatenate(pairs, axis=-1)
    z = jnp.concatenate([jnp.concatenate(zr, axis=-1), jnp.concatenate(zi, axis=-1)], axis=0).astype(BF16)
    fm = _dot(fn_ref[...], z) * ctx_scale
    yf = (fm * _silu(f_gate)).astype(BF16)
    ya = (a * _silu(m_gate)).astype(BF16)
    y = _dot(yf, w_out_ref[0:F_WIDTH, :]) + _dot(ya, w_out_ref[F_WIDTH:, :])
    x1 = x + g0_ref[...] * y
    h1 = (_rms(x1, ng1_ref[...]) * (1.0 + sc1_ref[...]) + sh1_ref[...]).astype(BF16)
    vt1 = _dot_nt(w_vt1_ref[...], h1) + vone1_ref[...]
    for g in range(GQA_KV_HEADS):
        ck1_ref[0, g] = _dot(h1, w_k1_ref[:, g * LANES:(g + 1) * LANES]).astype(BF16)
        cv1_ref[0, g] = vt1[g * GQA_VT_ROWS:(g + 1) * GQA_VT_ROWS].astype(BF16)


def _ctx_call(ctx, vecs0, vecs1, w_in, qn, w_qbt, kvn, w_k, w_vt, vone, tqt, tk, fc, fn, w_out, w_k1, w_vt1,
              vone1):
    b, c, d = ctx.shape
    ng0, sc0, sh0, g0 = vecs0
    ng1, sc1, sh1 = vecs1
    ctx_scale = 1.0 / math.sqrt(c * F_GROUP_DIM)
    consts = [ng0, sc0, sh0, g0, ng1, sc1, sh1, w_in, qn, w_qbt, kvn, w_k, w_vt, vone, tqt, tk, fc, fn, w_out,
              w_k1, w_vt1, vone1]
    return pl.pallas_call(
        functools.partial(_ctx_kernel, ctx_scale),
        grid=(b,),
        in_specs=[pl.BlockSpec((1, c, d), lambda bi: (bi, 0, 0))] + [_full(a.shape) for a in consts],
        out_specs=[pl.BlockSpec((1, MLA_HEADS, c, LANES), lambda bi: (bi, 0, 0, 0)),
                   pl.BlockSpec((1, MLA_HEADS, 1, MLA_VT_ROWS, c), lambda bi: (bi, 0, 0, 0, 0)),
                   pl.BlockSpec((1, GQA_KV_HEADS, c, LANES), lambda bi: (bi, 0, 0, 0)),
                   pl.BlockSpec((1, GQA_KV_HEADS, GQA_VT_ROWS, c), lambda bi: (bi, 0, 0, 0))],
        out_shape=[jax.ShapeDtypeStruct((b, MLA_HEADS, c, LANES), BF16),
                   jax.ShapeDtypeStruct((b, MLA_HEADS, 1, MLA_VT_ROWS, c), BF16),
                   jax.ShapeDtypeStruct((b, GQA_KV_HEADS, c, LANES), BF16),
                   jax.ShapeDtypeStruct((b, GQA_KV_HEADS, GQA_VT_ROWS, c), BF16)],
        compiler_params=_cparams(("arbitrary",)),
        name="ctx",
    )(ctx, *consts)


def _attn0_kernel(qt_ref, kc_ref, vct_ref, kl_ref, vlt_ref, gm_ref, o_ref, s_ref):
    tq = qt_ref.shape[-1]
    nch, _, tk = vlt_ref.shape[2:]
    nc = kc_ref.shape[2]
    n_ctx = nc // PV_KEYS
    n_sub = n_ctx + nch * tk // PV_KEYS
    group = [0] * n_ctx + [1 + j // SOFTMAX_GROUP for j in range(n_sub - n_ctx)]
    first = [t for t in range(n_sub) if t == 0 or group[t] != group[t - 1]]
    last = [t for t in range(n_sub) if t == n_sub - 1 or group[t] != group[t + 1]]

    def keys(e, t):
        if t < n_ctx:
            return kc_ref[0, e, t * PV_KEYS:(t + 1) * PV_KEYS, :]
        lo = (t - n_ctx) * PV_KEYS
        return kl_ref[0, e, lo:lo + PV_KEYS, :]

    def values_t(e, t):
        if t < n_ctx:
            return vct_ref[0, e, 0, :, t * PV_KEYS:(t + 1) * PV_KEYS]
        j, off = divmod((t - n_ctx) * PV_KEYS, tk)
        return vlt_ref[0, e, j, :, off:off + PV_KEYS]

    def slot(i):
        r = i % SCORE_SLOTS
        return slice(r * PV_KEYS, (r + 1) * PV_KEYS)

    tasks = [(qi, t) for qi in range(tq // MLA_TQ) for t in range(n_sub)]
    colmax = []
    for step in range(len(tasks) + PIPE_SKEW):
        if step < len(tasks):
            qi, t = tasks[step]
            cols = slice(qi * MLA_TQ, (qi + 1) * MLA_TQ)
            cm = []
            for e in range(2):
                s = _dot(keys(e, t), qt_ref[0, e, :, cols])
                s_ref[e, slot(step)] = s
                cm.append(jnp.max(s, axis=0, keepdims=True))
            colmax.append(cm)
        if step < PIPE_SKEW:
            continue
        qi, u = tasks[step - PIPE_SKEW]
        if u == 0:
            m = [jnp.full((1, MLA_TQ), -jnp.inf, F32) for _ in range(2)]
            acc = [jnp.zeros((MLA_VT_ROWS, MLA_TQ), F32) for _ in range(2)]
        if u in first:
            members = [qi * n_sub + t for t in range(n_sub) if group[t] == group[u]]
            assert members[-1] <= step
            m_new, alpha = [], []
            for e in range(2):
                mc = functools.reduce(jnp.maximum, [colmax[i][e] for i in members])
                m_new.append(jnp.maximum(m[e], mc))
                alpha.append(jnp.exp2(m[e] - m_new[-1]))
            pv = [None, None]
        gate = min(step - GATE_LAG, len(tasks) - 1)
        for e in range(2):
            m_use = m_new[e] + _zero_after(colmax[gate][e])
            p = jnp.exp2(s_ref[e, slot(step - PIPE_SKEW)] - m_use).astype(BF16)
            d = _dot(values_t(e, u), p)
            pv[e] = d if pv[e] is None else pv[e] + d
        if u in last:
            for e in range(2):
                acc[e] = alpha[e] * acc[e] + pv[e]
                m[e] = m_new[e]
        if u == n_sub - 1:
            rows = slice(qi * MLA_TQ, (qi + 1) * MLA_TQ)
            o_ref[0, rows] = (_mla_pair_out(acc[0], acc[1]) * gm_ref[0, rows].astype(F32)).astype(BF16)


def _attn0_call(qt, kc, vct, kl, vlt, gm, tq):
    b, h, _, n = qt.shape
    c = kc.shape[2]
    nch, _, tk = vlt.shape[2:]
    assert c % PV_KEYS == 0 and tk % PV_KEYS == 0 and SCORE_SLOTS > PIPE_SKEW >= SOFTMAX_GROUP - 1
    assert tq % MLA_TQ == 0
    o_spec = pl.BlockSpec((1, tq, LANES), lambda bi, hp, i: (bi, i, hp))
    return pl.pallas_call(
        _attn0_kernel,
        grid=(b, h // 2, n // tq),
        in_specs=[
            pl.BlockSpec((1, 2, LANES, tq), lambda bi, hp, i: (bi, hp, 0, i)),
            pl.BlockSpec((1, 2, c, LANES), lambda bi, hp, i: (bi, hp, 0, 0)),
            pl.BlockSpec((1, 2, 1, MLA_VT_ROWS, c), lambda bi, hp, i: (bi, hp, 0, 0, 0)),
            pl.BlockSpec((1, 2, n, LANES), lambda bi, hp, i: (bi, hp, 0, 0)),
            pl.BlockSpec((1, 2, nch, MLA_VT_ROWS, tk), lambda bi, hp, i: (bi, hp, 0, 0, 0)),
            o_spec],
        out_specs=o_spec,
        out_shape=jax.ShapeDtypeStruct((b, n, MLA_WIDTH), BF16),
        scratch_shapes=[pltpu.VMEM((2, SCORE_SLOTS * PV_KEYS, MLA_TQ), F32)],
        compiler_params=_cparams(("parallel", "parallel", "arbitrary")),
        name="attn0",
    )(qt, kc, vct, kl, vlt, gm)


def _fft_a_kernel(t2, n1, zr_ref, zi_ref, fa_ref, tw_ref, tr_ref, ti_ref):
    z = jnp.concatenate([zr_ref[0], zi_ref[0]], axis=0)
    a = _dot(fa_ref[...], z)
    for j in range(t2):
        c = tw_ref[0, j]
        s = tw_ref[1, j]
        for g in range(F_GROUPS):
            lo = j * F_WIDTH + g * LANES
            ar = a[0:n1, lo:lo + LANES]
            ai = a[n1:2 * n1, lo:lo + LANES]
            tr_ref[0, :, lo:lo + LANES] = (ar * c + ai * s).astype(BF16)
            ti_ref[0, :, lo:lo + LANES] = (ai * c - ar * s).astype(BF16)


def _fft_a_call(zr, zi, fa, tw, t2):
    b, n1, w = zr.shape
    spec = pl.BlockSpec((1, n1, t2 * F_WIDTH), lambda bi, j: (bi, 0, j))
    out = jax.ShapeDtypeStruct((b, n1, w), BF16)
    return pl.pallas_call(
        functools.partial(_fft_a_kernel, t2, n1),
        grid=(b, w // (t2 * F_WIDTH)),
        in_specs=[spec, spec, _full(fa.shape), pl.BlockSpec((2, t2, n1, LANES), lambda bi, j: (0, j, 0, 0))],
        out_specs=[spec, spec],
        out_shape=[out, out],
        compiler_params=_cparams(("parallel", "arbitrary")),
        name="fft_a",
    )(zr, zi, fa, tw)


def _fft_c_kernel(t1, tr_ref, ti_ref, fcs_ref, gf_ref, o_ref):
    for k in range(t1):
        rhs = jnp.concatenate([tr_ref[0, k], ti_ref[0, k]], axis=0)
        lo = k * F_WIDTH
        out = _dot(fcs_ref[...], rhs)
        o_ref[0, :, lo:lo + F_WIDTH] = (out * gf_ref[0, :, lo:lo + F_WIDTH].astype(F32)).astype(BF16)


def _fft_c_call(tr, ti, fcs, gf, t1):
    b, n1, r, w = tr.shape
    t_spec = pl.BlockSpec((1, t1, r, w), lambda bi, i: (bi, i, 0, 0))
    g_spec = pl.BlockSpec((1, r, t1 * w), lambda bi, i: (bi, 0, i))
    return pl.pallas_call(
        functools.partial(_fft_c_kernel, t1),
        grid=(b, n1 // t1),
        in_specs=[t_spec, t_spec, _full(fcs.shape), g_spec],
        out_specs=g_spec,
        out_shape=jax.ShapeDtypeStruct((b, r, n1 * w), BF16),
        compiler_params=_cparams(("parallel", "arbitrary")),
        name="fft_c",
    )(tr, ti, fcs, gf)


def _mid_kernel(fm_ref, ag_ref, x_ref, g0_ref, w_out_ref, ng_ref, sc_ref, sh_ref, w_qt_ref, w_k_ref, w_vt_ref,
                vone_ref, w_g_ref, tqt_ref, tk_ref, x1_ref, q_ref, k_ref, v_ref, sg_ref):
    tm = x_ref.shape[1]
    y = _dot(fm_ref[0], w_out_ref[0:F_WIDTH, :]) + _dot(ag_ref[0], w_out_ref[F_WIDTH:, :])
    x1 = x_ref[0] + g0_ref[0] * y
    x1_ref[0] = x1
    h = (_rms(x1, ng_ref[...]) * (1.0 + sc_ref[0]) + sh_ref[0]).astype(BF16)
    qt = _dot_nt(w_qt_ref[...], h)
    cos = tqt_ref[0]
    sin = tqt_ref[1]
    half = GQA_HEAD_DIM // 2
    for hd in range(GQA_HEADS):
        lo = hd * GQA_HEAD_DIM
        x1r = qt[lo:lo + half]
        x2r = qt[lo + half:lo + GQA_HEAD_DIM]
        qh = jnp.concatenate([x1r * cos - x2r * sin, x2r * cos + x1r * sin], axis=0).astype(BF16)
        g, j = divmod(hd, GQA_GROUP)
        for blk in range(tm // BLOCK):
            q_ref[0, blk, g, :, j * BLOCK:(j + 1) * BLOCK] = qh[:, blk * BLOCK:(blk + 1) * BLOCK]
    vt = _dot_nt(w_vt_ref[...], h) + vone_ref[...]
    k_all = _dot(h, w_k_ref[...])
    for g in range(GQA_KV_HEADS):
        k_ref[0, g] = _rope(k_all[:, g * LANES:(g + 1) * LANES], tk_ref, 32).astype(BF16)
        v_ref[0, g] = vt[g * GQA_VT_ROWS:(g + 1) * GQA_VT_ROWS].astype(BF16)
    sg_ref[0] = _silu(_dot(h, w_g_ref[...])).astype(BF16)


def _mid_call(fm, ag, x, g0, w_out, ng, sc, sh, w_qt, w_k, w_vt, vone, w_g, tqt, tk, tm):
    b, n, d = x.shape
    row = lambda w: pl.BlockSpec((1, tm, w), lambda bi, i: (bi, i, 0))
    vec_spec = pl.BlockSpec((1, 1, d), lambda bi, i: (bi, 0, 0))
    nblk = tm // BLOCK
    return pl.pallas_call(
        _mid_kernel,
        grid=(b, n // tm),
        in_specs=[row(F_WIDTH), row(MLA_WIDTH), row(d), vec_spec, _full(w_out.shape), _full(ng.shape),
                  vec_spec, vec_spec, _full(w_qt.shape), _full(w_k.shape), _full(w_vt.shape), _full(vone.shape),
                  _full(w_g.shape),
                  pl.BlockSpec((2, GQA_HEAD_DIM // 2, tm), lambda bi, i: (0, 0, i)),
                  pl.BlockSpec((3, tm, LANES), lambda bi, i: (0, i, 0))],
        out_specs=[row(d),
                   pl.BlockSpec((1, nblk, GQA_KV_HEADS, GQA_HEAD_DIM, GQA_GROUP * BLOCK),
                                lambda bi, i: (bi, i, 0, 0, 0)),
                   pl.BlockSpec((1, GQA_KV_HEADS, tm, LANES), lambda bi, i: (bi, 0, i, 0)),
                   pl.BlockSpec((1, GQA_KV_HEADS, GQA_VT_ROWS, tm), lambda bi, i: (bi, 0, 0, i)),
                   row(GQA_Q)],
        out_shape=[jax.ShapeDtypeStruct((b, n, d), F32),
                   jax.ShapeDtypeStruct((b, n // BLOCK, GQA_KV_HEADS, GQA_HEAD_DIM, GQA_GROUP * BLOCK), BF16),
                   jax.ShapeDtypeStruct((b, GQA_KV_HEADS, n, LANES), BF16),
                   jax.ShapeDtypeStruct((b, GQA_KV_HEADS, GQA_VT_ROWS, n), BF16),
                   jax.ShapeDtypeStruct((b, n, GQA_Q), BF16)],
        compiler_params=_cparams(("parallel", "arbitrary")),
        name="mid",
    )(fm, ag, x, g0, w_out, ng, sc, sh, w_qt, w_k, w_vt, vone, w_g, tqt, tk)


def _band_bias(nc):
    col = np.arange(3 * BLOCK)[:, None]
    q = np.arange(BLOCK)[None, :]
    dist = BLOCK + q - col
    band = np.abs(dist) <= WINDOW
    variants = [band & (col >= BLOCK), band, band & (col < 2 * BLOCK)]
    out = np.zeros((3, nc + 3 * BLOCK, BLOCK), np.float32)
    for v, ok in enumerate(variants):
        out[v, nc:] = np.where(ok, 0.0, NEG_BIG)
    return jnp.asarray(out)


def _attn1_kernel(sink_ref, q_ref, kp_ref, kc_ref, kn_ref, vp_ref, vc_ref, vn_ref, ck_ref, cv_ref, bias_ref,
                  sg_ref, x1_ref, g1_ref, w_out_ref, fg_ref, o_ref, s_ref):
    i = pl.program_id(1)
    last = pl.num_programs(1) - 1
    nc = ck_ref.shape[2]
    sel = []
    for blk in range(A1_BLOCKS):
        v = 1
        if blk == 0:
            v = jnp.where(i == 0, 0, v)
        if blk == A1_BLOCKS - 1:
            v = jnp.where(i == last, 2, v)
        sel.append(v)

    def band_keys(g, blk, j):
        pos = blk + j - 1
        if pos < 0:
            return kp_ref[0, g], vp_ref[0, g]
        if pos >= A1_BLOCKS:
            return kn_ref[0, g], vn_ref[0, g]
        return (kc_ref[0, g, pos * BLOCK:(pos + 1) * BLOCK, :], vc_ref[0, g, :, pos * BLOCK:(pos + 1) * BLOCK])

    def operands(blk, g, sb):
        if sb == 0:
            return ck_ref[0, g][:, :GQA_HEAD_DIM], cv_ref[0, g], None
        if sb == 1:
            (k0, v0), (k1, v1) = band_keys(g, blk, 0), band_keys(g, blk, 1)
            return (jnp.concatenate([k0, k1], axis=0)[:, :GQA_HEAD_DIM], jnp.concatenate([v0, v1], axis=1),
                    bias_ref[sel[blk], nc:nc + 2 * BLOCK])
        k2, v2 = band_keys(g, blk, 2)
        return k2[:, :GQA_HEAD_DIM], v2, bias_ref[sel[blk], nc + 2 * BLOCK:nc + 3 * BLOCK]

    def slot(idx, rows):
        r = idx % A1_SLOTS
        return slice(r * 2 * BLOCK, r * 2 * BLOCK + rows)

    tasks = [(blk, g, sb) for blk in range(A1_BLOCKS) for g in range(GQA_KV_HEADS) for sb in range(3)]
    colmax = []
    chunks = {}
    for step in range(len(tasks) + A1_SKEW):
        if step < len(tasks):
            blk, g, sb = tasks[step]
            kb, _, bias = operands(blk, g, sb)
            s = _dot(kb, q_ref[0, blk, g])
            if bias is not None:
                s = s + jnp.concatenate([bias] * GQA_GROUP, axis=1)
            s_ref[slot(step, s.shape[0])] = s
            colmax.append(jnp.max(s, axis=0, keepdims=True))
        if step < A1_SKEW:
            continue
        idx = step - A1_SKEW
        blk, g, sb = tasks[idx]
        _, vt, _ = operands(blk, g, sb)
        if sb == 0:
            assert idx + 2 <= step
            sink = jnp.concatenate(
                [jnp.full((1, BLOCK), sink_ref[GQA_GROUP * g + j] * LOG2E, F32) for j in range(GQA_GROUP)], axis=1)
            m = functools.reduce(jnp.maximum, [colmax[idx], colmax[idx + 1], colmax[idx + 2], sink])
            acc = None
        gate = min(step - A1_GATE_LAG, len(tasks) - 1)
        m_use = m + _zero_after(colmax[gate])
        p = jnp.exp2(s_ref[slot(idx, vt.shape[1])] - m_use).astype(BF16)
        d = _dot(vt, p)
        acc = d if acc is None else acc + d
        if sb == 2:
            l = acc[GQA_HEAD_DIM:GQA_HEAD_DIM + 1] + jnp.exp2(sink - m)
            ot = acc[0:GQA_HEAD_DIM] / l
            for p2 in range(2):
                lo = 2 * p2 * BLOCK
                pair = jnp.concatenate([ot[:, lo:lo + BLOCK], ot[:, lo + BLOCK:lo + 2 * BLOCK]], axis=0)
                chunks[(blk, 2 * g + p2)] = pair.T
            if g == GQA_KV_HEADS - 1:
                rows = slice(blk * BLOCK, (blk + 1) * BLOCK)
                o = jnp.concatenate([chunks[(blk, c)] for c in range(GQA_Q // LANES)], axis=-1)
                og = (o * sg_ref[0, rows].astype(F32)).astype(BF16)
                y = _dot(og, w_out_ref[...])
                x2 = x1_ref[0, rows] + g1_ref[0] * y
                o_ref[0, rows] = _rms(x2, fg_ref[...])


def _attn1_call(sink, qt, k, vt, ck, cvt, bias, sg, x1, g1, w_out, fg):
    b, n, d = x1.shape
    nc = ck.shape[2]
    nb = n // BLOCK
    ns = nb // A1_BLOCKS
    assert nb % A1_BLOCKS == 0 and nc == 2 * BLOCK and A1_SLOTS > A1_SKEW >= 2
    tr = A1_BLOCKS * BLOCK
    row = lambda w: pl.BlockSpec((1, tr, w), lambda bi, i: (bi, i, 0))
    prv = lambda i: jnp.maximum(i * A1_BLOCKS - 1, 0)
    nxt = lambda i: jnp.minimum((i + 1) * A1_BLOCKS, nb - 1)
    k_spec = lambda f: pl.BlockSpec((1, GQA_KV_HEADS, BLOCK, LANES), lambda bi, i: (bi, 0, f(i), 0))
    v_spec = lambda f: pl.BlockSpec((1, GQA_KV_HEADS, GQA_VT_ROWS, BLOCK), lambda bi, i: (bi, 0, 0, f(i)))
    vec_spec = pl.BlockSpec((1, 1, d), lambda bi, i: (bi, 0, 0))
    return pl.pallas_call(
        _attn1_kernel,
        grid=(b, ns),
        in_specs=[pl.BlockSpec(memory_space=pltpu.SMEM),
                  pl.BlockSpec((1, A1_BLOCKS, GQA_KV_HEADS, GQA_HEAD_DIM, GQA_GROUP * BLOCK),
                               lambda bi, i: (bi, i, 0, 0, 0)),
                  k_spec(prv), pl.BlockSpec((1, GQA_KV_HEADS, tr, LANES), lambda bi, i: (bi, 0, i, 0)), k_spec(nxt),
                  v_spec(prv), pl.BlockSpec((1, GQA_KV_HEADS, GQA_VT_ROWS, tr), lambda bi, i: (bi, 0, 0, i)),
                  v_spec(nxt),
                  pl.BlockSpec((1, GQA_KV_HEADS, nc, LANES), lambda bi, i: (bi, 0, 0, 0)),
                  pl.BlockSpec((1, GQA_KV_HEADS, GQA_VT_ROWS, nc), lambda bi, i: (bi, 0, 0, 0)),
                  _full(bias.shape), row(GQA_Q), row(d), vec_spec, _full(w_out.shape), _full(fg.shape)],
        out_specs=row(d),
        out_shape=jax.ShapeDtypeStruct((b, n, d), F32),
        scratch_shapes=[pltpu.VMEM((A1_SLOTS * 2 * BLOCK, GQA_GROUP * BLOCK), F32)],
        compiler_params=_cparams(("parallel", "arbitrary")),
        name="attn1",
    )(sink, qt, k, k, k, vt, vt, vt, ck, cvt, bias, sg, x1, g1, w_out, fg)


def _prep_even_weights(w_in, w_qb, w_kvb):
    d = w_in.shape[0]
    f_in, f_gate, q_a, kv_a, k_pe, m_gate = jnp.split(
        w_in, np.cumsum([F_WIDTH, F_WIDTH, MLA_Q_RANK, MLA_KV_RANK, MLA_ROPE]).tolist(), axis=1)
    kpe_blk = jnp.concatenate([jnp.zeros((d, MLA_NOPE), F32), k_pe, jnp.zeros((d, 32), F32)], axis=1)
    w_in_p = jnp.concatenate([f_in, f_gate, m_gate, q_a, kv_a, kpe_blk], axis=1).astype(BF16)
    w_qbt = w_qb.T.astype(BF16)
    kvb = w_kvb.reshape(MLA_KV_RANK, MLA_HEADS, MLA_NOPE + MLA_V)
    w_k = jnp.pad(kvb[:, :, :MLA_NOPE], ((0, 0), (0, 0), (0, 64))).reshape(MLA_KV_RANK, MLA_HEADS * LANES)
    w_vt = jnp.pad(kvb[:, :, MLA_NOPE:], ((0, 0), (0, 0), (0, MLA_VT_ROWS - MLA_V)))
    w_vt = w_vt.reshape(MLA_KV_RANK, MLA_HEADS * MLA_VT_ROWS).T.astype(BF16)
    vone = np.zeros((MLA_HEADS * MLA_VT_ROWS, 1), np.float32)
    vone[MLA_V::MLA_VT_ROWS] = 1.0
    return w_in_p, w_qbt, w_k.astype(BF16), w_vt, jnp.asarray(vone)


def _prep_odd_weights(w_in):
    d = w_in.shape[0]
    w_qt = w_in[:, O_Q:O_K].T.astype(BF16)
    kw = w_in[:, O_K:O_V].reshape(d, GQA_KV_HEADS, GQA_HEAD_DIM)
    w_k = jnp.pad(kw, ((0, 0), (0, 0), (0, LANES - GQA_HEAD_DIM))).reshape(d, GQA_KV_HEADS * LANES).astype(BF16)
    vw = w_in[:, O_V:O_G].reshape(d, GQA_KV_HEADS, GQA_HEAD_DIM)
    w_vt = jnp.pad(vw, ((0, 0), (0, 0), (0, GQA_VT_ROWS - GQA_HEAD_DIM)))
    w_vt = w_vt.reshape(d, GQA_KV_HEADS * GQA_VT_ROWS).T.astype(BF16)
    vone = np.zeros((GQA_KV_HEADS * GQA_VT_ROWS, 1), np.float32)
    vone[GQA_HEAD_DIM::GQA_VT_ROWS] = 1.0
    return w_qt, w_k, w_vt, jnp.asarray(vone), w_in[:, O_G:O_END].astype(BF16)


def _fft_consts(n, c):
    n1 = n // LANES
    c1, s1 = _dft_cs(n1)
    fa = np.block([[c1, s1], [-s1, c1]])
    c2, s2 = _dft_cs(LANES)
    fcs = np.concatenate([c2, s2], axis=1) / math.sqrt(n * F_GROUP_DIM)
    fc = np.concatenate([c2, -s2], axis=1)
    r = np.arange(LANES, dtype=np.int64)[:, None]
    k1 = np.arange(n1, dtype=np.int64)[None, :]
    ang = 2.0 * np.pi * ((r * k1) % n).astype(np.float64) / n
    tw = np.stack([np.cos(ang), np.sin(ang)])
    tw = np.broadcast_to(tw[..., None], (2, LANES, n1, LANES))
    cn, sn = _dft_cs(c)
    fn = np.concatenate([cn, sn], axis=1)
    as32 = lambda a: jnp.asarray(np.ascontiguousarray(a, dtype=np.float32))
    return as32(fa).astype(BF16), as32(fcs).astype(BF16), as32(fc).astype(BF16), as32(tw), as32(fn).astype(BF16)


def _pick_tile(n, pref):
    t = pref
    while n % t:
        t //= 2
    return t


def kernel(x, c, ctx, c_ctx, w_mod, b_mod, norm_g, e_w_in, e_q_norm, e_w_qb, e_kv_norm, e_w_kvb, e_w_out,
           o_w_in, o_sink, o_w_out, final_g):
    b, n, d = x.shape
    nc = ctx.shape[1]
    assert d == D_MODEL and n % (LANES * 8) == 0 and nc % LANES == 0 and b <= 7

    svec = jnp.zeros((8, d), F32).at[:b].set(c).at[b].set(c_ctx)
    mod = _mod_call(svec, w_mod, b_mod)
    sh, sc, gt = mod[:, :, :d], mod[:, :, d:2 * d], mod[:, :, 2 * d:]
    lat = lambda t, l: t[l, :b].reshape(b, 1, d)
    cvec = lambda t, l: t[l, b].reshape(1, d)
    ng0, ng1 = norm_g[0].reshape(1, d), norm_g[1].reshape(1, d)

    cos_m, sin_m = _axial_rope_tables(n, MLA_ROPE)
    qs = MLA_SCALE * LOG2E
    tqt0 = _const(np.stack([cos_m.T, sin_m.T]) * qs)
    tk0 = _mla_rope_tab(cos_m, sin_m)
    one_c, zero_c = np.ones((nc, MLA_ROPE // 2)), np.zeros((nc, MLA_ROPE // 2))
    tqt0c = _const(np.stack([one_c.T, zero_c.T]) * qs)
    tk0c = _mla_rope_tab(one_c, zero_c)
    cos_g, sin_g = _axial_rope_tables(n, GQA_HEAD_DIM)
    tqt1 = _const(np.stack([cos_g.T, sin_g.T]) * (GQA_SCALE * LOG2E))
    tk1 = _gqa_rope_tab(cos_g, sin_g)
    fa, fcs, fc, tw, fn = _fft_consts(n, nc)
    bias1 = _band_bias(nc)

    w_in0, w_qbt0, w_k0, w_vt0, vone = _prep_even_weights(e_w_in[0], e_w_qb[0], e_w_kvb[0])
    qn0 = e_q_norm[0].reshape(1, MLA_Q_RANK)
    kvn0 = e_kv_norm[0].reshape(1, MLA_KV_RANK)
    w_out0 = e_w_out[0].astype(BF16)
    w_qt1, w_k1, w_vt1, vone1, w_g1 = _prep_odd_weights(o_w_in[0])
    w_out1 = o_w_out[0].astype(BF16)

    kc, vct, ck1, cvt1 = _ctx_call(
        ctx, (ng0, cvec(sc, 0), cvec(sh, 0), cvec(gt, 0)), (ng1, cvec(sc, 1), cvec(sh, 1)),
        w_in0, qn0, w_qbt0, kvn0, w_k0, w_vt0, vone, tqt0c, tk0c, fc, fn, w_out0, w_k1, w_vt1, vone1)

    tm = _pick_tile(n, 512)
    qt0, k0, vt0, zr, zi, gf, gm = _front0_call(
        x, ng0, lat(sc, 0), lat(sh, 0), w_in0, qn0, w_qbt0, kvn0, w_k0, w_vt0, vone, tqt0, tk0, fc, tm)

    ag = _attn0_call(qt0, kc, vct, k0, vt0, gm, _pick_tile(n, 512))

    n1 = n // LANES
    tr, ti = _fft_a_call(zr.reshape(b, n1, LANES * F_WIDTH), zi.reshape(b, n1, LANES * F_WIDTH), fa, tw, 8)
    fmg = _fft_c_call(tr.reshape(b, n1, LANES, F_WIDTH), ti.reshape(b, n1, LANES, F_WIDTH), fcs,
                      gf.reshape(b, LANES, n1 * F_WIDTH), _pick_tile(n1, 8))
    fmg = fmg.reshape(b, n, F_WIDTH)

    tm1 = _pick_tile(n, 512)
    x1, qt1, k1, vt1, sg = _mid_call(fmg, ag, x, lat(gt, 0), w_out0, ng1, lat(sc, 1), lat(sh, 1),
                                     w_qt1, w_k1, w_vt1, vone1, w_g1, tqt1, tk1, tm1)

    return _attn1_call(o_sink[0], qt1, k1, vt1, ck1, cvt1, bias1, sg, x1, lat(gt, 1), w_out1,
                       final_g.reshape(1, d))
```

```python
import functools
import math

import numpy as np
import jax
import jax.numpy as jnp
from jax import lax
from jax.experimental import pallas as pl
from jax.experimental.pallas import tpu as pltpu

F32 = jnp.float32
BF16 = jnp.bfloat16

D_MODEL = 1024
GRID_W = 64
EPS = 1e-6
ROPE_BASE = 10000.0
LANES = 128

F_GROUPS = 4
F_GROUP_DIM = 128
F_WIDTH = F_GROUPS * F_GROUP_DIM
FFT_TILE = 16

MLA_HEADS = 8
MLA_NOPE = 64
MLA_ROPE = 32
MLA_V = 64
MLA_Q_RANK = 384
MLA_KV_RANK = 256
MLA_WIDTH = MLA_HEADS * MLA_V
MLA_SCALE = 1.0 / math.sqrt(MLA_NOPE + MLA_ROPE)
LOG2E = math.log2(math.e)
MLA_VT_ROWS = 80
PV_KEYS = 256
MLA_TQ = 256
SOFTMAX_GROUP = 2
PIPE_SKEW = 3
SCORE_SLOTS = 4
GATE_LAG = 2

GQA_HEADS = 16
GQA_KV_HEADS = 4
GQA_GROUP = GQA_HEADS // GQA_KV_HEADS
GQA_HEAD_DIM = 64
WINDOW = 128
BLOCK = 128
GQA_Q = GQA_HEADS * GQA_HEAD_DIM
GQA_KV = GQA_KV_HEADS * GQA_HEAD_DIM
GQA_SCALE = 1.0 / math.sqrt(GQA_HEAD_DIM)
GQA_VT_ROWS = 80
A1_BLOCKS = 4
A1_SKEW = 5
A1_SLOTS = 6
A1_GATE_LAG = 3

E_FIN, E_FGATE, E_MGATE, E_QA, E_KVA, E_KPE, E_END = 0, 512, 1024, 1536, 1920, 2176, 2304
O_Q, O_K, O_V, O_G, O_END = 0, 1024, 1280, 1536, 2560

NEG_BIG = -1e30
VMEM_LIMIT = 56 * 1024 * 1024


def _cparams(sem):
    return pltpu.CompilerParams(dimension_semantics=sem, vmem_limit_bytes=VMEM_LIMIT)


def _dot(a, b):
    return jnp.dot(a, b, preferred_element_type=F32)


def _dot_nt(a, b):
    return lax.dot_general(a, b, (((1,), (1,)), ((), ())), preferred_element_type=F32)


def _rms(x, g):
    return x * lax.rsqrt(jnp.mean(x * x, axis=-1, keepdims=True) + EPS) * g


def _silu(x):
    return x * jax.nn.sigmoid(x)


def _rope(x, tab_ref, shift):
    up = pltpu.roll(x, LANES - shift, 1)
    dn = pltpu.roll(x, shift, 1)
    return x * tab_ref[0] + up * tab_ref[1] + dn * tab_ref[2]


def _full(shape):
    nd = len(shape)
    return pl.BlockSpec(shape, lambda *_: (0,) * nd)


def _axial_rope_tables(n, rot_dim):
    rows = n // GRID_W
    row = np.repeat(np.arange(rows), GRID_W).astype(np.float64)
    col = np.tile(np.arange(GRID_W), rows).astype(np.float64)
    nf = rot_dim // 4
    inv = (np.float32(ROPE_BASE) ** (-np.arange(nf, dtype=np.float32) / np.float32(nf))).astype(np.float64)
    ang = np.concatenate([row[:, None] * inv, col[:, None] * inv], axis=-1)
    return np.cos(ang), np.sin(ang)


def _mla_rope_tab(cos, sin):
    n = cos.shape[0]
    z = lambda w: np.zeros((n, w))
    cf = np.concatenate([np.ones((n, MLA_NOPE)), cos, cos, z(32)], axis=-1)
    s1 = np.concatenate([z(MLA_NOPE), -sin, z(16), z(32)], axis=-1)
    s2 = np.concatenate([z(MLA_NOPE), z(16), sin, z(32)], axis=-1)
    return _const(np.stack([cf, s1, s2]))


def _gqa_rope_tab(cos, sin):
    z = np.zeros_like(sin)
    cf = np.concatenate([cos, cos, z, z], axis=-1)
    s1 = np.concatenate([-sin, z, z, z], axis=-1)
    s2 = np.concatenate([z, sin, z, z], axis=-1)
    return _const(np.stack([cf, s1, s2]))


def _const(a):
    return jnp.asarray(np.ascontiguousarray(a, dtype=np.float32))


def _dft_cs(n):
    idx = np.arange(n, dtype=np.int64)
    ang = 2.0 * np.pi * ((idx[:, None] * idx[None, :]) % n).astype(np.float64) / n
    return np.cos(ang), np.sin(ang)


def _mod_kernel(s_ref, w_ref, b_ref, o_ref):
    s = _silu(s_ref[...])
    o_ref[0] = _dot(s.astype(BF16), w_ref[0].astype(BF16)) + b_ref[0]


def _mod_call(svec, w_mod, b_mod):
    depth, d, d3 = w_mod.shape
    tn = 768
    return pl.pallas_call(
        _mod_kernel,
        grid=(depth, d3 // tn),
        in_specs=[
            pl.BlockSpec((8, d), lambda l, j: (0, 0)),
            pl.BlockSpec((1, d, tn), lambda l, j: (l, 0, j)),
            pl.BlockSpec((1, 1, tn), lambda l, j: (l, 0, j)),
        ],
        out_specs=pl.BlockSpec((1, 8, tn), lambda l, j: (l, 0, j)),
        out_shape=jax.ShapeDtypeStruct((depth, 8, d3), F32),
        compiler_params=_cparams(("arbitrary", "arbitrary")),
        name="mod",
    )(svec, w_mod, b_mod.reshape(depth, 1, d3))


def _even_front(x, ng, sc, sh, w_in_ref, qn, w_qbt_ref, kvn, w_k_ref, w_vt_ref, vone, tqt_ref, tk_ref, fc_ref):
    m = x.shape[0]
    h = (_rms(x, ng) * (1.0 + sc) + sh).astype(BF16)
    f_in = _dot(h, w_in_ref[:, E_FIN:E_FGATE])
    f_gate = _dot(h, w_in_ref[:, E_FGATE:E_MGATE])
    m_gate = _dot(h, w_in_ref[:, E_MGATE:E_QA])
    low_rank = _dot(h, w_in_ref[:, E_QA:E_END])
    q_a = low_rank[:, 0:E_KVA - E_QA]
    kv_a = low_rank[:, E_KVA - E_QA:E_KPE - E_QA]
    kpe = low_rank[:, E_KPE - E_QA:E_END - E_QA]
    qh = _rms(q_a, qn).astype(BF16)
    ch = _rms(kv_a, kvn).astype(BF16)
    kpe_r = _rope(kpe, tk_ref, 16)
    qt = _dot_nt(w_qbt_ref[...], qh)
    k_all = _dot(ch, w_k_ref[...])
    cos = tqt_ref[0]
    sin = tqt_ref[1]
    pad = jnp.zeros((LANES - MLA_NOPE - MLA_ROPE, m), F32)
    qts, ks = [], []
    for hd in range(MLA_HEADS):
        lo = hd * LANES
        ql = hd * (MLA_NOPE + MLA_ROPE)
        x1 = qt[ql + MLA_NOPE:ql + MLA_NOPE + 16]
        x2 = qt[ql + MLA_NOPE + 16:ql + MLA_NOPE + 32]
        qts.append(jnp.concatenate(
            [qt[ql:ql + MLA_NOPE] * (MLA_SCALE * LOG2E), x1 * cos - x2 * sin, x2 * cos + x1 * sin, pad], axis=0))
        ks.append(k_all[:, lo:lo + LANES] + kpe_r)
    vt = _dot_nt(w_vt_ref[...], ch) + vone
    zr, zi = [], []
    fb = f_in.astype(BF16)
    for g in range(F_GROUPS):
        z = _dot(fb[:, g * LANES:(g + 1) * LANES], fc_ref[...])
        zr.append(z[:, :LANES])
        zi.append(z[:, LANES:])
    return qts, ks, vt, zr, zi, f_gate, m_gate


def _front0_kernel(x_ref, ng_ref, sc_ref, sh_ref, w_in_ref, qn_ref, w_qbt_ref, kvn_ref, w_k_ref, w_vt_ref,
                   vone_ref, tqt_ref, tk_ref, fc_ref, q_ref, k_ref, v_ref, zr_ref, zi_ref, gf_ref, gm_ref):
    qts, ks, vt, zr, zi, f_gate, m_gate = _even_front(
        x_ref[0], ng_ref[...], sc_ref[0], sh_ref[0], w_in_ref, qn_ref[...], w_qbt_ref, kvn_ref[...],
        w_k_ref, w_vt_ref, vone_ref[...], tqt_ref, tk_ref, fc_ref)
    for hd in range(MLA_HEADS):
        q_ref[0, hd] = qts[hd].astype(BF16)
        k_ref[0, hd] = ks[hd].astype(BF16)
        v_ref[0, hd, 0] = vt[hd * MLA_VT_ROWS:(hd + 1) * MLA_VT_ROWS].astype(BF16)
    for g in range(F_GROUPS):
        zr_ref[0, :, g * LANES:(g + 1) * LANES] = zr[g].astype(BF16)
        zi_ref[0, :, g * LANES:(g + 1) * LANES] = zi[g].astype(BF16)
    gf_ref[0] = _silu(f_gate).astype(BF16)
    gm_ref[0] = _silu(m_gate).astype(BF16)


def _front0_call(x, ng, sc, sh, w_in, qn, w_qbt, kvn, w_k, w_vt, vone, tqt, tk, fc, tm):
    b, n, d = x.shape
    fw = jax.ShapeDtypeStruct((b, n, F_WIDTH), BF16)
    row_spec = pl.BlockSpec((1, tm, F_WIDTH), lambda bi, i: (bi, i, 0))
    vec_spec = pl.BlockSpec((1, 1, d), lambda bi, i: (bi, 0, 0))
    return pl.pallas_call(
        _front0_kernel,
        grid=(b, n // tm),
        in_specs=[
            pl.BlockSpec((1, tm, d), lambda bi, i: (bi, i, 0)),
            _full(ng.shape), vec_spec, vec_spec,
            _full(w_in.shape), _full(qn.shape), _full(w_qbt.shape), _full(kvn.shape), _full(w_k.shape),
            _full(w_vt.shape), _full(vone.shape),
            pl.BlockSpec((2, 16, tm), lambda bi, i: (0, 0, i)),
            pl.BlockSpec((3, tm, LANES), lambda bi, i: (0, i, 0)),
            _full(fc.shape),
        ],
        out_specs=[
            pl.BlockSpec((1, MLA_HEADS, LANES, tm), lambda bi, i: (bi, 0, 0, i)),
            pl.BlockSpec((1, MLA_HEADS, tm, LANES), lambda bi, i: (bi, 0, i, 0)),
            pl.BlockSpec((1, MLA_HEADS, 1, MLA_VT_ROWS, tm), lambda bi, i: (bi, 0, i, 0, 0)),
            row_spec, row_spec, row_spec, row_spec],
        out_shape=[
            jax.ShapeDtypeStruct((b, MLA_HEADS, LANES, n), BF16),
            jax.ShapeDtypeStruct((b, MLA_HEADS, n, LANES), BF16),
            jax.ShapeDtypeStruct((b, MLA_HEADS, n // tm, MLA_VT_ROWS, tm), BF16),
            fw, fw, fw, fw],
        compiler_params=_cparams(("parallel", "arbitrary")),
        name="front0",
    )(x, ng, sc, sh, w_in, qn, w_qbt, kvn, w_k, w_vt, vone, tqt, tk, fc)


def _zero_after(x):
    u = lax.bitcast_convert_type(x, jnp.uint32)
    z = lax.shift_right_logical(lax.shift_right_logical(u, jnp.uint32(16)), jnp.uint32(16))
    return lax.bitcast_convert_type(z, F32)


def _mla_chunk(qt, k, vt, m, acc):
    s = _dot(k, qt)
    m_new = jnp.maximum(m, jnp.max(s, axis=0, keepdims=True))
    alpha = jnp.exp2(m - m_new)
    p = jnp.exp2(s - m_new).astype(BF16)
    return m_new, alpha * acc + _dot(vt, p)


def _mla_pair_out(acc0, acc1):
    o0 = acc0[0:MLA_V] / acc0[MLA_V:MLA_V + 1]
    o1 = acc1[0:MLA_V] / acc1[MLA_V:MLA_V + 1]
    return jnp.concatenate([o0, o1], axis=0).T


def _ctx_kernel(ctx_scale, x_ref, ng0_ref, sc0_ref, sh0_ref, g0_ref, ng1_ref, sc1_ref, sh1_ref,
                w_in_ref, qn_ref, w_qbt_ref, kvn_ref, w_k_ref, w_vt_ref, vone_ref, tqt_ref, tk_ref, fc_ref,
                fn_ref, w_out_ref, w_k1_ref, w_vt1_ref, vone1_ref,
                kc_ref, vc_ref, ck1_ref, cv1_ref):
    x = x_ref[0]
    c = x.shape[0]
    qts, ks, vt, zr, zi, f_gate, m_gate = _even_front(
        x, ng0_ref[...], sc0_ref[...], sh0_ref[...], w_in_ref, qn_ref[...], w_qbt_ref, kvn_ref[...],
        w_k_ref, w_vt_ref, vone_ref[...], tqt_ref, tk_ref, fc_ref)
    pairs = []
    for hp in range(MLA_HEADS // 2):
        accs = []
        for e in range(2):
            hd = 2 * hp + e
            kb = ks[hd].astype(BF16)
            vtb = vt[hd * MLA_VT_ROWS:(hd + 1) * MLA_VT_ROWS].astype(BF16)
            kc_ref[0, hd] = kb
            vc_ref[0, hd, 0] = vtb
            m0 = jnp.full((1, c), -jnp.inf, F32)
            acc0 = jnp.zeros((MLA_VT_ROWS, c), F32)
            accs.append(_mla_chunk(qts[hd].astype(BF16), kb, vtb, m0, acc0)[1])
        pairs.append(_mla_pair_out(accs[0], accs[1]))
    a = jnp.concatenate(pairs, axis=-1)
    z = jnp.concatenate([jnp.concatenate(zr, axis=-1), jnp.concatenate(zi, axis=-1)], axis=0).astype(BF16)
    fm = _dot(fn_ref[...], z) * ctx_scale
    yf = (fm * _silu(f_gate)).astype(BF16)
    ya = (a * _silu(m_gate)).astype(BF16)
    y = _dot(yf, w_out_ref[0:F_WIDTH, :]) + _dot(ya, w_out_ref[F_WIDTH:, :])
    x1 = x + g0_ref[...] * y
    h1 = (_rms(x1, ng1_ref[...]) * (1.0 + sc1_ref[...]) + sh1_ref[...]).astype(BF16)
    vt1 = _dot_nt(w_vt1_ref[...], h1) + vone1_ref[...]
    for g in range(GQA_KV_HEADS):
        ck1_ref[0, g] = _dot(h1, w_k1_ref[:, g * LANES:(g + 1) * LANES]).astype(BF16)
        cv1_ref[0, g] = vt1[g * GQA_VT_ROWS:(g + 1) * GQA_VT_ROWS].astype(BF16)


def _ctx_call(ctx, vecs0, vecs1, w_in, qn, w_qbt, kvn, w_k, w_vt, vone, tqt, tk, fc, fn, w_out, w_k1, w_vt1,
              vone1):
    b, c, d = ctx.shape
    ng0, sc0, sh0, g0 = vecs0
    ng1, sc1, sh1 = vecs1
    ctx_scale = 1.0 / math.sqrt(c * F_GROUP_DIM)
    consts = [ng0, sc0, sh0, g0, ng1, sc1, sh1, w_in, qn, w_qbt, kvn, w_k, w_vt, vone, tqt, tk, fc, fn, w_out,
              w_k1, w_vt1, vone1]
    return pl.pallas_call(
        functools.partial(_ctx_kernel, ctx_scale),
        grid=(b,),
        in_specs=[pl.BlockSpec((1, c, d), lambda bi: (bi, 0, 0))] + [_full(a.shape) for a in consts],
        out_specs=[pl.BlockSpec((1, MLA_HEADS, c, LANES), lambda bi: (bi, 0, 0, 0)),
                   pl.BlockSpec((1, MLA_HEADS, 1, MLA_VT_ROWS, c), lambda bi: (bi, 0, 0, 0, 0)),
                   pl.BlockSpec((1, GQA_KV_HEADS, c, LANES), lambda bi: (bi, 0, 0, 0)),
                   pl.BlockSpec((1, GQA_KV_HEADS, GQA_VT_ROWS, c), lambda bi: (bi, 0, 0, 0))],
        out_shape=[jax.ShapeDtypeStruct((b, MLA_HEADS, c, LANES), BF16),
                   jax.ShapeDtypeStruct((b, MLA_HEADS, 1, MLA_VT_ROWS, c), BF16),
                   jax.ShapeDtypeStruct((b, GQA_KV_HEADS, c, LANES), BF16),
                   jax.ShapeDtypeStruct((b, GQA_KV_HEADS, GQA_VT_ROWS, c), BF16)],
        compiler_params=_cparams(("arbitrary",)),
        name="ctx",
    )(ctx, *consts)


def _attn0_kernel(qt_ref, kc_ref, vct_ref, kl_ref, vlt_ref, gm_ref, o_ref, s_ref):
    tq = qt_ref.shape[-1]
    nch, _, tk = vlt_ref.shape[2:]
    nc = kc_ref.shape[2]
    n_ctx = nc // PV_KEYS
    n_sub = n_ctx + nch * tk // PV_KEYS
    group = [0] * n_ctx + [1 + j // SOFTMAX_GROUP for j in range(n_sub - n_ctx)]
    first = [t for t in range(n_sub) if t == 0 or group[t] != group[t - 1]]
    last = [t for t in range(n_sub) if t == n_sub - 1 or group[t] != group[t + 1]]

    def keys(e, t):
        if t < n_ctx:
            return kc_ref[0, e, t * PV_KEYS:(t + 1) * PV_KEYS, :]
        lo = (t - n_ctx) * PV_KEYS
        return kl_ref[0, e, lo:lo + PV_KEYS, :]

    def values_t(e, t):
        if t < n_ctx:
            return vct_ref[0, e, 0, :, t * PV_KEYS:(t + 1) * PV_KEYS]
        j, off = divmod((t - n_ctx) * PV_KEYS, tk)
        return vlt_ref[0, e, j, :, off:off + PV_KEYS]

    def slot(i):
        r = i % SCORE_SLOTS
        return slice(r * PV_KEYS, (r + 1) * PV_KEYS)

    tasks = [(qi, t) for qi in range(tq // MLA_TQ) for t in range(n_sub)]
    colmax = []
    for step in range(len(tasks) + PIPE_SKEW):
        if step < len(tasks):
            qi, t = tasks[step]
            cols = slice(qi * MLA_TQ, (qi + 1) * MLA_TQ)
            cm = []
            for e in range(2):
                s = _dot(keys(e, t), qt_ref[0, e, :, cols])
                s_ref[e, slot(step)] = s
                cm.append(jnp.max(s, axis=0, keepdims=True))
            colmax.append(cm)
        if step < PIPE_SKEW:
            continue
        qi, u = tasks[step - PIPE_SKEW]
        if u == 0:
            m = [jnp.full((1, MLA_TQ), -jnp.inf, F32) for _ in range(2)]
            acc = [jnp.zeros((MLA_VT_ROWS, MLA_TQ), F32) for _ in range(2)]
        if u in first:
            members = [qi * n_sub + t for t in range(n_sub) if group[t] == group[u]]
            assert members[-1] <= step
            m_new, alpha = [], []
            for e in range(2):
                mc = functools.reduce(jnp.maximum, [colmax[i][e] for i in members])
                m_new.append(jnp.maximum(m[e], mc))
                alpha.append(jnp.exp2(m[e] - m_new[-1]))
            pv = [None, None]
        gate = min(step - GATE_LAG, len(tasks) - 1)
        for e in range(2):
            m_use = m_new[e] + _zero_after(colmax[gate][e])
            p = jnp.exp2(s_ref[e, slot(step - PIPE_SKEW)] - m_use).astype(BF16)
            d = _dot(values_t(e, u), p)
            pv[e] = d if pv[e] is None else pv[e] + d
        if u in last:
            for e in range(2):
                acc[e] = alpha[e] * acc[e] + pv[e]
                m[e] = m_new[e]
        if u == n_sub - 1:
            rows = slice(qi * MLA_TQ, (qi + 1) * MLA_TQ)
            o_ref[0, rows] = (_mla_pair_out(acc[0], acc[1]) * gm_ref[0, rows].astype(F32)).astype(BF16)


def _attn0_call(qt, kc, vct, kl, vlt, gm, tq):
    b, h, _, n = qt.shape
    c = kc.shape[2]
    nch, _, tk = vlt.shape[2:]
    assert c % PV_KEYS == 0 and tk % PV_KEYS == 0 and SCORE_SLOTS > PIPE_SKEW >= SOFTMAX_GROUP - 1
    assert tq % MLA_TQ == 0
    o_spec = pl.BlockSpec((1, tq, LANES), lambda bi, hp, i: (bi, i, hp))
    return pl.pallas_call(
        _attn0_kernel,
        grid=(b, h // 2, n // tq),
        in_specs=[
            pl.BlockSpec((1, 2, LANES, tq), lambda bi, hp, i: (bi, hp, 0, i)),
            pl.BlockSpec((1, 2, c, LANES), lambda bi, hp, i: (bi, hp, 0, 0)),
            pl.BlockSpec((1, 2, 1, MLA_VT_ROWS, c), lambda bi, hp, i: (bi, hp, 0, 0, 0)),
            pl.BlockSpec((1, 2, n, LANES), lambda bi, hp, i: (bi, hp, 0, 0)),
            pl.BlockSpec((1, 2, nch, MLA_VT_ROWS, tk), lambda bi, hp, i: (bi, hp, 0, 0, 0)),
            o_spec],
        out_specs=o_spec,
        out_shape=jax.ShapeDtypeStruct((b, n, MLA_WIDTH), BF16),
        scratch_shapes=[pltpu.VMEM((2, SCORE_SLOTS * PV_KEYS, MLA_TQ), F32)],
        compiler_params=_cparams(("parallel", "parallel", "arbitrary")),
        name="attn0",
    )(qt, kc, vct, kl, vlt, gm)


def _swap16(p16, blocks):
    ys = [_dot(p16, blk).astype(BF16) for blk in blocks]
    return [jnp.concatenate([y[t * FFT_TILE:(t + 1) * FFT_TILE] for y in ys], axis=0) for t in range(FFT_TILE)]


def _fft_a_kernel(n1, zr_ref, zi_ref, p16_ref, fa_ref, tw_ref, tr_ref, ti_ref):
    rows = n1 * FFT_TILE
    z = jnp.concatenate([zr_ref[0].reshape(rows, F_WIDTH), zi_ref[0].reshape(rows, F_WIDTH)], axis=1)
    grp = FFT_TILE * FFT_TILE
    per_r = _swap16(p16_ref[...], [z[j * grp:(j + 1) * grp] for j in range(rows // grp)])
    for r in range(FFT_TILE):
        zz = per_r[r]
        a = _dot(fa_ref[...], jnp.concatenate([zz[:, :F_WIDTH], zz[:, F_WIDTH:]], axis=0))
        c = tw_ref[0, r]
        s = tw_ref[1, r]
        for g in range(F_GROUPS):
            ar = a[0:n1, g * LANES:(g + 1) * LANES]
            ai = a[n1:2 * n1, g * LANES:(g + 1) * LANES]
            tr_ref[0, r, :, g * LANES:(g + 1) * LANES] = (ar * c + ai * s).astype(BF16)
            ti_ref[0, r, :, g * LANES:(g + 1) * LANES] = (ai * c - ar * s).astype(BF16)


def _fft_a_call(zr, zi, p16, fa, tw):
    b, n1, r, w = zr.shape
    in_spec = pl.BlockSpec((1, n1, FFT_TILE, w), lambda bi, j: (bi, 0, j, 0))
    out_spec = pl.BlockSpec((1, FFT_TILE, n1, w), lambda bi, j: (bi, j, 0, 0))
    out = jax.ShapeDtypeStruct((b, r, n1, w), BF16)
    return pl.pallas_call(
        functools.partial(_fft_a_kernel, n1),
        grid=(b, r // FFT_TILE),
        in_specs=[in_spec, in_spec, _full(p16.shape), _full(fa.shape),
                  pl.BlockSpec((2, FFT_TILE, n1, LANES), lambda bi, j: (0, j, 0, 0))],
        out_specs=[out_spec, out_spec],
        out_shape=[out, out],
        compiler_params=_cparams(("parallel", "arbitrary")),
        name="fft_a",
    )(zr, zi, p16, fa, tw)


def _fft_c_kernel(tr_ref, ti_ref, p16_ref, fcs_ref, gf_ref, o_ref):
    nr = tr_ref.shape[1]
    rows = nr * FFT_TILE
    grp = FFT_TILE * FFT_TILE
    t = jnp.concatenate([tr_ref[0].reshape(rows, F_WIDTH), ti_ref[0].reshape(rows, F_WIDTH)], axis=1)
    per_k = _swap16(p16_ref[...], [t[j * grp:(j + 1) * grp] for j in range(rows // grp)])
    outs = []
    for k in range(FFT_TILE):
        tt = per_k[k]
        rhs = jnp.concatenate([tt[:, :F_WIDTH], tt[:, F_WIDTH:]], axis=0)
        outs.append(_dot(fcs_ref[...], rhs).astype(BF16))
    gf = gf_ref[0].reshape(rows, F_WIDTH)
    for j in range(nr // FFT_TILE):
        blk = jnp.concatenate([o[j * FFT_TILE:(j + 1) * FFT_TILE] for o in outs], axis=0)
        y = _dot(p16_ref[...], blk)
        o_ref[0, j * FFT_TILE:(j + 1) * FFT_TILE] = (
            y * gf[j * grp:(j + 1) * grp].astype(F32)).astype(BF16).reshape(FFT_TILE, FFT_TILE, F_WIDTH)


def _fft_c_call(tr, ti, p16, fcs, gf):
    b, r, n1, w = tr.shape
    spec = pl.BlockSpec((1, r, FFT_TILE, w), lambda bi, i: (bi, 0, i, 0))
    return pl.pallas_call(
        _fft_c_kernel,
        grid=(b, n1 // FFT_TILE),
        in_specs=[spec, spec, _full(p16.shape), _full(fcs.shape), spec],
        out_specs=spec,
        out_shape=jax.ShapeDtypeStruct((b, r, n1, w), BF16),
        compiler_params=_cparams(("parallel", "arbitrary")),
        name="fft_c",
    )(tr, ti, p16, fcs, gf)


def _mid_kernel(fm_ref, ag_ref, x_ref, g0_ref, w_out_ref, ng_ref, sc_ref, sh_ref, w_qt_ref, w_k_ref, w_vt_ref,
                vone_ref, w_g_ref, tqt_ref, tk_ref, x1_ref, q_ref, k_ref, v_ref, sg_ref):
    tm = x_ref.shape[1]
    y = _dot(fm_ref[0], w_out_ref[0:F_WIDTH, :]) + _dot(ag_ref[0], w_out_ref[F_WIDTH:, :])
    x1 = x_ref[0] + g0_ref[0] * y
    x1_ref[0] = x1
    h = (_rms(x1, ng_ref[...]) * (1.0 + sc_ref[0]) + sh_ref[0]).astype(BF16)
    qt = _dot_nt(w_qt_ref[...], h)
    cos = tqt_ref[0]
    sin = tqt_ref[1]
    half = GQA_HEAD_DIM // 2
    for hd in range(GQA_HEADS):
        lo = hd * GQA_HEAD_DIM
        x1r = qt[lo:lo + half]
        x2r = qt[lo + half:lo + GQA_HEAD_DIM]
        qh = jnp.concatenate([x1r * cos - x2r * sin, x2r * cos + x1r * sin], axis=0).astype(BF16)
        g, j = divmod(hd, GQA_GROUP)
        for blk in range(tm // BLOCK):
            q_ref[0, blk, g, :, j * BLOCK:(j + 1) * BLOCK] = qh[:, blk * BLOCK:(blk + 1) * BLOCK]
    vt = _dot_nt(w_vt_ref[...], h) + vone_ref[...]
    k_all = _dot(h, w_k_ref[...])
    for g in range(GQA_KV_HEADS):
        k_ref[0, g] = _rope(k_all[:, g * LANES:(g + 1) * LANES], tk_ref, 32).astype(BF16)
        v_ref[0, g] = vt[g * GQA_VT_ROWS:(g + 1) * GQA_VT_ROWS].astype(BF16)
    sg_ref[0] = _silu(_dot(h, w_g_ref[...])).astype(BF16)


def _mid_call(fm, ag, x, g0, w_out, ng, sc, sh, w_qt, w_k, w_vt, vone, w_g, tqt, tk, tm):
    b, n, d = x.shape
    row = lambda w: pl.BlockSpec((1, tm, w), lambda bi, i: (bi, i, 0))
    vec_spec = pl.BlockSpec((1, 1, d), lambda bi, i: (bi, 0, 0))
    nblk = tm // BLOCK
    return pl.pallas_call(
        _mid_kernel,
        grid=(b, n // tm),
        in_specs=[row(F_WIDTH), row(MLA_WIDTH), row(d), vec_spec, _full(w_out.shape), _full(ng.shape),
                  vec_spec, vec_spec, _full(w_qt.shape), _full(w_k.shape), _full(w_vt.shape), _full(vone.shape),
                  _full(w_g.shape),
                  pl.BlockSpec((2, GQA_HEAD_DIM // 2, tm), lambda bi, i: (0, 0, i)),
                  pl.BlockSpec((3, tm, LANES), lambda bi, i: (0, i, 0))],
        out_specs=[row(d),
                   pl.BlockSpec((1, nblk, GQA_KV_HEADS, GQA_HEAD_DIM, GQA_GROUP * BLOCK),
                                lambda bi, i: (bi, i, 0, 0, 0)),
                   pl.BlockSpec((1, GQA_KV_HEADS, tm, LANES), lambda bi, i: (bi, 0, i, 0)),
                   pl.BlockSpec((1, GQA_KV_HEADS, GQA_VT_ROWS, tm), lambda bi, i: (bi, 0, 0, i)),
                   row(GQA_Q)],
        out_shape=[jax.ShapeDtypeStruct((b, n, d), F32),
                   jax.ShapeDtypeStruct((b, n // BLOCK, GQA_KV_HEADS, GQA_HEAD_DIM, GQA_GROUP * BLOCK), BF16),
                   jax.ShapeDtypeStruct((b, GQA_KV_HEADS, n, LANES), BF16),
                   jax.ShapeDtypeStruct((b, GQA_KV_HEADS, GQA_VT_ROWS, n), BF16),
                   jax.ShapeDtypeStruct((b, n, GQA_Q), BF16)],
        compiler_params=_cparams(("parallel", "arbitrary")),
        name="mid",
    )(fm, ag, x, g0, w_out, ng, sc, sh, w_qt, w_k, w_vt, vone, w_g, tqt, tk)


def _band_bias(nc):
    col = np.arange(3 * BLOCK)[:, None]
    q = np.arange(BLOCK)[None, :]
    dist = BLOCK + q - col
    band = np.abs(dist) <= WINDOW
    variants = [band & (col >= BLOCK), band, band & (col < 2 * BLOCK)]
    out = np.zeros((3, nc + 3 * BLOCK, BLOCK), np.float32)
    for v, ok in enumerate(variants):
        out[v, nc:] = np.where(ok, 0.0, NEG_BIG)
    return jnp.asarray(out)


def _attn1_kernel(sink_ref, q_ref, kp_ref, kc_ref, kn_ref, vp_ref, vc_ref, vn_ref, ck_ref, cv_ref, bias_ref,
                  sg_ref, x1_ref, g1_ref, w_out_ref, fg_ref, o_ref, s_ref):
    i = pl.program_id(1)
    last = pl.num_programs(1) - 1
    nc = ck_ref.shape[2]
    sel = []
    for blk in range(A1_BLOCKS):
        v = 1
        if blk == 0:
            v = jnp.where(i == 0, 0, v)
        if blk == A1_BLOCKS - 1:
            v = jnp.where(i == last, 2, v)
        sel.append(v)

    def band_keys(g, blk, j):
        pos = blk + j - 1
        if pos < 0:
            return kp_ref[0, g], vp_ref[0, g]
        if pos >= A1_BLOCKS:
            return kn_ref[0, g], vn_ref[0, g]
        return (kc_ref[0, g, pos * BLOCK:(pos + 1) * BLOCK, :], vc_ref[0, g, :, pos * BLOCK:(pos + 1) * BLOCK])

    def operands(blk, g, sb):
        if sb == 0:
            return ck_ref[0, g][:, :GQA_HEAD_DIM], cv_ref[0, g], None
        if sb == 1:
            (k0, v0), (k1, v1) = band_keys(g, blk, 0), band_keys(g, blk, 1)
            return (jnp.concatenate([k0, k1], axis=0)[:, :GQA_HEAD_DIM], jnp.concatenate([v0, v1], axis=1),
                    bias_ref[sel[blk], nc:nc + 2 * BLOCK])
        k2, v2 = band_keys(g, blk, 2)
        return k2[:, :GQA_HEAD_DIM], v2, bias_ref[sel[blk], nc + 2 * BLOCK:nc + 3 * BLOCK]

    def slot(idx, rows):
        r = idx % A1_SLOTS
        return slice(r * 2 * BLOCK, r * 2 * BLOCK + rows)

    tasks = [(blk, g, sb) for blk in range(A1_BLOCKS) for g in range(GQA_KV_HEADS) for sb in range(3)]
    colmax = []
    chunks = {}
    for step in range(len(tasks) + A1_SKEW):
        if step < len(tasks):
            blk, g, sb = tasks[step]
            kb, _, bias = operands(blk, g, sb)
            s = _dot(kb, q_ref[0, blk, g])
            if bias is not None:
                s = s + jnp.concatenate([bias] * GQA_GROUP, axis=1)
            s_ref[slot(step, s.shape[0])] = s
            colmax.append(jnp.max(s, axis=0, keepdims=True))
        if step < A1_SKEW:
            continue
        idx = step - A1_SKEW
        blk, g, sb = tasks[idx]
        _, vt, _ = operands(blk, g, sb)
        if sb == 0:
            assert idx + 2 <= step
            sink = jnp.concatenate(
                [jnp.full((1, BLOCK), sink_ref[GQA_GROUP * g + j] * LOG2E, F32) for j in range(GQA_GROUP)], axis=1)
            m = functools.reduce(jnp.maximum, [colmax[idx], colmax[idx + 1], colmax[idx + 2], sink])
            acc = None
        gate = min(step - A1_GATE_LAG, len(tasks) - 1)
        m_use = m + _zero_after(colmax[gate])
        p = jnp.exp2(s_ref[slot(idx, vt.shape[1])] - m_use).astype(BF16)
        d = _dot(vt, p)
        acc = d if acc is None else acc + d
        if sb == 2:
            l = acc[GQA_HEAD_DIM:GQA_HEAD_DIM + 1] + jnp.exp2(sink - m)
            ot = acc[0:GQA_HEAD_DIM] / l
            for p2 in range(2):
                lo = 2 * p2 * BLOCK
                pair = jnp.concatenate([ot[:, lo:lo + BLOCK], ot[:, lo + BLOCK:lo + 2 * BLOCK]], axis=0)
                chunks[(blk, 2 * g + p2)] = pair.T
            if g == GQA_KV_HEADS - 1:
                rows = slice(blk * BLOCK, (blk + 1) * BLOCK)
                o = jnp.concatenate([chunks[(blk, c)] for c in range(GQA_Q // LANES)], axis=-1)
                og = (o * sg_ref[0, rows].astype(F32)).astype(BF16)
                y = _dot(og, w_out_ref[...])
                x2 = x1_ref[0, rows] + g1_ref[0] * y
                o_ref[0, rows] = _rms(x2, fg_ref[...])


def _attn1_call(sink, qt, k, vt, ck, cvt, bias, sg, x1, g1, w_out, fg):
    b, n, d = x1.shape
    nc = ck.shape[2]
    nb = n // BLOCK
    ns = nb // A1_BLOCKS
    assert nb % A1_BLOCKS == 0 and nc == 2 * BLOCK and A1_SLOTS > A1_SKEW >= 2
    tr = A1_BLOCKS * BLOCK
    row = lambda w: pl.BlockSpec((1, tr, w), lambda bi, i: (bi, i, 0))
    prv = lambda i: jnp.maximum(i * A1_BLOCKS - 1, 0)
    nxt = lambda i: jnp.minimum((i + 1) * A1_BLOCKS, nb - 1)
    k_spec = lambda f: pl.BlockSpec((1, GQA_KV_HEADS, BLOCK, LANES), lambda bi, i: (bi, 0, f(i), 0))
    v_spec = lambda f: pl.BlockSpec((1, GQA_KV_HEADS, GQA_VT_ROWS, BLOCK), lambda bi, i: (bi, 0, 0, f(i)))
    vec_spec = pl.BlockSpec((1, 1, d), lambda bi, i: (bi, 0, 0))
    return pl.pallas_call(
        _attn1_kernel,
        grid=(b, ns),
        in_specs=[pl.BlockSpec(memory_space=pltpu.SMEM),
                  pl.BlockSpec((1, A1_BLOCKS, GQA_KV_HEADS, GQA_HEAD_DIM, GQA_GROUP * BLOCK),
                               lambda bi, i: (bi, i, 0, 0, 0)),
                  k_spec(prv), pl.BlockSpec((1, GQA_KV_HEADS, tr, LANES), lambda bi, i: (bi, 0, i, 0)), k_spec(nxt),
                  v_spec(prv), pl.BlockSpec((1, GQA_KV_HEADS, GQA_VT_ROWS, tr), lambda bi, i: (bi, 0, 0, i)),
                  v_spec(nxt),
                  pl.BlockSpec((1, GQA_KV_HEADS, nc, LANES), lambda bi, i: (bi, 0, 0, 0)),
                  pl.BlockSpec((1, GQA_KV_HEADS, GQA_VT_ROWS, nc), lambda bi, i: (bi, 0, 0, 0)),
                  _full(bias.shape), row(GQA_Q), row(d), vec_spec, _full(w_out.shape), _full(fg.shape)],
        out_specs=row(d),
        out_shape=jax.ShapeDtypeStruct((b, n, d), F32),
        scratch_shapes=[pltpu.VMEM((A1_SLOTS * 2 * BLOCK, GQA_GROUP * BLOCK), F32)],
        compiler_params=_cparams(("parallel", "arbitrary")),
        name="attn1",
    )(sink, qt, k, k, k, vt, vt, vt, ck, cvt, bias, sg, x1, g1, w_out, fg)


def _prep_even_weights(w_in, w_qb, w_kvb):
    d = w_in.shape[0]
    f_in, f_gate, q_a, kv_a, k_pe, m_gate = jnp.split(
        w_in, np.cumsum([F_WIDTH, F_WIDTH, MLA_Q_RANK, MLA_KV_RANK, MLA_ROPE]).tolist(), axis=1)
    kpe_blk = jnp.concatenate([jnp.zeros((d, MLA_NOPE), F32), k_pe, jnp.zeros((d, 32), F32)], axis=1)
    w_in_p = jnp.concatenate([f_in, f_gate, m_gate, q_a, kv_a, kpe_blk], axis=1).astype(BF16)
    w_qbt = w_qb.T.astype(BF16)
    kvb = w_kvb.reshape(MLA_KV_RANK, MLA_HEADS, MLA_NOPE + MLA_V)
    w_k = jnp.pad(kvb[:, :, :MLA_NOPE], ((0, 0), (0, 0), (0, 64))).reshape(MLA_KV_RANK, MLA_HEADS * LANES)
    w_vt = jnp.pad(kvb[:, :, MLA_NOPE:], ((0, 0), (0, 0), (0, MLA_VT_ROWS - MLA_V)))
    w_vt = w_vt.reshape(MLA_KV_RANK, MLA_HEADS * MLA_VT_ROWS).T.astype(BF16)
    vone = np.zeros((MLA_HEADS * MLA_VT_ROWS, 1), np.float32)
    vone[MLA_V::MLA_VT_ROWS] = 1.0
    return w_in_p, w_qbt, w_k.astype(BF16), w_vt, jnp.asarray(vone)


def _prep_odd_weights(w_in):
    d = w_in.shape[0]
    w_qt = w_in[:, O_Q:O_K].T.astype(BF16)
    kw = w_in[:, O_K:O_V].reshape(d, GQA_KV_HEADS, GQA_HEAD_DIM)
    w_k = jnp.pad(kw, ((0, 0), (0, 0), (0, LANES - GQA_HEAD_DIM))).reshape(d, GQA_KV_HEADS * LANES).astype(BF16)
    vw = w_in[:, O_V:O_G].reshape(d, GQA_KV_HEADS, GQA_HEAD_DIM)
    w_vt = jnp.pad(vw, ((0, 0), (0, 0), (0, GQA_VT_ROWS - GQA_HEAD_DIM)))
    w_vt = w_vt.reshape(d, GQA_KV_HEADS * GQA_VT_ROWS).T.astype(BF16)
    vone = np.zeros((GQA_KV_HEADS * GQA_VT_ROWS, 1), np.float32)
    vone[GQA_HEAD_DIM::GQA_VT_ROWS] = 1.0
    return w_qt, w_k, w_vt, jnp.asarray(vone), w_in[:, O_G:O_END].astype(BF16)


def _fft_consts(n, c):
    n1 = n // LANES
    c1, s1 = _dft_cs(n1)
    fa = np.block([[c1, s1], [-s1, c1]])
    c2, s2 = _dft_cs(LANES)
    fcs = np.concatenate([c2, s2], axis=1) / math.sqrt(n * F_GROUP_DIM)
    fc = np.concatenate([c2, -s2], axis=1)
    r = np.arange(LANES, dtype=np.int64)[:, None]
    k1 = np.arange(n1, dtype=np.int64)[None, :]
    ang = 2.0 * np.pi * ((r * k1) % n).astype(np.float64) / n
    tw = np.stack([np.cos(ang), np.sin(ang)])
    tw = np.broadcast_to(tw[..., None], (2, LANES, n1, LANES))
    cn, sn = _dft_cs(c)
    fn = np.concatenate([cn, sn], axis=1)
    as32 = lambda a: jnp.asarray(np.ascontiguousarray(a, dtype=np.float32))
    idx = np.arange(FFT_TILE * FFT_TILE)
    p16 = np.zeros((FFT_TILE * FFT_TILE,) * 2)
    p16[(idx % FFT_TILE) * FFT_TILE + idx // FFT_TILE, idx] = 1.0
    return (as32(fa).astype(BF16), as32(fcs).astype(BF16), as32(fc).astype(BF16), as32(tw), as32(fn).astype(BF16),
            as32(p16).astype(BF16))


def _pick_tile(n, pref):
    t = pref
    while n % t:
        t //= 2
    return t


def kernel(x, c, ctx, c_ctx, w_mod, b_mod, norm_g, e_w_in, e_q_norm, e_w_qb, e_kv_norm, e_w_kvb, e_w_out,
           o_w_in, o_sink, o_w_out, final_g):
    b, n, d = x.shape
    nc = ctx.shape[1]
    assert d == D_MODEL and n % (LANES * FFT_TILE) == 0 and nc % LANES == 0 and b <= 7

    svec = jnp.zeros((8, d), F32).at[:b].set(c).at[b].set(c_ctx)
    mod = _mod_call(svec, w_mod, b_mod)
    sh, sc, gt = mod[:, :, :d], mod[:, :, d:2 * d], mod[:, :, 2 * d:]
    lat = lambda t, l: t[l, :b].reshape(b, 1, d)
    cvec = lambda t, l: t[l, b].reshape(1, d)
    ng0, ng1 = norm_g[0].reshape(1, d), norm_g[1].reshape(1, d)

    cos_m, sin_m = _axial_rope_tables(n, MLA_ROPE)
    qs = MLA_SCALE * LOG2E
    tqt0 = _const(np.stack([cos_m.T, sin_m.T]) * qs)
    tk0 = _mla_rope_tab(cos_m, sin_m)
    one_c, zero_c = np.ones((nc, MLA_ROPE // 2)), np.zeros((nc, MLA_ROPE // 2))
    tqt0c = _const(np.stack([one_c.T, zero_c.T]) * qs)
    tk0c = _mla_rope_tab(one_c, zero_c)
    cos_g, sin_g = _axial_rope_tables(n, GQA_HEAD_DIM)
    tqt1 = _const(np.stack([cos_g.T, sin_g.T]) * (GQA_SCALE * LOG2E))
    tk1 = _gqa_rope_tab(cos_g, sin_g)
    fa, fcs, fc, tw, fn, p16 = _fft_consts(n, nc)
    bias1 = _band_bias(nc)

    w_in0, w_qbt0, w_k0, w_vt0, vone = _prep_even_weights(e_w_in[0], e_w_qb[0], e_w_kvb[0])
    qn0 = e_q_norm[0].reshape(1, MLA_Q_RANK)
    kvn0 = e_kv_norm[0].reshape(1, MLA_KV_RANK)
    w_out0 = e_w_out[0].astype(BF16)
    w_qt1, w_k1, w_vt1, vone1, w_g1 = _prep_odd_weights(o_w_in[0])
    w_out1 = o_w_out[0].astype(BF16)

    kc, vct, ck1, cvt1 = _ctx_call(
        ctx, (ng0, cvec(sc, 0), cvec(sh, 0), cvec(gt, 0)), (ng1, cvec(sc, 1), cvec(sh, 1)),
        w_in0, qn0, w_qbt0, kvn0, w_k0, w_vt0, vone, tqt0c, tk0c, fc, fn, w_out0, w_k1, w_vt1, vone1)

    tm = _pick_tile(n, 512)
    qt0, k0, vt0, zr, zi, gf, gm = _front0_call(
        x, ng0, lat(sc, 0), lat(sh, 0), w_in0, qn0, w_qbt0, kvn0, w_k0, w_vt0, vone, tqt0, tk0, fc, tm)

    ag = _attn0_call(qt0, kc, vct, k0, vt0, gm, _pick_tile(n, 512))

    n1 = n // LANES
    tr, ti = _fft_a_call(zr.reshape(b, n1, LANES, F_WIDTH), zi.reshape(b, n1, LANES, F_WIDTH), p16, fa, tw)
    fmg = _fft_c_call(tr, ti, p16, fcs, gf.reshape(b, LANES, n1, F_WIDTH))
    fmg = fmg.reshape(b, n, F_WIDTH)

    tm1 = _pick_tile(n, 512)
    x1, qt1, k1, vt1, sg = _mid_call(fmg, ag, x, lat(gt, 0), w_out0, ng1, lat(sc, 1), lat(sh, 1),
                                     w_qt1, w_k1, w_vt1, vone1, w_g1, tqt1, tk1, tm1)

    return _attn1_call(o_sink[0], qt1, k1, vt1, ck1, cvt1, bias1, sg, x1, lat(gt, 1), w_out1,
                       final_g.reshape(1, d))
```

```python
import functools
import math

import numpy as np
import jax
import jax.numpy as jnp
from jax import lax
from jax.experimental import pallas as pl
from jax.experimental.pallas import tpu as pltpu

F32 = jnp.float32
BF16 = jnp.bfloat16

D_MODEL = 1024
GRID_W = 64
EPS = 1e-6
ROPE_BASE = 10000.0
LANES = 128

F_GROUPS = 4
F_GROUP_DIM = 128
F_WIDTH = F_GROUPS * F_GROUP_DIM
FFT_TILE = 16

MLA_HEADS = 8
MLA_NOPE = 64
MLA_ROPE = 32
MLA_V = 64
MLA_Q_RANK = 384
MLA_KV_RANK = 256
MLA_WIDTH = MLA_HEADS * MLA_V
MLA_SCALE = 1.0 / math.sqrt(MLA_NOPE + MLA_ROPE)
LOG2E = math.log2(math.e)
MLA_VT_ROWS = 80
PV_KEYS = 256
MLA_TQ = 256
SOFTMAX_GROUP = 2
PIPE_SKEW = 3
SCORE_SLOTS = 4
GATE_LAG = 2

GQA_HEADS = 16
GQA_KV_HEADS = 4
GQA_GROUP = GQA_HEADS // GQA_KV_HEADS
GQA_HEAD_DIM = 64
WINDOW = 128
BLOCK = 128
GQA_Q = GQA_HEADS * GQA_HEAD_DIM
GQA_KV = GQA_KV_HEADS * GQA_HEAD_DIM
GQA_SCALE = 1.0 / math.sqrt(GQA_HEAD_DIM)
GQA_VT_ROWS = 80
A1_BLOCKS = 8
A1_SKEW = 5
A1_SLOTS = 6
A1_GATE_LAG = 3

E_FIN, E_FGATE, E_MGATE, E_QA, E_KVA, E_KPE, E_END = 0, 512, 1024, 1536, 1920, 2176, 2304
O_Q, O_K, O_V, O_G, O_END = 0, 1024, 1280, 1536, 2560

NEG_BIG = -1e30
VMEM_LIMIT = 56 * 1024 * 1024


def _cparams(sem):
    return pltpu.CompilerParams(dimension_semantics=sem, vmem_limit_bytes=VMEM_LIMIT)


def _dot(a, b):
    return jnp.dot(a, b, preferred_element_type=F32)


def _dot_nt(a, b):
    return lax.dot_general(a, b, (((1,), (1,)), ((), ())), preferred_element_type=F32)


def _rms(x, g):
    return x * lax.rsqrt(jnp.mean(x * x, axis=-1, keepdims=True) + EPS) * g


def _silu(x):
    return x * jax.nn.sigmoid(x)


def _rope(x, tab_ref, shift):
    up = pltpu.roll(x, LANES - shift, 1)
    dn = pltpu.roll(x, shift, 1)
    return x * tab_ref[0] + up * tab_ref[1] + dn * tab_ref[2]


def _full(shape):
    nd = len(shape)
    return pl.BlockSpec(shape, lambda *_: (0,) * nd)


def _axial_rope_tables(n, rot_dim):
    rows = n // GRID_W
    row = np.repeat(np.arange(rows), GRID_W).astype(np.float64)
    col = np.tile(np.arange(GRID_W), rows).astype(np.float64)
    nf = rot_dim // 4
    inv = (np.float32(ROPE_BASE) ** (-np.arange(nf, dtype=np.float32) / np.float32(nf))).astype(np.float64)
    ang = np.concatenate([row[:, None] * inv, col[:, None] * inv], axis=-1)
    return np.cos(ang), np.sin(ang)


def _mla_rope_tab(cos, sin):
    n = cos.shape[0]
    z = lambda w: np.zeros((n, w))
    cf = np.concatenate([np.ones((n, MLA_NOPE)), cos, cos, z(32)], axis=-1)
    s1 = np.concatenate([z(MLA_NOPE), -sin, z(16), z(32)], axis=-1)
    s2 = np.concatenate([z(MLA_NOPE), z(16), sin, z(32)], axis=-1)
    return _const(np.stack([cf, s1, s2]))


def _gqa_rope_tab(cos, sin):
    z = np.zeros_like(sin)
    cf = np.concatenate([cos, cos, z, z], axis=-1)
    s1 = np.concatenate([-sin, z, z, z], axis=-1)
    s2 = np.concatenate([z, sin, z, z], axis=-1)
    return _const(np.stack([cf, s1, s2]))


def _const(a):
    return jnp.asarray(np.ascontiguousarray(a, dtype=np.float32))


def _dft_cs(n):
    idx = np.arange(n, dtype=np.int64)
    ang = 2.0 * np.pi * ((idx[:, None] * idx[None, :]) % n).astype(np.float64) / n
    return np.cos(ang), np.sin(ang)


def _mod_kernel(s_ref, w_ref, b_ref, o_ref):
    s = _silu(s_ref[...])
    o_ref[0] = _dot(s.astype(BF16), w_ref[0].astype(BF16)) + b_ref[0]


def _mod_call(svec, w_mod, b_mod):
    depth, d, d3 = w_mod.shape
    tn = 768
    return pl.pallas_call(
        _mod_kernel,
        grid=(depth, d3 // tn),
        in_specs=[
            pl.BlockSpec((8, d), lambda l, j: (0, 0)),
            pl.BlockSpec((1, d, tn), lambda l, j: (l, 0, j)),
            pl.BlockSpec((1, 1, tn), lambda l, j: (l, 0, j)),
        ],
        out_specs=pl.BlockSpec((1, 8, tn), lambda l, j: (l, 0, j)),
        out_shape=jax.ShapeDtypeStruct((depth, 8, d3), F32),
        compiler_params=_cparams(("arbitrary", "arbitrary")),
        name="mod",
    )(svec, w_mod, b_mod.reshape(depth, 1, d3))


def _even_front(x, ng, sc, sh, w_in_ref, qn, w_qbt_ref, kvn, w_k_ref, w_vt_ref, vone, tqt_ref, tk_ref, fc_ref):
    m = x.shape[0]
    h = (_rms(x, ng) * (1.0 + sc) + sh).astype(BF16)
    f_in = _dot(h, w_in_ref[:, E_FIN:E_FGATE])
    f_gate = _dot(h, w_in_ref[:, E_FGATE:E_MGATE])
    m_gate = _dot(h, w_in_ref[:, E_MGATE:E_QA])
    low_rank = _dot(h, w_in_ref[:, E_QA:E_END])
    q_a = low_rank[:, 0:E_KVA - E_QA]
    kv_a = low_rank[:, E_KVA - E_QA:E_KPE - E_QA]
    kpe = low_rank[:, E_KPE - E_QA:E_END - E_QA]
    qh = _rms(q_a, qn).astype(BF16)
    ch = _rms(kv_a, kvn).astype(BF16)
    kpe_r = _rope(kpe, tk_ref, 16)
    qt = _dot_nt(w_qbt_ref[...], qh)
    k_all = _dot(ch, w_k_ref[...])
    cos = tqt_ref[0]
    sin = tqt_ref[1]
    pad = jnp.zeros((LANES - MLA_NOPE - MLA_ROPE, m), F32)
    qts, ks = [], []
    for hd in range(MLA_HEADS):
        lo = hd * LANES
        ql = hd * (MLA_NOPE + MLA_ROPE)
        x1 = qt[ql + MLA_NOPE:ql + MLA_NOPE + 16]
        x2 = qt[ql + MLA_NOPE + 16:ql + MLA_NOPE + 32]
        qts.append(jnp.concatenate(
            [qt[ql:ql + MLA_NOPE] * (MLA_SCALE * LOG2E), x1 * cos - x2 * sin, x2 * cos + x1 * sin, pad], axis=0))
        ks.append(k_all[:, lo:lo + LANES] + kpe_r)
    vt = _dot_nt(w_vt_ref[...], ch) + vone
    zr, zi = [], []
    fb = f_in.astype(BF16)
    for g in range(F_GROUPS):
        z = _dot(fb[:, g * LANES:(g + 1) * LANES], fc_ref[...])
        zr.append(z[:, :LANES])
        zi.append(z[:, LANES:])
    return qts, ks, vt, zr, zi, f_gate, m_gate


def _front0_kernel(x_ref, ng_ref, sc_ref, sh_ref, w_in_ref, qn_ref, w_qbt_ref, kvn_ref, w_k_ref, w_vt_ref,
                   vone_ref, tqt_ref, tk_ref, fc_ref, q_ref, k_ref, v_ref, zr_ref, zi_ref, gf_ref, gm_ref):
    qts, ks, vt, zr, zi, f_gate, m_gate = _even_front(
        x_ref[0], ng_ref[...], sc_ref[0], sh_ref[0], w_in_ref, qn_ref[...], w_qbt_ref, kvn_ref[...],
        w_k_ref, w_vt_ref, vone_ref[...], tqt_ref, tk_ref, fc_ref)
    for hd in range(MLA_HEADS):
        q_ref[0, hd] = qts[hd].astype(BF16)
        k_ref[0, hd] = ks[hd].astype(BF16)
        v_ref[0, hd, 0] = vt[hd * MLA_VT_ROWS:(hd + 1) * MLA_VT_ROWS].astype(BF16)
    for g in range(F_GROUPS):
        zr_ref[0, :, g * LANES:(g + 1) * LANES] = zr[g].astype(BF16)
        zi_ref[0, :, g * LANES:(g + 1) * LANES] = zi[g].astype(BF16)
    gf_ref[0] = _silu(f_gate).astype(BF16)
    gm_ref[0] = _silu(m_gate).astype(BF16)


def _front0_call(x, ng, sc, sh, w_in, qn, w_qbt, kvn, w_k, w_vt, vone, tqt, tk, fc, tm):
    b, n, d = x.shape
    fw = jax.ShapeDtypeStruct((b, n, F_WIDTH), BF16)
    row_spec = pl.BlockSpec((1, tm, F_WIDTH), lambda bi, i: (bi, i, 0))
    vec_spec = pl.BlockSpec((1, 1, d), lambda bi, i: (bi, 0, 0))
    return pl.pallas_call(
        _front0_kernel,
        grid=(b, n // tm),
        in_specs=[
            pl.BlockSpec((1, tm, d), lambda bi, i: (bi, i, 0)),
            _full(ng.shape), vec_spec, vec_spec,
            _full(w_in.shape), _full(qn.shape), _full(w_qbt.shape), _full(kvn.shape), _full(w_k.shape),
            _full(w_vt.shape), _full(vone.shape),
            pl.BlockSpec((2, 16, tm), lambda bi, i: (0, 0, i)),
            pl.BlockSpec((3, tm, LANES), lambda bi, i: (0, i, 0)),
            _full(fc.shape),
        ],
        out_specs=[
            pl.BlockSpec((1, MLA_HEADS, LANES, tm), lambda bi, i: (bi, 0, 0, i)),
            pl.BlockSpec((1, MLA_HEADS, tm, LANES), lambda bi, i: (bi, 0, i, 0)),
            pl.BlockSpec((1, MLA_HEADS, 1, MLA_VT_ROWS, tm), lambda bi, i: (bi, 0, i, 0, 0)),
            row_spec, row_spec, row_spec, row_spec],
        out_shape=[
            jax.ShapeDtypeStruct((b, MLA_HEADS, LANES, n), BF16),
            jax.ShapeDtypeStruct((b, MLA_HEADS, n, LANES), BF16),
            jax.ShapeDtypeStruct((b, MLA_HEADS, n // tm, MLA_VT_ROWS, tm), BF16),
            fw, fw, fw, fw],
        compiler_params=_cparams(("parallel", "arbitrary")),
        name="front0",
    )(x, ng, sc, sh, w_in, qn, w_qbt, kvn, w_k, w_vt, vone, tqt, tk, fc)


def _zero_after(x):
    u = lax.bitcast_convert_type(x, jnp.uint32)
    z = lax.shift_right_logical(lax.shift_right_logical(u, jnp.uint32(16)), jnp.uint32(16))
    return lax.bitcast_convert_type(z, F32)


def _mla_chunk(qt, k, vt, m, acc):
    s = _dot(k, qt)
    m_new = jnp.maximum(m, jnp.max(s, axis=0, keepdims=True))
    alpha = jnp.exp2(m - m_new)
    p = jnp.exp2(s - m_new).astype(BF16)
    return m_new, alpha * acc + _dot(vt, p)


def _mla_pair_out(acc0, acc1):
    o0 = acc0[0:MLA_V] / acc0[MLA_V:MLA_V + 1]
    o1 = acc1[0:MLA_V] / acc1[MLA_V:MLA_V + 1]
    return jnp.concatenate([o0, o1], axis=0).T


def _ctx_kernel(ctx_scale, x_ref, ng0_ref, sc0_ref, sh0_ref, g0_ref, ng1_ref, sc1_ref, sh1_ref,
                w_in_ref, qn_ref, w_qbt_ref, kvn_ref, w_k_ref, w_vt_ref, vone_ref, tqt_ref, tk_ref, fc_ref,
                fn_ref, w_out_ref, w_k1_ref, w_vt1_ref, vone1_ref,
                kc_ref, vc_ref, ck1_ref, cv1_ref):
    x = x_ref[0]
    c = x.shape[0]
    qts, ks, vt, zr, zi, f_gate, m_gate = _even_front(
        x, ng0_ref[...], sc0_ref[...], sh0_ref[...], w_in_ref, qn_ref[...], w_qbt_ref, kvn_ref[...],
        w_k_ref, w_vt_ref, vone_ref[...], tqt_ref, tk_ref, fc_ref)
    pairs = []
    for hp in range(MLA_HEADS // 2):
        accs = []
        for e in range(2):
            hd = 2 * hp + e
            kb = ks[hd].astype(BF16)
            vtb = vt[hd * MLA_VT_ROWS:(hd + 1) * MLA_VT_ROWS].astype(BF16)
            kc_ref[0, hd] = kb
            vc_ref[0, hd, 0] = vtb
            m0 = jnp.full((1, c), -jnp.inf, F32)
            acc0 = jnp.zeros((MLA_VT_ROWS, c), F32)
            accs.append(_mla_chunk(qts[hd].astype(BF16), kb, vtb, m0, acc0)[1])
        pairs.append(_mla_pair_out(accs[0], accs[1]))
    a = jnp.concatenate(pairs, axis=-1)
    z = jnp.concatenate([jnp.concatenate(zr, axis=-1), jnp.concatenate(zi, axis=-1)], axis=0).astype(BF16)
    fm = _dot(fn_ref[...], z) * ctx_scale
    yf = (fm * _silu(f_gate)).astype(BF16)
    ya = (a * _silu(m_gate)).astype(BF16)
    y = _dot(yf, w_out_ref[0:F_WIDTH, :]) + _dot(ya, w_out_ref[F_WIDTH:, :])
    x1 = x + g0_ref[...] * y
    h1 = (_rms(x1, ng1_ref[...]) * (1.0 + sc1_ref[...]) + sh1_ref[...]).astype(BF16)
    vt1 = _dot_nt(w_vt1_ref[...], h1) + vone1_ref[...]
    for g in range(GQA_KV_HEADS):
        ck1_ref[0, g] = _dot(h1, w_k1_ref[:, g * LANES:(g + 1) * LANES]).astype(BF16)
        cv1_ref[0, g] = vt1[g * GQA_VT_ROWS:(g + 1) * GQA_VT_ROWS].astype(BF16)


def _ctx_call(ctx, vecs0, vecs1, w_in, qn, w_qbt, kvn, w_k, w_vt, vone, tqt, tk, fc, fn, w_out, w_k1, w_vt1,
              vone1):
    b, c, d = ctx.shape
    ng0, sc0, sh0, g0 = vecs0
    ng1, sc1, sh1 = vecs1
    ctx_scale = 1.0 / math.sqrt(c * F_GROUP_DIM)
    consts = [ng0, sc0, sh0, g0, ng1, sc1, sh1, w_in, qn, w_qbt, kvn, w_k, w_vt, vone, tqt, tk, fc, fn, w_out,
              w_k1, w_vt1, vone1]
    return pl.pallas_call(
        functools.partial(_ctx_kernel, ctx_scale),
        grid=(b,),
        in_specs=[pl.BlockSpec((1, c, d), lambda bi: (bi, 0, 0))] + [_full(a.shape) for a in consts],
        out_specs=[pl.BlockSpec((1, MLA_HEADS, c, LANES), lambda bi: (bi, 0, 0, 0)),
                   pl.BlockSpec((1, MLA_HEADS, 1, MLA_VT_ROWS, c), lambda bi: (bi, 0, 0, 0, 0)),
                   pl.BlockSpec((1, GQA_KV_HEADS, c, LANES), lambda bi: (bi, 0, 0, 0)),
                   pl.BlockSpec((1, GQA_KV_HEADS, GQA_VT_ROWS, c), lambda bi: (bi, 0, 0, 0))],
        out_shape=[jax.ShapeDtypeStruct((b, MLA_HEADS, c, LANES), BF16),
                   jax.ShapeDtypeStruct((b, MLA_HEADS, 1, MLA_VT_ROWS, c), BF16),
                   jax.ShapeDtypeStruct((b, GQA_KV_HEADS, c, LANES), BF16),
                   jax.ShapeDtypeStruct((b, GQA_KV_HEADS, GQA_VT_ROWS, c), BF16)],
        compiler_params=_cparams(("arbitrary",)),
        name="ctx",
    )(ctx, *consts)


def _attn0_kernel(qt_ref, kc_ref, vct_ref, kl_ref, vlt_ref, gm_ref, o_ref, s_ref):
    tq = qt_ref.shape[-1]
    nch, _, tk = vlt_ref.shape[2:]
    nc = kc_ref.shape[2]
    n_ctx = nc // PV_KEYS
    n_sub = n_ctx + nch * tk // PV_KEYS
    group = [0] * n_ctx + [1 + j // SOFTMAX_GROUP for j in range(n_sub - n_ctx)]
    first = [t for t in range(n_sub) if t == 0 or group[t] != group[t - 1]]
    last = [t for t in range(n_sub) if t == n_sub - 1 or group[t] != group[t + 1]]

    def keys(e, t):
        if t < n_ctx:
            return kc_ref[0, e, t * PV_KEYS:(t + 1) * PV_KEYS, :]
        lo = (t - n_ctx) * PV_KEYS
        return kl_ref[0, e, lo:lo + PV_KEYS, :]

    def values_t(e, t):
        if t < n_ctx:
            return vct_ref[0, e, 0, :, t * PV_KEYS:(t + 1) * PV_KEYS]
        j, off = divmod((t - n_ctx) * PV_KEYS, tk)
        return vlt_ref[0, e, j, :, off:off + PV_KEYS]

    def slot(i):
        r = i % SCORE_SLOTS
        return slice(r * PV_KEYS, (r + 1) * PV_KEYS)

    tasks = [(qi, t) for qi in range(tq // MLA_TQ) for t in range(n_sub)]
    colmax = []
    for step in range(len(tasks) + PIPE_SKEW):
        if step < len(tasks):
            qi, t = tasks[step]
            cols = slice(qi * MLA_TQ, (qi + 1) * MLA_TQ)
            cm = []
            for e in range(2):
                s = _dot(keys(e, t), qt_ref[0, e, :, cols])
                s_ref[e, slot(step)] = s
                cm.append(jnp.max(s, axis=0, keepdims=True))
            colmax.append(cm)
        if step < PIPE_SKEW:
            continue
        qi, u = tasks[step - PIPE_SKEW]
        if u == 0:
            m = [jnp.full((1, MLA_TQ), -jnp.inf, F32) for _ in range(2)]
            acc = [jnp.zeros((MLA_VT_ROWS, MLA_TQ), F32) for _ in range(2)]
        if u in first:
            members = [qi * n_sub + t for t in range(n_sub) if group[t] == group[u]]
            assert members[-1] <= step
            m_new, alpha = [], []
            for e in range(2):
                mc = functools.reduce(jnp.maximum, [colmax[i][e] for i in members])
                m_new.append(jnp.maximum(m[e], mc))
                alpha.append(jnp.exp2(m[e] - m_new[-1]))
            pv = [None, None]
        gate = min(step - GATE_LAG, len(tasks) - 1)
        for e in range(2):
            m_use = m_new[e] + _zero_after(colmax[gate][e])
            p = jnp.exp2(s_ref[e, slot(step - PIPE_SKEW)] - m_use).astype(BF16)
            d = _dot(values_t(e, u), p)
            pv[e] = d if pv[e] is None else pv[e] + d
        if u in last:
            for e in range(2):
                acc[e] = alpha[e] * acc[e] + pv[e]
                m[e] = m_new[e]
        if u == n_sub - 1:
            rows = slice(qi * MLA_TQ, (qi + 1) * MLA_TQ)
            o_ref[0, rows] = (_mla_pair_out(acc[0], acc[1]) * gm_ref[0, rows].astype(F32)).astype(BF16)


def _attn0_call(qt, kc, vct, kl, vlt, gm, tq):
    b, h, _, n = qt.shape
    c = kc.shape[2]
    nch, _, tk = vlt.shape[2:]
    assert c % PV_KEYS == 0 and tk % PV_KEYS == 0 and SCORE_SLOTS > PIPE_SKEW >= SOFTMAX_GROUP - 1
    assert tq % MLA_TQ == 0
    o_spec = pl.BlockSpec((1, tq, LANES), lambda bi, hp, i: (bi, i, hp))
    return pl.pallas_call(
        _attn0_kernel,
        grid=(b, h // 2, n // tq),
        in_specs=[
            pl.BlockSpec((1, 2, LANES, tq), lambda bi, hp, i: (bi, hp, 0, i)),
            pl.BlockSpec((1, 2, c, LANES), lambda bi, hp, i: (bi, hp, 0, 0)),
            pl.BlockSpec((1, 2, 1, MLA_VT_ROWS, c), lambda bi, hp, i: (bi, hp, 0, 0, 0)),
            pl.BlockSpec((1, 2, n, LANES), lambda bi, hp, i: (bi, hp, 0, 0)),
            pl.BlockSpec((1, 2, nch, MLA_VT_ROWS, tk), lambda bi, hp, i: (bi, hp, 0, 0, 0)),
            o_spec],
        out_specs=o_spec,
        out_shape=jax.ShapeDtypeStruct((b, n, MLA_WIDTH), BF16),
        scratch_shapes=[pltpu.VMEM((2, SCORE_SLOTS * PV_KEYS, MLA_TQ), F32)],
        compiler_params=_cparams(("parallel", "parallel", "arbitrary")),
        name="attn0",
    )(qt, kc, vct, kl, vlt, gm)


def _swap16(p16, blocks):
    ys = [_dot(p16, blk).astype(BF16) for blk in blocks]
    return [jnp.concatenate([y[t * FFT_TILE:(t + 1) * FFT_TILE] for y in ys], axis=0) for t in range(FFT_TILE)]


def _fft_a_kernel(n1, zr_ref, zi_ref, p16_ref, fa_ref, tw_ref, tr_ref, ti_ref):
    rows = n1 * FFT_TILE
    z = jnp.concatenate([zr_ref[0].reshape(rows, F_WIDTH), zi_ref[0].reshape(rows, F_WIDTH)], axis=1)
    grp = FFT_TILE * FFT_TILE
    per_r = _swap16(p16_ref[...], [z[j * grp:(j + 1) * grp] for j in range(rows // grp)])
    for r in range(FFT_TILE):
        zz = per_r[r]
        a = _dot(fa_ref[...], jnp.concatenate([zz[:, :F_WIDTH], zz[:, F_WIDTH:]], axis=0))
        c = tw_ref[0, r]
        s = tw_ref[1, r]
        for g in range(F_GROUPS):
            ar = a[0:n1, g * LANES:(g + 1) * LANES]
            ai = a[n1:2 * n1, g * LANES:(g + 1) * LANES]
            tr_ref[0, r, :, g * LANES:(g + 1) * LANES] = (ar * c + ai * s).astype(BF16)
            ti_ref[0, r, :, g * LANES:(g + 1) * LANES] = (ai * c - ar * s).astype(BF16)


def _fft_a_call(zr, zi, p16, fa, tw):
    b, n1, r, w = zr.shape
    in_spec = pl.BlockSpec((1, n1, FFT_TILE, w), lambda bi, j: (bi, 0, j, 0))
    out_spec = pl.BlockSpec((1, FFT_TILE, n1, w), lambda bi, j: (bi, j, 0, 0))
    out = jax.ShapeDtypeStruct((b, r, n1, w), BF16)
    return pl.pallas_call(
        functools.partial(_fft_a_kernel, n1),
        grid=(b, r // FFT_TILE),
        in_specs=[in_spec, in_spec, _full(p16.shape), _full(fa.shape),
                  pl.BlockSpec((2, FFT_TILE, n1, LANES), lambda bi, j: (0, j, 0, 0))],
        out_specs=[out_spec, out_spec],
        out_shape=[out, out],
        compiler_params=_cparams(("parallel", "arbitrary")),
        name="fft_a",
    )(zr, zi, p16, fa, tw)


def _fft_c_kernel(tr_ref, ti_ref, p16_ref, fcs_ref, gf_ref, o_ref):
    nr = tr_ref.shape[1]
    rows = nr * FFT_TILE
    grp = FFT_TILE * FFT_TILE
    t = jnp.concatenate([tr_ref[0].reshape(rows, F_WIDTH), ti_ref[0].reshape(rows, F_WIDTH)], axis=1)
    per_k = _swap16(p16_ref[...], [t[j * grp:(j + 1) * grp] for j in range(rows // grp)])
    outs = []
    for k in range(FFT_TILE):
        tt = per_k[k]
        rhs = jnp.concatenate([tt[:, :F_WIDTH], tt[:, F_WIDTH:]], axis=0)
        outs.append(_dot(fcs_ref[...], rhs).astype(BF16))
    gf = gf_ref[0].reshape(rows, F_WIDTH)
    for j in range(nr // FFT_TILE):
        blk = jnp.concatenate([o[j * FFT_TILE:(j + 1) * FFT_TILE] for o in outs], axis=0)
        y = _dot(p16_ref[...], blk)
        o_ref[0, j * FFT_TILE:(j + 1) * FFT_TILE] = (
            y * gf[j * grp:(j + 1) * grp].astype(F32)).astype(BF16).reshape(FFT_TILE, FFT_TILE, F_WIDTH)


def _fft_c_call(tr, ti, p16, fcs, gf):
    b, r, n1, w = tr.shape
    spec = pl.BlockSpec((1, r, FFT_TILE, w), lambda bi, i: (bi, 0, i, 0))
    return pl.pallas_call(
        _fft_c_kernel,
        grid=(b, n1 // FFT_TILE),
        in_specs=[spec, spec, _full(p16.shape), _full(fcs.shape), spec],
        out_specs=spec,
        out_shape=jax.ShapeDtypeStruct((b, r, n1, w), BF16),
        compiler_params=_cparams(("parallel", "arbitrary")),
        name="fft_c",
    )(tr, ti, p16, fcs, gf)


def _mid_kernel(fm_ref, ag_ref, x_ref, g0_ref, w_out_ref, ng_ref, sc_ref, sh_ref, w_qt_ref, w_k_ref, w_vt_ref,
                vone_ref, w_g_ref, tqt_ref, tk_ref, x1_ref, q_ref, k_ref, v_ref, sg_ref):
    tm = x_ref.shape[1]
    y = _dot(fm_ref[0], w_out_ref[0:F_WIDTH, :]) + _dot(ag_ref[0], w_out_ref[F_WIDTH:, :])
    x1 = x_ref[0] + g0_ref[0] * y
    x1_ref[0] = x1
    h = (_rms(x1, ng_ref[...]) * (1.0 + sc_ref[0]) + sh_ref[0]).astype(BF16)
    qt = _dot_nt(w_qt_ref[...], h)
    cos = tqt_ref[0]
    sin = tqt_ref[1]
    half = GQA_HEAD_DIM // 2
    for hd in range(GQA_HEADS):
        lo = hd * GQA_HEAD_DIM
        x1r = qt[lo:lo + half]
        x2r = qt[lo + half:lo + GQA_HEAD_DIM]
        qh = jnp.concatenate([x1r * cos - x2r * sin, x2r * cos + x1r * sin], axis=0).astype(BF16)
        g, j = divmod(hd, GQA_GROUP)
        for blk in range(tm // BLOCK):
            q_ref[0, blk, g, :, j * BLOCK:(j + 1) * BLOCK] = qh[:, blk * BLOCK:(blk + 1) * BLOCK]
    vt = _dot_nt(w_vt_ref[...], h) + vone_ref[...]
    k_all = _dot(h, w_k_ref[...])
    for g in range(GQA_KV_HEADS):
        k_ref[0, g] = _rope(k_all[:, g * LANES:(g + 1) * LANES], tk_ref, 32).astype(BF16)
        v_ref[0, g] = vt[g * GQA_VT_ROWS:(g + 1) * GQA_VT_ROWS].astype(BF16)
    sg_ref[0] = _silu(_dot(h, w_g_ref[...])).astype(BF16)


def _mid_call(fm, ag, x, g0, w_out, ng, sc, sh, w_qt, w_k, w_vt, vone, w_g, tqt, tk, tm):
    b, n, d = x.shape
    row = lambda w: pl.BlockSpec((1, tm, w), lambda bi, i: (bi, i, 0))
    vec_spec = pl.BlockSpec((1, 1, d), lambda bi, i: (bi, 0, 0))
    nblk = tm // BLOCK
    return pl.pallas_call(
        _mid_kernel,
        grid=(b, n // tm),
        in_specs=[row(F_WIDTH), row(MLA_WIDTH), row(d), vec_spec, _full(w_out.shape), _full(ng.shape),
                  vec_spec, vec_spec, _full(w_qt.shape), _full(w_k.shape), _full(w_vt.shape), _full(vone.shape),
                  _full(w_g.shape),
                  pl.BlockSpec((2, GQA_HEAD_DIM // 2, tm), lambda bi, i: (0, 0, i)),
                  pl.BlockSpec((3, tm, LANES), lambda bi, i: (0, i, 0))],
        out_specs=[row(d),
                   pl.BlockSpec((1, nblk, GQA_KV_HEADS, GQA_HEAD_DIM, GQA_GROUP * BLOCK),
                                lambda bi, i: (bi, i, 0, 0, 0)),
                   pl.BlockSpec((1, GQA_KV_HEADS, tm, LANES), lambda bi, i: (bi, 0, i, 0)),
                   pl.BlockSpec((1, GQA_KV_HEADS, GQA_VT_ROWS, tm), lambda bi, i: (bi, 0, 0, i)),
                   row(GQA_Q)],
        out_shape=[jax.ShapeDtypeStruct((b, n, d), F32),
                   jax.ShapeDtypeStruct((b, n // BLOCK, GQA_KV_HEADS, GQA_HEAD_DIM, GQA_GROUP * BLOCK), BF16),
                   jax.ShapeDtypeStruct((b, GQA_KV_HEADS, n, LANES), BF16),
                   jax.ShapeDtypeStruct((b, GQA_KV_HEADS, GQA_VT_ROWS, n), BF16),
                   jax.ShapeDtypeStruct((b, n, GQA_Q), BF16)],
        compiler_params=_cparams(("parallel", "arbitrary")),
        name="mid",
    )(fm, ag, x, g0, w_out, ng, sc, sh, w_qt, w_k, w_vt, vone, w_g, tqt, tk)


def _band_bias(nc):
    col = np.arange(3 * BLOCK)[:, None]
    q = np.arange(BLOCK)[None, :]
    dist = BLOCK + q - col
    band = np.abs(dist) <= WINDOW
    variants = [band & (col >= BLOCK), band, band & (col < 2 * BLOCK)]
    out = np.zeros((3, nc + 3 * BLOCK, BLOCK), np.float32)
    for v, ok in enumerate(variants):
        out[v, nc:] = np.where(ok, 0.0, NEG_BIG)
    return jnp.asarray(out)


def _attn1_kernel(sink_ref, q_ref, kp_ref, kc_ref, kn_ref, vp_ref, vc_ref, vn_ref, ck_ref, cv_ref, bias_ref,
                  sg_ref, x1_ref, g1_ref, w_out_ref, fg_ref, o_ref, s_ref):
    i = pl.program_id(1)
    last = pl.num_programs(1) - 1
    nc = ck_ref.shape[2]
    sel = []
    for blk in range(A1_BLOCKS):
        v = 1
        if blk == 0:
            v = jnp.where(i == 0, 0, v)
        if blk == A1_BLOCKS - 1:
            v = jnp.where(i == last, 2, v)
        sel.append(v)

    def band_keys(g, blk, j):
        pos = blk + j - 1
        if pos < 0:
            return kp_ref[0, g], vp_ref[0, g]
        if pos >= A1_BLOCKS:
            return kn_ref[0, g], vn_ref[0, g]
        return (kc_ref[0, g, pos * BLOCK:(pos + 1) * BLOCK, :], vc_ref[0, g, :, pos * BLOCK:(pos + 1) * BLOCK])

    def operands(blk, g, sb):
        if sb == 0:
            return ck_ref[0, g][:, :GQA_HEAD_DIM], cv_ref[0, g], None
        if sb == 1:
            (k0, v0), (k1, v1) = band_keys(g, blk, 0), band_keys(g, blk, 1)
            return (jnp.concatenate([k0, k1], axis=0)[:, :GQA_HEAD_DIM], jnp.concatenate([v0, v1], axis=1),
                    bias_ref[sel[blk], nc:nc + 2 * BLOCK])
        k2, v2 = band_keys(g, blk, 2)
        return k2[:, :GQA_HEAD_DIM], v2, bias_ref[sel[blk], nc + 2 * BLOCK:nc + 3 * BLOCK]

    def slot(idx, rows):
        r = idx % A1_SLOTS
        return slice(r * 2 * BLOCK, r * 2 * BLOCK + rows)

    tasks = [(blk, g, sb) for blk in range(A1_BLOCKS) for g in range(GQA_KV_HEADS) for sb in range(3)]
    colmax = []
    chunks = {}
    for step in range(len(tasks) + A1_SKEW):
        if step < len(tasks):
            blk, g, sb = tasks[step]
            kb, _, bias = operands(blk, g, sb)
            s = _dot(kb, q_ref[0, blk, g])
            if bias is not None:
                s = s + jnp.concatenate([bias] * GQA_GROUP, axis=1)
            s_ref[slot(step, s.shape[0])] = s
            colmax.append(jnp.max(s, axis=0, keepdims=True))
        if step < A1_SKEW:
            continue
        idx = step - A1_SKEW
        blk, g, sb = tasks[idx]
        _, vt, _ = operands(blk, g, sb)
        if sb == 0:
            assert idx + 2 <= step
            sink = jnp.concatenate(
                [jnp.full((1, BLOCK), sink_ref[GQA_GROUP * g + j] * LOG2E, F32) for j in range(GQA_GROUP)], axis=1)
            m = functools.reduce(jnp.maximum, [colmax[idx], colmax[idx + 1], colmax[idx + 2], sink])
            acc = None
        gate = min(step - A1_GATE_LAG, len(tasks) - 1)
        m_use = m + _zero_after(colmax[gate])
        p = jnp.exp2(s_ref[slot(idx, vt.shape[1])] - m_use).astype(BF16)
        d = _dot(vt, p)
        acc = d if acc is None else acc + d
        if sb == 2:
            l = acc[GQA_HEAD_DIM:GQA_HEAD_DIM + 1] + jnp.exp2(sink - m)
            ot = acc[0:GQA_HEAD_DIM] / l
            for p2 in range(2):
                lo = 2 * p2 * BLOCK
                pair = jnp.concatenate([ot[:, lo:lo + BLOCK], ot[:, lo + BLOCK:lo + 2 * BLOCK]], axis=0)
                chunks[(blk, 2 * g + p2)] = pair.T
            if g == GQA_KV_HEADS - 1:
                rows = slice(blk * BLOCK, (blk + 1) * BLOCK)
                o = jnp.concatenate([chunks[(blk, c)] for c in range(GQA_Q // LANES)], axis=-1)
                og = (o * sg_ref[0, rows].astype(F32)).astype(BF16)
                y = _dot(og, w_out_ref[...])
                x2 = x1_ref[0, rows] + g1_ref[0] * y
                o_ref[0, rows] = _rms(x2, fg_ref[...])


def _attn1_call(sink, qt, k, vt, ck, cvt, bias, sg, x1, g1, w_out, fg):
    b, n, d = x1.shape
    nc = ck.shape[2]
    nb = n // BLOCK
    ns = nb // A1_BLOCKS
    assert nb % A1_BLOCKS == 0 and nc == 2 * BLOCK and A1_SLOTS > A1_SKEW >= 2
    tr = A1_BLOCKS * BLOCK
    row = lambda w: pl.BlockSpec((1, tr, w), lambda bi, i: (bi, i, 0))
    prv = lambda i: jnp.maximum(i * A1_BLOCKS - 1, 0)
    nxt = lambda i: jnp.minimum((i + 1) * A1_BLOCKS, nb - 1)
    k_spec = lambda f: pl.BlockSpec((1, GQA_KV_HEADS, BLOCK, LANES), lambda bi, i: (bi, 0, f(i), 0))
    v_spec = lambda f: pl.BlockSpec((1, GQA_KV_HEADS, GQA_VT_ROWS, BLOCK), lambda bi, i: (bi, 0, 0, f(i)))
    vec_spec = pl.BlockSpec((1, 1, d), lambda bi, i: (bi, 0, 0))
    return pl.pallas_call(
        _attn1_kernel,
        grid=(b, ns),
        in_specs=[pl.BlockSpec(memory_space=pltpu.SMEM),
                  pl.BlockSpec((1, A1_BLOCKS, GQA_KV_HEADS, GQA_HEAD_DIM, GQA_GROUP * BLOCK),
                               lambda bi, i: (bi, i, 0, 0, 0)),
                  k_spec(prv), pl.BlockSpec((1, GQA_KV_HEADS, tr, LANES), lambda bi, i: (bi, 0, i, 0)), k_spec(nxt),
                  v_spec(prv), pl.BlockSpec((1, GQA_KV_HEADS, GQA_VT_ROWS, tr), lambda bi, i: (bi, 0, 0, i)),
                  v_spec(nxt),
                  pl.BlockSpec((1, GQA_KV_HEADS, nc, LANES), lambda bi, i: (bi, 0, 0, 0)),
                  pl.BlockSpec((1, GQA_KV_HEADS, GQA_VT_ROWS, nc), lambda bi, i: (bi, 0, 0, 0)),
                  _full(bias.shape), row(GQA_Q), row(d), vec_spec, _full(w_out.shape), _full(fg.shape)],
        out_specs=row(d),
        out_shape=jax.ShapeDtypeStruct((b, n, d), F32),
        scratch_shapes=[pltpu.VMEM((A1_SLOTS * 2 * BLOCK, GQA_GROUP * BLOCK), F32)],
        compiler_params=_cparams(("parallel", "arbitrary")),
        name="attn1",
    )(sink, qt, k, k, k, vt, vt, vt, ck, cvt, bias, sg, x1, g1, w_out, fg)


def _prep_even_weights(w_in, w_qb, w_kvb):
    d = w_in.shape[0]
    f_in, f_gate, q_a, kv_a, k_pe, m_gate = jnp.split(
        w_in, np.cumsum([F_WIDTH, F_WIDTH, MLA_Q_RANK, MLA_KV_RANK, MLA_ROPE]).tolist(), axis=1)
    kpe_blk = jnp.concatenate([jnp.zeros((d, MLA_NOPE), F32), k_pe, jnp.zeros((d, 32), F32)], axis=1)
    w_in_p = jnp.concatenate([f_in, f_gate, m_gate, q_a, kv_a, kpe_blk], axis=1).astype(BF16)
    w_qbt = w_qb.T.astype(BF16)
    kvb = w_kvb.reshape(MLA_KV_RANK, MLA_HEADS, MLA_NOPE + MLA_V)
    w_k = jnp.pad(kvb[:, :, :MLA_NOPE], ((0, 0), (0, 0), (0, 64))).reshape(MLA_KV_RANK, MLA_HEADS * LANES)
    w_vt = jnp.pad(kvb[:, :, MLA_NOPE:], ((0, 0), (0, 0), (0, MLA_VT_ROWS - MLA_V)))
    w_vt = w_vt.reshape(MLA_KV_RANK, MLA_HEADS * MLA_VT_ROWS).T.astype(BF16)
    vone = np.zeros((MLA_HEADS * MLA_VT_ROWS, 1), np.float32)
    vone[MLA_V::MLA_VT_ROWS] = 1.0
    return w_in_p, w_qbt, w_k.astype(BF16), w_vt, jnp.asarray(vone)


def _prep_odd_weights(w_in):
    d = w_in.shape[0]
    w_qt = w_in[:, O_Q:O_K].T.astype(BF16)
    kw = w_in[:, O_K:O_V].reshape(d, GQA_KV_HEADS, GQA_HEAD_DIM)
    w_k = jnp.pad(kw, ((0, 0), (0, 0), (0, LANES - GQA_HEAD_DIM))).reshape(d, GQA_KV_HEADS * LANES).astype(BF16)
    vw = w_in[:, O_V:O_G].reshape(d, GQA_KV_HEADS, GQA_HEAD_DIM)
    w_vt = jnp.pad(vw, ((0, 0), (0, 0), (0, GQA_VT_ROWS - GQA_HEAD_DIM)))
    w_vt = w_vt.reshape(d, GQA_KV_HEADS * GQA_VT_ROWS).T.astype(BF16)
    vone = np.zeros((GQA_KV_HEADS * GQA_VT_ROWS, 1), np.float32)
    vone[GQA_HEAD_DIM::GQA_VT_ROWS] = 1.0
    return w_qt, w_k, w_vt, jnp.asarray(vone), w_in[:, O_G:O_END].astype(BF16)


def _fft_consts(n, c):
    n1 = n // LANES
    c1, s1 = _dft_cs(n1)
    fa = np.block([[c1, s1], [-s1, c1]])
    c2, s2 = _dft_cs(LANES)
    fcs = np.concatenate([c2, s2], axis=1) / math.sqrt(n * F_GROUP_DIM)
    fc = np.concatenate([c2, -s2], axis=1)
    r = np.arange(LANES, dtype=np.int64)[:, None]
    k1 = np.arange(n1, dtype=np.int64)[None, :]
    ang = 2.0 * np.pi * ((r * k1) % n).astype(np.float64) / n
    tw = np.stack([np.cos(ang), np.sin(ang)])
    tw = np.broadcast_to(tw[..., None], (2, LANES, n1, LANES))
    cn, sn = _dft_cs(c)
    fn = np.concatenate([cn, sn], axis=1)
    as32 = lambda a: jnp.asarray(np.ascontiguousarray(a, dtype=np.float32))
    idx = np.arange(FFT_TILE * FFT_TILE)
    p16 = np.zeros((FFT_TILE * FFT_TILE,) * 2)
    p16[(idx % FFT_TILE) * FFT_TILE + idx // FFT_TILE, idx] = 1.0
    return (as32(fa).astype(BF16), as32(fcs).astype(BF16), as32(fc).astype(BF16), as32(tw), as32(fn).astype(BF16),
            as32(p16).astype(BF16))


def _pick_tile(n, pref):
    t = pref
    while n % t:
        t //= 2
    return t


def kernel(x, c, ctx, c_ctx, w_mod, b_mod, norm_g, e_w_in, e_q_norm, e_w_qb, e_kv_norm, e_w_kvb, e_w_out,
           o_w_in, o_sink, o_w_out, final_g):
    b, n, d = x.shape
    nc = ctx.shape[1]
    assert d == D_MODEL and n % (LANES * FFT_TILE) == 0 and nc % LANES == 0 and b <= 7

    svec = jnp.zeros((8, d), F32).at[:b].set(c).at[b].set(c_ctx)
    mod = _mod_call(svec, w_mod, b_mod)
    sh, sc, gt = mod[:, :, :d], mod[:, :, d:2 * d], mod[:, :, 2 * d:]
    lat = lambda t, l: t[l, :b].reshape(b, 1, d)
    cvec = lambda t, l: t[l, b].reshape(1, d)
    ng0, ng1 = norm_g[0].reshape(1, d), norm_g[1].reshape(1, d)

    cos_m, sin_m = _axial_rope_tables(n, MLA_ROPE)
    qs = MLA_SCALE * LOG2E
    tqt0 = _const(np.stack([cos_m.T, sin_m.T]) * qs)
    tk0 = _mla_rope_tab(cos_m, sin_m)
    one_c, zero_c = np.ones((nc, MLA_ROPE // 2)), np.zeros((nc, MLA_ROPE // 2))
    tqt0c = _const(np.stack([one_c.T, zero_c.T]) * qs)
    tk0c = _mla_rope_tab(one_c, zero_c)
    cos_g, sin_g = _axial_rope_tables(n, GQA_HEAD_DIM)
    tqt1 = _const(np.stack([cos_g.T, sin_g.T]) * (GQA_SCALE * LOG2E))
    tk1 = _gqa_rope_tab(cos_g, sin_g)
    fa, fcs, fc, tw, fn, p16 = _fft_consts(n, nc)
    bias1 = _band_bias(nc)

    w_in0, w_qbt0, w_k0, w_vt0, vone = _prep_even_weights(e_w_in[0], e_w_qb[0], e_w_kvb[0])
    qn0 = e_q_norm[0].reshape(1, MLA_Q_RANK)
    kvn0 = e_kv_norm[0].reshape(1, MLA_KV_RANK)
    w_out0 = e_w_out[0].astype(BF16)
    w_qt1, w_k1, w_vt1, vone1, w_g1 = _prep_odd_weights(o_w_in[0])
    w_out1 = o_w_out[0].astype(BF16)

    kc, vct, ck1, cvt1 = _ctx_call(
        ctx, (ng0, cvec(sc, 0), cvec(sh, 0), cvec(gt, 0)), (ng1, cvec(sc, 1), cvec(sh, 1)),
        w_in0, qn0, w_qbt0, kvn0, w_k0, w_vt0, vone, tqt0c, tk0c, fc, fn, w_out0, w_k1, w_vt1, vone1)

    tm = _pick_tile(n, 512)
    qt0, k0, vt0, zr, zi, gf, gm = _front0_call(
        x, ng0, lat(sc, 0), lat(sh, 0), w_in0, qn0, w_qbt0, kvn0, w_k0, w_vt0, vone, tqt0, tk0, fc, tm)

    ag = _attn0_call(qt0, kc, vct, k0, vt0, gm, _pick_tile(n, 512))

    n1 = n // LANES
    tr, ti = _fft_a_call(zr.reshape(b, n1, LANES, F_WIDTH), zi.reshape(b, n1, LANES, F_WIDTH), p16, fa, tw)
    fmg = _fft_c_call(tr, ti, p16, fcs, gf.reshape(b, LANES, n1, F_WIDTH))
    fmg = fmg.reshape(b, n, F_WIDTH)

    tm1 = _pick_tile(n, 512)
    x1, qt1, k1, vt1, sg = _mid_call(fmg, ag, x, lat(gt, 0), w_out0, ng1, lat(sc, 1), lat(sh, 1),
                                     w_qt1, w_k1, w_vt1, vone1, w_g1, tqt1, tk1, tm1)

    return _attn1_call(o_sink[0], qt1, k1, vt1, ck1, cvt1, bias1, sg, x1, lat(gt, 1), w_out1,
                       final_g.reshape(1, d))
```

```python
import functools
import math

import numpy as np
import jax
import jax.numpy as jnp
from jax import lax
from jax.experimental import pallas as pl
from jax.experimental.pallas import tpu as pltpu

F32 = jnp.float32
BF16 = jnp.bfloat16

D_MODEL = 1024
GRID_W = 64
EPS = 1e-6
ROPE_BASE = 10000.0
LANES = 128

F_GROUPS = 4
F_GROUP_DIM = 128
F_WIDTH = F_GROUPS * F_GROUP_DIM
FFT_TILE = 16

MLA_HEADS = 8
MLA_NOPE = 64
MLA_ROPE = 32
MLA_V = 64
MLA_Q_RANK = 384
MLA_KV_RANK = 256
MLA_WIDTH = MLA_HEADS * MLA_V
MLA_SCALE = 1.0 / math.sqrt(MLA_NOPE + MLA_ROPE)
LOG2E = math.log2(math.e)
MLA_VT_ROWS = 80
PV_KEYS = 256
MLA_VT_CHUNK = 512
MLA_TQ = 256
SOFTMAX_GROUP = 2
PIPE_SKEW = 3
SCORE_SLOTS = 4
GATE_LAG = 2

GQA_HEADS = 16
GQA_KV_HEADS = 4
GQA_GROUP = GQA_HEADS // GQA_KV_HEADS
GQA_HEAD_DIM = 64
WINDOW = 128
BLOCK = 128
GQA_Q = GQA_HEADS * GQA_HEAD_DIM
GQA_KV = GQA_KV_HEADS * GQA_HEAD_DIM
GQA_SCALE = 1.0 / math.sqrt(GQA_HEAD_DIM)
GQA_VT_ROWS = 80
A1_BLOCKS = 8
A1_SKEW = 5
A1_SLOTS = 6
A1_GATE_LAG = 3

E_FIN, E_FGATE, E_MGATE, E_QA, E_KVA, E_KPE, E_END = 0, 512, 1024, 1536, 1920, 2176, 2304
O_Q, O_K, O_V, O_G, O_END = 0, 1024, 1280, 1536, 2560

NEG_BIG = -1e30
V7X_VMEM_BYTES = 64 * 1024 * 1024
VMEM_LIMIT = V7X_VMEM_BYTES * 7 // 8


def _cparams(sem):
    return pltpu.CompilerParams(dimension_semantics=sem, vmem_limit_bytes=VMEM_LIMIT)


def _dot(a, b):
    return jnp.dot(a, b, preferred_element_type=F32)


def _dot_nt(a, b):
    return lax.dot_general(a, b, (((1,), (1,)), ((), ())), preferred_element_type=F32)


def _rms(x, g):
    return x * lax.rsqrt(jnp.mean(x * x, axis=-1, keepdims=True) + EPS) * g


def _silu(x):
    return x * jax.nn.sigmoid(x)


def _rope(x, tab_ref, shift):
    up = pltpu.roll(x, LANES - shift, 1)
    dn = pltpu.roll(x, shift, 1)
    return x * tab_ref[0] + up * tab_ref[1] + dn * tab_ref[2]


def _full(shape):
    nd = len(shape)
    return pl.BlockSpec(shape, lambda *_: (0,) * nd)


def _axial_rope_tables(n, rot_dim):
    rows = n // GRID_W
    row = np.repeat(np.arange(rows), GRID_W).astype(np.float64)
    col = np.tile(np.arange(GRID_W), rows).astype(np.float64)
    nf = rot_dim // 4
    inv = (np.float32(ROPE_BASE) ** (-np.arange(nf, dtype=np.float32) / np.float32(nf))).astype(np.float64)
    ang = np.concatenate([row[:, None] * inv, col[:, None] * inv], axis=-1)
    return np.cos(ang), np.sin(ang)


def _mla_rope_tab(cos, sin):
    n = cos.shape[0]
    z = lambda w: np.zeros((n, w))
    cf = np.concatenate([np.ones((n, MLA_NOPE)), cos, cos, z(32)], axis=-1)
    s1 = np.concatenate([z(MLA_NOPE), -sin, z(16), z(32)], axis=-1)
    s2 = np.concatenate([z(MLA_NOPE), z(16), sin, z(32)], axis=-1)
    return _const(np.stack([cf, s1, s2]))


def _gqa_rope_tab(cos, sin):
    z = np.zeros_like(sin)
    cf = np.concatenate([cos, cos, z, z], axis=-1)
    s1 = np.concatenate([-sin, z, z, z], axis=-1)
    s2 = np.concatenate([z, sin, z, z], axis=-1)
    return _const(np.stack([cf, s1, s2]))


def _const(a):
    return jnp.asarray(np.ascontiguousarray(a, dtype=np.float32))


def _dft_cs(n):
    idx = np.arange(n, dtype=np.int64)
    ang = 2.0 * np.pi * ((idx[:, None] * idx[None, :]) % n).astype(np.float64) / n
    return np.cos(ang), np.sin(ang)


def _mod_kernel(s_ref, w_ref, b_ref, o_ref):
    s = _silu(s_ref[...])
    o_ref[0] = _dot(s.astype(BF16), w_ref[0].astype(BF16)) + b_ref[0]


def _mod_call(svec, w_mod, b_mod):
    depth, d, d3 = w_mod.shape
    tn = 768
    return pl.pallas_call(
        _mod_kernel,
        grid=(depth, d3 // tn),
        in_specs=[
            pl.BlockSpec((8, d), lambda l, j: (0, 0)),
            pl.BlockSpec((1, d, tn), lambda l, j: (l, 0, j)),
            pl.BlockSpec((1, 1, tn), lambda l, j: (l, 0, j)),
        ],
        out_specs=pl.BlockSpec((1, 8, tn), lambda l, j: (l, 0, j)),
        out_shape=jax.ShapeDtypeStruct((depth, 8, d3), F32),
        compiler_params=_cparams(("arbitrary", "arbitrary")),
        name="mod",
    )(svec, w_mod, b_mod.reshape(depth, 1, d3))


def _even_front(x, ng, sc, sh, w_in_ref, qn, w_qbt_ref, kvn, w_k_ref, w_vt_ref, vone, tqt_ref, tk_ref, fc_ref):
    m = x.shape[0]
    h = (_rms(x, ng) * (1.0 + sc) + sh).astype(BF16)
    f_in = _dot(h, w_in_ref[:, E_FIN:E_FGATE])
    f_gate = _dot(h, w_in_ref[:, E_FGATE:E_MGATE])
    m_gate = _dot(h, w_in_ref[:, E_MGATE:E_QA])
    low_rank = _dot(h, w_in_ref[:, E_QA:E_END])
    q_a = low_rank[:, 0:E_KVA - E_QA]
    kv_a = low_rank[:, E_KVA - E_QA:E_KPE - E_QA]
    kpe = low_rank[:, E_KPE - E_QA:E_END - E_QA]
    qh = _rms(q_a, qn).astype(BF16)
    ch = _rms(kv_a, kvn).astype(BF16)
    kpe_r = _rope(kpe, tk_ref, 16)
    qt = _dot_nt(w_qbt_ref[...], qh)
    k_all = _dot(ch, w_k_ref[...])
    cos = tqt_ref[0]
    sin = tqt_ref[1]
    pad = jnp.zeros((LANES - MLA_NOPE - MLA_ROPE, m), F32)
    qts, ks = [], []
    for hd in range(MLA_HEADS):
        lo = hd * LANES
        ql = hd * (MLA_NOPE + MLA_ROPE)
        x1 = qt[ql + MLA_NOPE:ql + MLA_NOPE + 16]
        x2 = qt[ql + MLA_NOPE + 16:ql + MLA_NOPE + 32]
        qts.append(jnp.concatenate(
            [qt[ql:ql + MLA_NOPE] * (MLA_SCALE * LOG2E), x1 * cos - x2 * sin, x2 * cos + x1 * sin, pad], axis=0))
        ks.append(k_all[:, lo:lo + LANES] + kpe_r)
    vt = _dot_nt(w_vt_ref[...], ch) + vone
    zr, zi = [], []
    fb = f_in.astype(BF16)
    for g in range(F_GROUPS):
        z = _dot(fb[:, g * LANES:(g + 1) * LANES], fc_ref[...])
        zr.append(z[:, :LANES])
        zi.append(z[:, LANES:])
    return qts, ks, vt, zr, zi, f_gate, m_gate


def _front0_kernel(x_ref, ng_ref, sc_ref, sh_ref, w_in_ref, qn_ref, w_qbt_ref, kvn_ref, w_k_ref, w_vt_ref,
                   vone_ref, tqt_ref, tk_ref, fc_ref, q_ref, k_ref, v_ref, zr_ref, zi_ref, gf_ref, gm_ref):
    qts, ks, vt, zr, zi, f_gate, m_gate = _even_front(
        x_ref[0], ng_ref[...], sc_ref[0], sh_ref[0], w_in_ref, qn_ref[...], w_qbt_ref, kvn_ref[...],
        w_k_ref, w_vt_ref, vone_ref[...], tqt_ref, tk_ref, fc_ref)
    for hd in range(MLA_HEADS):
        q_ref[0, hd] = qts[hd].astype(BF16)
        k_ref[0, hd] = ks[hd].astype(BF16)
        vh = vt[hd * MLA_VT_ROWS:(hd + 1) * MLA_VT_ROWS].astype(BF16)
        for ck in range(v_ref.shape[2]):
            v_ref[0, hd, ck] = vh[:, ck * MLA_VT_CHUNK:(ck + 1) * MLA_VT_CHUNK]
    for g in range(F_GROUPS):
        zr_ref[0, :, g * LANES:(g + 1) * LANES] = zr[g].astype(BF16)
        zi_ref[0, :, g * LANES:(g + 1) * LANES] = zi[g].astype(BF16)
    gf_ref[0] = _silu(f_gate).astype(BF16)
    gm_ref[0] = _silu(m_gate).astype(BF16)


def _front0_call(x, ng, sc, sh, w_in, qn, w_qbt, kvn, w_k, w_vt, vone, tqt, tk, fc, tm):
    b, n, d = x.shape
    fw = jax.ShapeDtypeStruct((b, n, F_WIDTH), BF16)
    row_spec = pl.BlockSpec((1, tm, F_WIDTH), lambda bi, i: (bi, i, 0))
    vec_spec = pl.BlockSpec((1, 1, d), lambda bi, i: (bi, 0, 0))
    return pl.pallas_call(
        _front0_kernel,
        grid=(b, n // tm),
        in_specs=[
            pl.BlockSpec((1, tm, d), lambda bi, i: (bi, i, 0)),
            _full(ng.shape), vec_spec, vec_spec,
            _full(w_in.shape), _full(qn.shape), _full(w_qbt.shape), _full(kvn.shape), _full(w_k.shape),
            _full(w_vt.shape), _full(vone.shape),
            pl.BlockSpec((2, 16, tm), lambda bi, i: (0, 0, i)),
            pl.BlockSpec((3, tm, LANES), lambda bi, i: (0, i, 0)),
            _full(fc.shape),
        ],
        out_specs=[
            pl.BlockSpec((1, MLA_HEADS, LANES, tm), lambda bi, i: (bi, 0, 0, i)),
            pl.BlockSpec((1, MLA_HEADS, tm, LANES), lambda bi, i: (bi, 0, i, 0)),
            pl.BlockSpec((1, MLA_HEADS, tm // MLA_VT_CHUNK, MLA_VT_ROWS, MLA_VT_CHUNK),
                         lambda bi, i: (bi, 0, i, 0, 0)),
            row_spec, row_spec, row_spec, row_spec],
        out_shape=[
            jax.ShapeDtypeStruct((b, MLA_HEADS, LANES, n), BF16),
            jax.ShapeDtypeStruct((b, MLA_HEADS, n, LANES), BF16),
            jax.ShapeDtypeStruct((b, MLA_HEADS, n // MLA_VT_CHUNK, MLA_VT_ROWS, MLA_VT_CHUNK), BF16),
            fw, fw, fw, fw],
        compiler_params=_cparams(("parallel", "arbitrary")),
        name="front0",
    )(x, ng, sc, sh, w_in, qn, w_qbt, kvn, w_k, w_vt, vone, tqt, tk, fc)


def _zero_after(x):
    u = lax.bitcast_convert_type(x, jnp.uint32)
    z = lax.shift_right_logical(lax.shift_right_logical(u, jnp.uint32(16)), jnp.uint32(16))
    return lax.bitcast_convert_type(z, F32)


def _mla_chunk(qt, k, vt, m, acc):
    s = _dot(k, qt)
    m_new = jnp.maximum(m, jnp.max(s, axis=0, keepdims=True))
    alpha = jnp.exp2(m - m_new)
    p = jnp.exp2(s - m_new).astype(BF16)
    return m_new, alpha * acc + _dot(vt, p)


def _mla_pair_out(acc0, acc1):
    o0 = acc0[0:MLA_V] / acc0[MLA_V:MLA_V + 1]
    o1 = acc1[0:MLA_V] / acc1[MLA_V:MLA_V + 1]
    return jnp.concatenate([o0, o1], axis=0).T


def _ctx_kernel(ctx_scale, x_ref, ng0_ref, sc0_ref, sh0_ref, g0_ref, ng1_ref, sc1_ref, sh1_ref,
                w_in_ref, qn_ref, w_qbt_ref, kvn_ref, w_k_ref, w_vt_ref, vone_ref, tqt_ref, tk_ref, fc_ref,
                fn_ref, w_out_ref, w_k1_ref, w_vt1_ref, vone1_ref,
                kc_ref, vc_ref, ck1_ref, cv1_ref):
    x = x_ref[0]
    c = x.shape[0]
    qts, ks, vt, zr, zi, f_gate, m_gate = _even_front(
        x, ng0_ref[...], sc0_ref[...], sh0_ref[...], w_in_ref, qn_ref[...], w_qbt_ref, kvn_ref[...],
        w_k_ref, w_vt_ref, vone_ref[...], tqt_ref, tk_ref, fc_ref)
    pairs = []
    for hp in range(MLA_HEADS // 2):
        accs = []
        for e in range(2):
            hd = 2 * hp + e
            kb = ks[hd].astype(BF16)
            vtb = vt[hd * MLA_VT_ROWS:(hd + 1) * MLA_VT_ROWS].astype(BF16)
            kc_ref[0, hd] = kb
            vc_ref[0, hd, 0] = vtb
            m0 = jnp.full((1, c), -jnp.inf, F32)
            acc0 = jnp.zeros((MLA_VT_ROWS, c), F32)
            accs.append(_mla_chunk(qts[hd].astype(BF16), kb, vtb, m0, acc0)[1])
        pairs.append(_mla_pair_out(accs[0], accs[1]))
    a = jnp.concatenate(pairs, axis=-1)
    z = jnp.concatenate([jnp.concatenate(zr, axis=-1), jnp.concatenate(zi, axis=-1)], axis=0).astype(BF16)
    fm = _dot(fn_ref[...], z) * ctx_scale
    yf = (fm * _silu(f_gate)).astype(BF16)
    ya = (a * _silu(m_gate)).astype(BF16)
    y = _dot(yf, w_out_ref[0:F_WIDTH, :]) + _dot(ya, w_out_ref[F_WIDTH:, :])
    x1 = x + g0_ref[...] * y
    h1 = (_rms(x1, ng1_ref[...]) * (1.0 + sc1_ref[...]) + sh1_ref[...]).astype(BF16)
    vt1 = _dot_nt(w_vt1_ref[...], h1) + vone1_ref[...]
    for g in range(GQA_KV_HEADS):
        ck1_ref[0, g] = _dot(h1, w_k1_ref[:, g * LANES:(g + 1) * LANES]).astype(BF16)
        cv1_ref[0, g] = vt1[g * GQA_VT_ROWS:(g + 1) * GQA_VT_ROWS].astype(BF16)


def _ctx_call(ctx, vecs0, vecs1, w_in, qn, w_qbt, kvn, w_k, w_vt, vone, tqt, tk, fc, fn, w_out, w_k1, w_vt1,
              vone1):
    b, c, d = ctx.shape
    ng0, sc0, sh0, g0 = vecs0
    ng1, sc1, sh1 = vecs1
    ctx_scale = 1.0 / math.sqrt(c * F_GROUP_DIM)
    consts = [ng0, sc0, sh0, g0, ng1, sc1, sh1, w_in, qn, w_qbt, kvn, w_k, w_vt, vone, tqt, tk, fc, fn, w_out,
              w_k1, w_vt1, vone1]
    return pl.pallas_call(
        functools.partial(_ctx_kernel, ctx_scale),
        grid=(b,),
        in_specs=[pl.BlockSpec((1, c, d), lambda bi: (bi, 0, 0))] + [_full(a.shape) for a in consts],
        out_specs=[pl.BlockSpec((1, MLA_HEADS, c, LANES), lambda bi: (bi, 0, 0, 0)),
                   pl.BlockSpec((1, MLA_HEADS, 1, MLA_VT_ROWS, c), lambda bi: (bi, 0, 0, 0, 0)),
                   pl.BlockSpec((1, GQA_KV_HEADS, c, LANES), lambda bi: (bi, 0, 0, 0)),
                   pl.BlockSpec((1, GQA_KV_HEADS, GQA_VT_ROWS, c), lambda bi: (bi, 0, 0, 0))],
        out_shape=[jax.ShapeDtypeStruct((b, MLA_HEADS, c, LANES), BF16),
                   jax.ShapeDtypeStruct((b, MLA_HEADS, 1, MLA_VT_ROWS, c), BF16),
                   jax.ShapeDtypeStruct((b, GQA_KV_HEADS, c, LANES), BF16),
                   jax.ShapeDtypeStruct((b, GQA_KV_HEADS, GQA_VT_ROWS, c), BF16)],
        compiler_params=_cparams(("arbitrary",)),
        name="ctx",
    )(ctx, *consts)


def _attn0_kernel(qt_ref, kc_ref, vct_ref, kl_ref, vlt_ref, gm_ref, o_ref, s_ref):
    tq = qt_ref.shape[-1]
    nch, _, tk = vlt_ref.shape[2:]
    nc = kc_ref.shape[2]
    n_ctx = nc // PV_KEYS
    n_sub = n_ctx + nch * tk // PV_KEYS
    group = [0] * n_ctx + [1 + j // SOFTMAX_GROUP for j in range(n_sub - n_ctx)]
    first = [t for t in range(n_sub) if t == 0 or group[t] != group[t - 1]]
    last = [t for t in range(n_sub) if t == n_sub - 1 or group[t] != group[t + 1]]

    def keys(e, t):
        if t < n_ctx:
            return kc_ref[0, e, t * PV_KEYS:(t + 1) * PV_KEYS, :]
        lo = (t - n_ctx) * PV_KEYS
        return kl_ref[0, e, lo:lo + PV_KEYS, :]

    def values_t(e, t):
        if t < n_ctx:
            return vct_ref[0, e, 0, :, t * PV_KEYS:(t + 1) * PV_KEYS]
        j, off = divmod((t - n_ctx) * PV_KEYS, tk)
        return vlt_ref[0, e, j, :, off:off + PV_KEYS]

    def slot(i):
        r = i % SCORE_SLOTS
        return slice(r * PV_KEYS, (r + 1) * PV_KEYS)

    tasks = [(qi, t) for qi in range(tq // MLA_TQ) for t in range(n_sub)]
    colmax = []
    for step in range(len(tasks) + PIPE_SKEW):
        if step < len(tasks):
            qi, t = tasks[step]
            cols = slice(qi * MLA_TQ, (qi + 1) * MLA_TQ)
            cm = []
            for e in range(2):
                s = _dot(keys(e, t), qt_ref[0, e, :, cols])
                s_ref[e, slot(step)] = s
                cm.append(jnp.max(s, axis=0, keepdims=True))
            colmax.append(cm)
        if step < PIPE_SKEW:
            continue
        qi, u = tasks[step - PIPE_SKEW]
        if u == 0:
            m = [jnp.full((1, MLA_TQ), -jnp.inf, F32) for _ in range(2)]
            acc = [jnp.zeros((MLA_VT_ROWS, MLA_TQ), F32) for _ in range(2)]
        if u in first:
            members = [qi * n_sub + t for t in range(n_sub) if group[t] == group[u]]
            assert members[-1] <= step
            m_new, alpha = [], []
            for e in range(2):
                mc = functools.reduce(jnp.maximum, [colmax[i][e] for i in members])
                m_new.append(jnp.maximum(m[e], mc))
                alpha.append(jnp.exp2(m[e] - m_new[-1]))
            pv = [None, None]
        gate = min(step - GATE_LAG, len(tasks) - 1)
        for e in range(2):
            m_use = m_new[e] + _zero_after(colmax[gate][e])
            p = jnp.exp2(s_ref[e, slot(step - PIPE_SKEW)] - m_use).astype(BF16)
            d = _dot(values_t(e, u), p)
            pv[e] = d if pv[e] is None else pv[e] + d
        if u in last:
            for e in range(2):
                acc[e] = alpha[e] * acc[e] + pv[e]
                m[e] = m_new[e]
        if u == n_sub - 1:
            rows = slice(qi * MLA_TQ, (qi + 1) * MLA_TQ)
            o_ref[0, rows] = (_mla_pair_out(acc[0], acc[1]) * gm_ref[0, rows].astype(F32)).astype(BF16)


def _attn0_call(qt, kc, vct, kl, vlt, gm, tq):
    b, h, _, n = qt.shape
    c = kc.shape[2]
    nch, _, tk = vlt.shape[2:]
    assert c % PV_KEYS == 0 and tk % PV_KEYS == 0 and SCORE_SLOTS > PIPE_SKEW >= SOFTMAX_GROUP - 1
    assert tq % MLA_TQ == 0
    o_spec = pl.BlockSpec((1, tq, LANES), lambda bi, hp, i: (bi, i, hp))
    return pl.pallas_call(
        _attn0_kernel,
        grid=(b, h // 2, n // tq),
        in_specs=[
            pl.BlockSpec((1, 2, LANES, tq), lambda bi, hp, i: (bi, hp, 0, i)),
            pl.BlockSpec((1, 2, c, LANES), lambda bi, hp, i: (bi, hp, 0, 0)),
            pl.BlockSpec((1, 2, 1, MLA_VT_ROWS, c), lambda bi, hp, i: (bi, hp, 0, 0, 0)),
            pl.BlockSpec((1, 2, n, LANES), lambda bi, hp, i: (bi, hp, 0, 0)),
            pl.BlockSpec((1, 2, nch, MLA_VT_ROWS, tk), lambda bi, hp, i: (bi, hp, 0, 0, 0)),
            o_spec],
        out_specs=o_spec,
        out_shape=jax.ShapeDtypeStruct((b, n, MLA_WIDTH), BF16),
        scratch_shapes=[pltpu.VMEM((2, SCORE_SLOTS * PV_KEYS, MLA_TQ), F32)],
        compiler_params=_cparams(("parallel", "parallel", "arbitrary")),
        name="attn0",
    )(qt, kc, vct, kl, vlt, gm)


def _swap16(p16, blocks):
    ys = [_dot(p16, blk).astype(BF16) for blk in blocks]
    return [jnp.concatenate([y[t * FFT_TILE:(t + 1) * FFT_TILE] for y in ys], axis=0) for t in range(FFT_TILE)]


def _fft_a_kernel(n1, zr_ref, zi_ref, p16_ref, fa_ref, tw_ref, tr_ref, ti_ref):
    rows = n1 * FFT_TILE
    z = jnp.concatenate([zr_ref[0].reshape(rows, F_WIDTH), zi_ref[0].reshape(rows, F_WIDTH)], axis=1)
    grp = FFT_TILE * FFT_TILE
    per_r = _swap16(p16_ref[...], [z[j * grp:(j + 1) * grp] for j in range(rows // grp)])
    for r in range(FFT_TILE):
        zz = per_r[r]
        a = _dot(fa_ref[...], jnp.concatenate([zz[:, :F_WIDTH], zz[:, F_WIDTH:]], axis=0))
        c = tw_ref[0, r]
        s = tw_ref[1, r]
        for g in range(F_GROUPS):
            ar = a[0:n1, g * LANES:(g + 1) * LANES]
            ai = a[n1:2 * n1, g * LANES:(g + 1) * LANES]
            tr_ref[0, r, :, g * LANES:(g + 1) * LANES] = (ar * c + ai * s).astype(BF16)
            ti_ref[0, r, :, g * LANES:(g + 1) * LANES] = (ai * c - ar * s).astype(BF16)


def _fft_a_call(zr, zi, p16, fa, tw):
    b, n1, r, w = zr.shape
    in_spec = pl.BlockSpec((1, n1, FFT_TILE, w), lambda bi, j: (bi, 0, j, 0))
    out_spec = pl.BlockSpec((1, FFT_TILE, n1, w), lambda bi, j: (bi, j, 0, 0))
    out = jax.ShapeDtypeStruct((b, r, n1, w), BF16)
    return pl.pallas_call(
        functools.partial(_fft_a_kernel, n1),
        grid=(b, r // FFT_TILE),
        in_specs=[in_spec, in_spec, _full(p16.shape), _full(fa.shape),
                  pl.BlockSpec((2, FFT_TILE, n1, LANES), lambda bi, j: (0, j, 0, 0))],
        out_specs=[out_spec, out_spec],
        out_shape=[out, out],
        compiler_params=_cparams(("parallel", "arbitrary")),
        name="fft_a",
    )(zr, zi, p16, fa, tw)


def _fft_c_kernel(tr_ref, ti_ref, p16_ref, fcs_ref, gf_ref, o_ref):
    nr = tr_ref.shape[1]
    rows = nr * FFT_TILE
    grp = FFT_TILE * FFT_TILE
    t = jnp.concatenate([tr_ref[0].reshape(rows, F_WIDTH), ti_ref[0].reshape(rows, F_WIDTH)], axis=1)
    per_k = _swap16(p16_ref[...], [t[j * grp:(j + 1) * grp] for j in range(rows // grp)])
    outs = []
    for k in range(FFT_TILE):
        tt = per_k[k]
        rhs = jnp.concatenate([tt[:, :F_WIDTH], tt[:, F_WIDTH:]], axis=0)
        outs.append(_dot(fcs_ref[...], rhs).astype(BF16))
    gf = gf_ref[0].reshape(rows, F_WIDTH)
    for j in range(nr // FFT_TILE):
        blk = jnp.concatenate([o[j * FFT_TILE:(j + 1) * FFT_TILE] for o in outs], axis=0)
        y = _dot(p16_ref[...], blk)
        o_ref[0, j * FFT_TILE:(j + 1) * FFT_TILE] = (
            y * gf[j * grp:(j + 1) * grp].astype(F32)).astype(BF16).reshape(FFT_TILE, FFT_TILE, F_WIDTH)


def _fft_c_call(tr, ti, p16, fcs, gf):
    b, r, n1, w = tr.shape
    spec = pl.BlockSpec((1, r, FFT_TILE, w), lambda bi, i: (bi, 0, i, 0))
    return pl.pallas_call(
        _fft_c_kernel,
        grid=(b, n1 // FFT_TILE),
        in_specs=[spec, spec, _full(p16.shape), _full(fcs.shape), spec],
        out_specs=spec,
        out_shape=jax.ShapeDtypeStruct((b, r, n1, w), BF16),
        compiler_params=_cparams(("parallel", "arbitrary")),
        name="fft_c",
    )(tr, ti, p16, fcs, gf)


def _mid_kernel(fm_ref, ag_ref, x_ref, g0_ref, w_out_ref, ng_ref, sc_ref, sh_ref, w_qt_ref, w_k_ref, w_vt_ref,
                vone_ref, w_g_ref, tqt_ref, tk_ref, x1_ref, q_ref, k_ref, v_ref, sg_ref):
    tm = x_ref.shape[1]
    y = _dot(fm_ref[0], w_out_ref[0:F_WIDTH, :]) + _dot(ag_ref[0], w_out_ref[F_WIDTH:, :])
    x1 = x_ref[0] + g0_ref[0] * y
    x1_ref[0] = x1
    h = (_rms(x1, ng_ref[...]) * (1.0 + sc_ref[0]) + sh_ref[0]).astype(BF16)
    qt = _dot_nt(w_qt_ref[...], h)
    cos = tqt_ref[0]
    sin = tqt_ref[1]
    half = GQA_HEAD_DIM // 2
    for hd in range(GQA_HEADS):
        lo = hd * GQA_HEAD_DIM
        x1r = qt[lo:lo + half]
        x2r = qt[lo + half:lo + GQA_HEAD_DIM]
        qh = jnp.concatenate([x1r * cos - x2r * sin, x2r * cos + x1r * sin], axis=0).astype(BF16)
        g, j = divmod(hd, GQA_GROUP)
        for blk in range(tm // BLOCK):
            q_ref[0, blk, g, :, j * BLOCK:(j + 1) * BLOCK] = qh[:, blk * BLOCK:(blk + 1) * BLOCK]
    vt = _dot_nt(w_vt_ref[...], h) + vone_ref[...]
    k_all = _dot(h, w_k_ref[...])
    for g in range(GQA_KV_HEADS):
        k_ref[0, g] = _rope(k_all[:, g * LANES:(g + 1) * LANES], tk_ref, 32).astype(BF16)
        v_ref[0, g] = vt[g * GQA_VT_ROWS:(g + 1) * GQA_VT_ROWS].astype(BF16)
    sg_ref[0] = _silu(_dot(h, w_g_ref[...])).astype(BF16)


def _mid_call(fm, ag, x, g0, w_out, ng, sc, sh, w_qt, w_k, w_vt, vone, w_g, tqt, tk, tm):
    b, n, d = x.shape
    row = lambda w: pl.BlockSpec((1, tm, w), lambda bi, i: (bi, i, 0))
    vec_spec = pl.BlockSpec((1, 1, d), lambda bi, i: (bi, 0, 0))
    nblk = tm // BLOCK
    return pl.pallas_call(
        _mid_kernel,
        grid=(b, n // tm),
        in_specs=[row(F_WIDTH), row(MLA_WIDTH), row(d), vec_spec, _full(w_out.shape), _full(ng.shape),
                  vec_spec, vec_spec, _full(w_qt.shape), _full(w_k.shape), _full(w_vt.shape), _full(vone.shape),
                  _full(w_g.shape),
                  pl.BlockSpec((2, GQA_HEAD_DIM // 2, tm), lambda bi, i: (0, 0, i)),
                  pl.BlockSpec((3, tm, LANES), lambda bi, i: (0, i, 0))],
        out_specs=[row(d),
                   pl.BlockSpec((1, nblk, GQA_KV_HEADS, GQA_HEAD_DIM, GQA_GROUP * BLOCK),
                                lambda bi, i: (bi, i, 0, 0, 0)),
                   pl.BlockSpec((1, GQA_KV_HEADS, tm, LANES), lambda bi, i: (bi, 0, i, 0)),
                   pl.BlockSpec((1, GQA_KV_HEADS, GQA_VT_ROWS, tm), lambda bi, i: (bi, 0, 0, i)),
                   row(GQA_Q)],
        out_shape=[jax.ShapeDtypeStruct((b, n, d), F32),
                   jax.ShapeDtypeStruct((b, n // BLOCK, GQA_KV_HEADS, GQA_HEAD_DIM, GQA_GROUP * BLOCK), BF16),
                   jax.ShapeDtypeStruct((b, GQA_KV_HEADS, n, LANES), BF16),
                   jax.ShapeDtypeStruct((b, GQA_KV_HEADS, GQA_VT_ROWS, n), BF16),
                   jax.ShapeDtypeStruct((b, n, GQA_Q), BF16)],
        compiler_params=_cparams(("parallel", "arbitrary")),
        name="mid",
    )(fm, ag, x, g0, w_out, ng, sc, sh, w_qt, w_k, w_vt, vone, w_g, tqt, tk)


def _band_bias(nc):
    col = np.arange(3 * BLOCK)[:, None]
    q = np.arange(BLOCK)[None, :]
    dist = BLOCK + q - col
    band = np.abs(dist) <= WINDOW
    variants = [band & (col >= BLOCK), band, band & (col < 2 * BLOCK)]
    out = np.zeros((3, nc + 3 * BLOCK, BLOCK), np.float32)
    for v, ok in enumerate(variants):
        out[v, nc:] = np.where(ok, 0.0, NEG_BIG)
    return jnp.asarray(out)


def _attn1_kernel(sink_ref, q_ref, kp_ref, kc_ref, kn_ref, vp_ref, vc_ref, vn_ref, ck_ref, cv_ref, bias_ref,
                  sg_ref, x1_ref, g1_ref, w_out_ref, fg_ref, o_ref, s_ref):
    i = pl.program_id(1)
    last = pl.num_programs(1) - 1
    nc = ck_ref.shape[2]
    sel = []
    for blk in range(A1_BLOCKS):
        v = 1
        if blk == 0:
            v = jnp.where(i == 0, 0, v)
        if blk == A1_BLOCKS - 1:
            v = jnp.where(i == last, 2, v)
        sel.append(v)

    def band_keys(g, blk, j):
        pos = blk + j - 1
        if pos < 0:
            return kp_ref[0, g], vp_ref[0, g]
        if pos >= A1_BLOCKS:
            return kn_ref[0, g], vn_ref[0, g]
        return (kc_ref[0, g, pos * BLOCK:(pos + 1) * BLOCK, :], vc_ref[0, g, :, pos * BLOCK:(pos + 1) * BLOCK])

    def operands(blk, g, sb):
        if sb == 0:
            return ck_ref[0, g][:, :GQA_HEAD_DIM], cv_ref[0, g], None
        if sb == 1:
            (k0, v0), (k1, v1) = band_keys(g, blk, 0), band_keys(g, blk, 1)
            return (jnp.concatenate([k0, k1], axis=0)[:, :GQA_HEAD_DIM], jnp.concatenate([v0, v1], axis=1),
                    bias_ref[sel[blk], nc:nc + 2 * BLOCK])
        k2, v2 = band_keys(g, blk, 2)
        return k2[:, :GQA_HEAD_DIM], v2, bias_ref[sel[blk], nc + 2 * BLOCK:nc + 3 * BLOCK]

    def slot(idx, rows):
        r = idx % A1_SLOTS
        return slice(r * 2 * BLOCK, r * 2 * BLOCK + rows)

    tasks = [(blk, g, sb) for blk in range(A1_BLOCKS) for g in range(GQA_KV_HEADS) for sb in range(3)]
    colmax = []
    chunks = {}
    for step in range(len(tasks) + A1_SKEW):
        if step < len(tasks):
            blk, g, sb = tasks[step]
            kb, _, bias = operands(blk, g, sb)
            s = _dot(kb, q_ref[0, blk, g])
            if bias is not None:
                s = s + jnp.concatenate([bias] * GQA_GROUP, axis=1)
            s_ref[slot(step, s.shape[0])] = s
            colmax.append(jnp.max(s, axis=0, keepdims=True))
        if step < A1_SKEW:
            continue
        idx = step - A1_SKEW
        blk, g, sb = tasks[idx]
        _, vt, _ = operands(blk, g, sb)
        if sb == 0:
            assert idx + 2 <= step
            sink = jnp.concatenate(
                [jnp.full((1, BLOCK), sink_ref[GQA_GROUP * g + j] * LOG2E, F32) for j in range(GQA_GROUP)], axis=1)
            m = functools.reduce(jnp.maximum, [colmax[idx], colmax[idx + 1], colmax[idx + 2], sink])
            acc = None
        gate = min(step - A1_GATE_LAG, len(tasks) - 1)
        m_use = m + _zero_after(colmax[gate])
        p = jnp.exp2(s_ref[slot(idx, vt.shape[1])] - m_use).astype(BF16)
        d = _dot(vt, p)
        acc = d if acc is None else acc + d
        if sb == 2:
            l = acc[GQA_HEAD_DIM:GQA_HEAD_DIM + 1] + jnp.exp2(sink - m)
            ot = acc[0:GQA_HEAD_DIM] / l
            for p2 in range(2):
                lo = 2 * p2 * BLOCK
                pair = jnp.concatenate([ot[:, lo:lo + BLOCK], ot[:, lo + BLOCK:lo + 2 * BLOCK]], axis=0)
                chunks[(blk, 2 * g + p2)] = pair.T
            if g == GQA_KV_HEADS - 1:
                rows = slice(blk * BLOCK, (blk + 1) * BLOCK)
                o = jnp.concatenate([chunks[(blk, c)] for c in range(GQA_Q // LANES)], axis=-1)
                og = (o * sg_ref[0, rows].astype(F32)).astype(BF16)
                y = _dot(og, w_out_ref[...])
                x2 = x1_ref[0, rows] + g1_ref[0] * y
                o_ref[0, rows] = _rms(x2, fg_ref[...])


def _attn1_call(sink, qt, k, vt, ck, cvt, bias, sg, x1, g1, w_out, fg):
    b, n, d = x1.shape
    nc = ck.shape[2]
    nb = n // BLOCK
    ns = nb // A1_BLOCKS
    assert nb % A1_BLOCKS == 0 and nc == 2 * BLOCK and A1_SLOTS > A1_SKEW >= 2
    tr = A1_BLOCKS * BLOCK
    row = lambda w: pl.BlockSpec((1, tr, w), lambda bi, i: (bi, i, 0))
    prv = lambda i: jnp.maximum(i * A1_BLOCKS - 1, 0)
    nxt = lambda i: jnp.minimum((i + 1) * A1_BLOCKS, nb - 1)
    k_spec = lambda f: pl.BlockSpec((1, GQA_KV_HEADS, BLOCK, LANES), lambda bi, i: (bi, 0, f(i), 0))
    v_spec = lambda f: pl.BlockSpec((1, GQA_KV_HEADS, GQA_VT_ROWS, BLOCK), lambda bi, i: (bi, 0, 0, f(i)))
    vec_spec = pl.BlockSpec((1, 1, d), lambda bi, i: (bi, 0, 0))
    return pl.pallas_call(
        _attn1_kernel,
        grid=(b, ns),
        in_specs=[pl.BlockSpec(memory_space=pltpu.SMEM),
                  pl.BlockSpec((1, A1_BLOCKS, GQA_KV_HEADS, GQA_HEAD_DIM, GQA_GROUP * BLOCK),
                               lambda bi, i: (bi, i, 0, 0, 0)),
                  k_spec(prv), pl.BlockSpec((1, GQA_KV_HEADS, tr, LANES), lambda bi, i: (bi, 0, i, 0)), k_spec(nxt),
                  v_spec(prv), pl.BlockSpec((1, GQA_KV_HEADS, GQA_VT_ROWS, tr), lambda bi, i: (bi, 0, 0, i)),
                  v_spec(nxt),
                  pl.BlockSpec((1, GQA_KV_HEADS, nc, LANES), lambda bi, i: (bi, 0, 0, 0)),
                  pl.BlockSpec((1, GQA_KV_HEADS, GQA_VT_ROWS, nc), lambda bi, i: (bi, 0, 0, 0)),
                  _full(bias.shape), row(GQA_Q), row(d), vec_spec, _full(w_out.shape), _full(fg.shape)],
        out_specs=row(d),
        out_shape=jax.ShapeDtypeStruct((b, n, d), F32),
        scratch_shapes=[pltpu.VMEM((A1_SLOTS * 2 * BLOCK, GQA_GROUP * BLOCK), F32)],
        compiler_params=_cparams(("parallel", "arbitrary")),
        name="attn1",
    )(sink, qt, k, k, k, vt, vt, vt, ck, cvt, bias, sg, x1, g1, w_out, fg)


def _prep_even_weights(w_in, w_qb, w_kvb):
    d = w_in.shape[0]
    f_in, f_gate, q_a, kv_a, k_pe, m_gate = jnp.split(
        w_in, np.cumsum([F_WIDTH, F_WIDTH, MLA_Q_RANK, MLA_KV_RANK, MLA_ROPE]).tolist(), axis=1)
    kpe_blk = jnp.concatenate([jnp.zeros((d, MLA_NOPE), F32), k_pe, jnp.zeros((d, 32), F32)], axis=1)
    w_in_p = jnp.concatenate([f_in, f_gate, m_gate, q_a, kv_a, kpe_blk], axis=1).astype(BF16)
    w_qbt = w_qb.T.astype(BF16)
    kvb = w_kvb.reshape(MLA_KV_RANK, MLA_HEADS, MLA_NOPE + MLA_V)
    w_k = jnp.pad(kvb[:, :, :MLA_NOPE], ((0, 0), (0, 0), (0, 64))).reshape(MLA_KV_RANK, MLA_HEADS * LANES)
    w_vt = jnp.pad(kvb[:, :, MLA_NOPE:], ((0, 0), (0, 0), (0, MLA_VT_ROWS - MLA_V)))
    w_vt = w_vt.reshape(MLA_KV_RANK, MLA_HEADS * MLA_VT_ROWS).T.astype(BF16)
    vone = np.zeros((MLA_HEADS * MLA_VT_ROWS, 1), np.float32)
    vone[MLA_V::MLA_VT_ROWS] = 1.0
    return w_in_p, w_qbt, w_k.astype(BF16), w_vt, jnp.asarray(vone)


def _prep_odd_weights(w_in):
    d = w_in.shape[0]
    w_qt = w_in[:, O_Q:O_K].T.astype(BF16)
    kw = w_in[:, O_K:O_V].reshape(d, GQA_KV_HEADS, GQA_HEAD_DIM)
    w_k = jnp.pad(kw, ((0, 0), (0, 0), (0, LANES - GQA_HEAD_DIM))).reshape(d, GQA_KV_HEADS * LANES).astype(BF16)
    vw = w_in[:, O_V:O_G].reshape(d, GQA_KV_HEADS, GQA_HEAD_DIM)
    w_vt = jnp.pad(vw, ((0, 0), (0, 0), (0, GQA_VT_ROWS - GQA_HEAD_DIM)))
    w_vt = w_vt.reshape(d, GQA_KV_HEADS * GQA_VT_ROWS).T.astype(BF16)
    vone = np.zeros((GQA_KV_HEADS * GQA_VT_ROWS, 1), np.float32)
    vone[GQA_HEAD_DIM::GQA_VT_ROWS] = 1.0
    return w_qt, w_k, w_vt, jnp.asarray(vone), w_in[:, O_G:O_END].astype(BF16)


def _fft_consts(n, c):
    n1 = n // LANES
    c1, s1 = _dft_cs(n1)
    fa = np.block([[c1, s1], [-s1, c1]])
    c2, s2 = _dft_cs(LANES)
    fcs = np.concatenate([c2, s2], axis=1) / math.sqrt(n * F_GROUP_DIM)
    fc = np.concatenate([c2, -s2], axis=1)
    r = np.arange(LANES, dtype=np.int64)[:, None]
    k1 = np.arange(n1, dtype=np.int64)[None, :]
    ang = 2.0 * np.pi * ((r * k1) % n).astype(np.float64) / n
    tw = np.stack([np.cos(ang), np.sin(ang)])
    tw = np.broadcast_to(tw[..., None], (2, LANES, n1, LANES))
    cn, sn = _dft_cs(c)
    fn = np.concatenate([cn, sn], axis=1)
    as32 = lambda a: jnp.asarray(np.ascontiguousarray(a, dtype=np.float32))
    idx = np.arange(FFT_TILE * FFT_TILE)
    p16 = np.zeros((FFT_TILE * FFT_TILE,) * 2)
    p16[(idx % FFT_TILE) * FFT_TILE + idx // FFT_TILE, idx] = 1.0
    return (as32(fa).astype(BF16), as32(fcs).astype(BF16), as32(fc).astype(BF16), as32(tw), as32(fn).astype(BF16),
            as32(p16).astype(BF16))


def _pick_tile(n, pref):
    t = pref
    while n % t:
        t //= 2
    return t


def _tiles(n):
    return dict(front=_pick_tile(n, 1024), mid=_pick_tile(n, 1024), attn0=_pick_tile(n, 2 * MLA_TQ))


def kernel(x, c, ctx, c_ctx, w_mod, b_mod, norm_g, e_w_in, e_q_norm, e_w_qb, e_kv_norm, e_w_kvb, e_w_out,
           o_w_in, o_sink, o_w_out, final_g):
    b, n, d = x.shape
    nc = ctx.shape[1]
    assert d == D_MODEL and n % (LANES * FFT_TILE) == 0 and nc % LANES == 0 and b <= 7

    svec = jnp.zeros((8, d), F32).at[:b].set(c).at[b].set(c_ctx)
    mod = _mod_call(svec, w_mod, b_mod)
    sh, sc, gt = mod[:, :, :d], mod[:, :, d:2 * d], mod[:, :, 2 * d:]
    lat = lambda t, l: t[l, :b].reshape(b, 1, d)
    cvec = lambda t, l: t[l, b].reshape(1, d)
    ng0, ng1 = norm_g[0].reshape(1, d), norm_g[1].reshape(1, d)

    cos_m, sin_m = _axial_rope_tables(n, MLA_ROPE)
    qs = MLA_SCALE * LOG2E
    tqt0 = _const(np.stack([cos_m.T, sin_m.T]) * qs)
    tk0 = _mla_rope_tab(cos_m, sin_m)
    one_c, zero_c = np.ones((nc, MLA_ROPE // 2)), np.zeros((nc, MLA_ROPE // 2))
    tqt0c = _const(np.stack([one_c.T, zero_c.T]) * qs)
    tk0c = _mla_rope_tab(one_c, zero_c)
    cos_g, sin_g = _axial_rope_tables(n, GQA_HEAD_DIM)
    tqt1 = _const(np.stack([cos_g.T, sin_g.T]) * (GQA_SCALE * LOG2E))
    tk1 = _gqa_rope_tab(cos_g, sin_g)
    fa, fcs, fc, tw, fn, p16 = _fft_consts(n, nc)
    bias1 = _band_bias(nc)

    w_in0, w_qbt0, w_k0, w_vt0, vone = _prep_even_weights(e_w_in[0], e_w_qb[0], e_w_kvb[0])
    qn0 = e_q_norm[0].reshape(1, MLA_Q_RANK)
    kvn0 = e_kv_norm[0].reshape(1, MLA_KV_RANK)
    w_out0 = e_w_out[0].astype(BF16)
    w_qt1, w_k1, w_vt1, vone1, w_g1 = _prep_odd_weights(o_w_in[0])
    w_out1 = o_w_out[0].astype(BF16)

    kc, vct, ck1, cvt1 = _ctx_call(
        ctx, (ng0, cvec(sc, 0), cvec(sh, 0), cvec(gt, 0)), (ng1, cvec(sc, 1), cvec(sh, 1)),
        w_in0, qn0, w_qbt0, kvn0, w_k0, w_vt0, vone, tqt0c, tk0c, fc, fn, w_out0, w_k1, w_vt1, vone1)

    tiles = _tiles(n)
    qt0, k0, vt0, zr, zi, gf, gm = _front0_call(
        x, ng0, lat(sc, 0), lat(sh, 0), w_in0, qn0, w_qbt0, kvn0, w_k0, w_vt0, vone, tqt0, tk0, fc,
        tiles["front"])

    ag = _attn0_call(qt0, kc, vct, k0, vt0, gm, tiles["attn0"])

    n1 = n // LANES
    tr, ti = _fft_a_call(zr.reshape(b, n1, LANES, F_WIDTH), zi.reshape(b, n1, LANES, F_WIDTH), p16, fa, tw)
    fmg = _fft_c_call(tr, ti, p16, fcs, gf.reshape(b, LANES, n1, F_WIDTH))
    fmg = fmg.reshape(b, n, F_WIDTH)

    x1, qt1, k1, vt1, sg = _mid_call(fmg, ag, x, lat(gt, 0), w_out0, ng1, lat(sc, 1), lat(sh, 1),
                                     w_qt1, w_k1, w_vt1, vone1, w_g1, tqt1, tk1, tiles["mid"])

    return _attn1_call(o_sink[0], qt1, k1, vt1, ck1, cvt1, bias1, sg, x1, lat(gt, 1), w_out1,
                       final_g.reshape(1, d))
```

```python
import functools
import math

import numpy as np
import jax
import jax.numpy as jnp
from jax import lax
from jax.experimental import pallas as pl
from jax.experimental.pallas import tpu as pltpu

F32 = jnp.float32
BF16 = jnp.bfloat16

D_MODEL = 1024
GRID_W = 64
EPS = 1e-6
ROPE_BASE = 10000.0
LANES = 128

F_GROUPS = 4
F_GROUP_DIM = 128
F_WIDTH = F_GROUPS * F_GROUP_DIM
FFT_TILE = 16

MLA_HEADS = 8
MLA_NOPE = 64
MLA_ROPE = 32
MLA_V = 64
MLA_Q_RANK = 384
MLA_KV_RANK = 256
MLA_WIDTH = MLA_HEADS * MLA_V
MLA_SCALE = 1.0 / math.sqrt(MLA_NOPE + MLA_ROPE)
LOG2E = math.log2(math.e)
MLA_VT_ROWS = 80
PV_KEYS = 256
MLA_VT_CHUNK = 512
MLA_TQ = 256
SOFTMAX_GROUP = 2
PIPE_SKEW = 3
SCORE_SLOTS = 4
GATE_LAG = 2

GQA_HEADS = 16
GQA_KV_HEADS = 4
GQA_GROUP = GQA_HEADS // GQA_KV_HEADS
GQA_HEAD_DIM = 64
WINDOW = 128
BLOCK = 128
GQA_Q = GQA_HEADS * GQA_HEAD_DIM
GQA_KV = GQA_KV_HEADS * GQA_HEAD_DIM
GQA_SCALE = 1.0 / math.sqrt(GQA_HEAD_DIM)
GQA_VT_ROWS = 80
A1_BLOCKS = 8
A1_SKEW = 4
A1_SLOTS = 6
A1_GATE_LAG = 3

E_FIN, E_FGATE, E_MGATE, E_QA, E_KVA, E_KPE, E_END = 0, 512, 1024, 1536, 1920, 2176, 2304
O_Q, O_K, O_V, O_G, O_END = 0, 1024, 1280, 1536, 2560

NEG_BIG = -1e30
V7X_VMEM_BYTES = 64 * 1024 * 1024
VMEM_LIMIT = V7X_VMEM_BYTES * 7 // 8


def _cparams(sem):
    return pltpu.CompilerParams(dimension_semantics=sem, vmem_limit_bytes=VMEM_LIMIT)


def _dot(a, b):
    return jnp.dot(a, b, preferred_element_type=F32)


def _dot_nt(a, b):
    return lax.dot_general(a, b, (((1,), (1,)), ((), ())), preferred_element_type=F32)


def _rms(x, g):
    return x * lax.rsqrt(jnp.mean(x * x, axis=-1, keepdims=True) + EPS) * g


def _silu(x):
    return x * jax.nn.sigmoid(x)


def _rope(x, tab_ref, shift):
    up = pltpu.roll(x, LANES - shift, 1)
    dn = pltpu.roll(x, shift, 1)
    return x * tab_ref[0] + up * tab_ref[1] + dn * tab_ref[2]


def _full(shape):
    nd = len(shape)
    return pl.BlockSpec(shape, lambda *_: (0,) * nd)


def _axial_rope_tables(n, rot_dim):
    rows = n // GRID_W
    row = np.repeat(np.arange(rows), GRID_W).astype(np.float64)
    col = np.tile(np.arange(GRID_W), rows).astype(np.float64)
    nf = rot_dim // 4
    inv = (np.float32(ROPE_BASE) ** (-np.arange(nf, dtype=np.float32) / np.float32(nf))).astype(np.float64)
    ang = np.concatenate([row[:, None] * inv, col[:, None] * inv], axis=-1)
    return np.cos(ang), np.sin(ang)


def _mla_rope_tab(cos, sin):
    n = cos.shape[0]
    z = lambda w: np.zeros((n, w))
    cf = np.concatenate([np.ones((n, MLA_NOPE)), cos, cos, z(32)], axis=-1)
    s1 = np.concatenate([z(MLA_NOPE), -sin, z(16), z(32)], axis=-1)
    s2 = np.concatenate([z(MLA_NOPE), z(16), sin, z(32)], axis=-1)
    return _const(np.stack([cf, s1, s2]))


def _gqa_rope_tab(cos, sin):
    z = np.zeros_like(sin)
    cf = np.concatenate([cos, cos, z, z], axis=-1)
    s1 = np.concatenate([-sin, z, z, z], axis=-1)
    s2 = np.concatenate([z, sin, z, z], axis=-1)
    return _const(np.stack([cf, s1, s2]))


def _const(a):
    return jnp.asarray(np.ascontiguousarray(a, dtype=np.float32))


def _dft_cs(n):
    idx = np.arange(n, dtype=np.int64)
    ang = 2.0 * np.pi * ((idx[:, None] * idx[None, :]) % n).astype(np.float64) / n
    return np.cos(ang), np.sin(ang)


def _mod_kernel(s_ref, w_ref, b_ref, o_ref):
    s = _silu(s_ref[...])
    o_ref[0] = _dot(s.astype(BF16), w_ref[0].astype(BF16)) + b_ref[0]


def _mod_call(svec, w_mod, b_mod):
    depth, d, d3 = w_mod.shape
    tn = 768
    return pl.pallas_call(
        _mod_kernel,
        grid=(depth, d3 // tn),
        in_specs=[
            pl.BlockSpec((8, d), lambda l, j: (0, 0)),
            pl.BlockSpec((1, d, tn), lambda l, j: (l, 0, j)),
            pl.BlockSpec((1, 1, tn), lambda l, j: (l, 0, j)),
        ],
        out_specs=pl.BlockSpec((1, 8, tn), lambda l, j: (l, 0, j)),
        out_shape=jax.ShapeDtypeStruct((depth, 8, d3), F32),
        compiler_params=_cparams(("arbitrary", "arbitrary")),
        name="mod",
    )(svec, w_mod, b_mod.reshape(depth, 1, d3))


def _even_front(x, ng, sc, sh, w_in_ref, qn, w_qbt_ref, kvn, w_k_ref, w_vt_ref, vone, tqt_ref, tk_ref, fc_ref):
    m = x.shape[0]
    h = (_rms(x, ng) * (1.0 + sc) + sh).astype(BF16)
    f_in = _dot(h, w_in_ref[:, E_FIN:E_FGATE])
    f_gate = _dot(h, w_in_ref[:, E_FGATE:E_MGATE])
    m_gate = _dot(h, w_in_ref[:, E_MGATE:E_QA])
    low_rank = _dot(h, w_in_ref[:, E_QA:E_END])
    q_a = low_rank[:, 0:E_KVA - E_QA]
    kv_a = low_rank[:, E_KVA - E_QA:E_KPE - E_QA]
    kpe = low_rank[:, E_KPE - E_QA:E_END - E_QA]
    qh = _rms(q_a, qn).astype(BF16)
    ch = _rms(kv_a, kvn).astype(BF16)
    kpe_r = _rope(kpe, tk_ref, 16)
    qt = _dot_nt(w_qbt_ref[...], qh)
    k_all = _dot(ch, w_k_ref[...])
    cos = tqt_ref[0]
    sin = tqt_ref[1]
    pad = jnp.zeros((LANES - MLA_NOPE - MLA_ROPE, m), F32)
    qts, ks = [], []
    for hd in range(MLA_HEADS):
        lo = hd * LANES
        ql = hd * (MLA_NOPE + MLA_ROPE)
        x1 = qt[ql + MLA_NOPE:ql + MLA_NOPE + 16]
        x2 = qt[ql + MLA_NOPE + 16:ql + MLA_NOPE + 32]
        qts.append(jnp.concatenate(
            [qt[ql:ql + MLA_NOPE] * (MLA_SCALE * LOG2E), x1 * cos - x2 * sin, x2 * cos + x1 * sin, pad], axis=0))
        ks.append(k_all[:, lo:lo + LANES] + kpe_r)
    vt = _dot_nt(w_vt_ref[...], ch) + vone
    zr, zi = [], []
    fb = f_in.astype(BF16)
    for g in range(F_GROUPS):
        z = _dot(fb[:, g * LANES:(g + 1) * LANES], fc_ref[...])
        zr.append(z[:, :LANES])
        zi.append(z[:, LANES:])
    return qts, ks, vt, zr, zi, f_gate, m_gate


def _front0_kernel(x_ref, ng_ref, sc_ref, sh_ref, w_in_ref, qn_ref, w_qbt_ref, kvn_ref, w_k_ref, w_vt_ref,
                   vone_ref, tqt_ref, tk_ref, fc_ref, q_ref, k_ref, v_ref, zr_ref, zi_ref, gf_ref, gm_ref):
    qts, ks, vt, zr, zi, f_gate, m_gate = _even_front(
        x_ref[0], ng_ref[...], sc_ref[0], sh_ref[0], w_in_ref, qn_ref[...], w_qbt_ref, kvn_ref[...],
        w_k_ref, w_vt_ref, vone_ref[...], tqt_ref, tk_ref, fc_ref)
    for hd in range(MLA_HEADS):
        q_ref[0, hd] = qts[hd].astype(BF16)
        k_ref[0, hd] = ks[hd].astype(BF16)
        vh = vt[hd * MLA_VT_ROWS:(hd + 1) * MLA_VT_ROWS].astype(BF16)
        for ck in range(v_ref.shape[2]):
            v_ref[0, hd, ck] = vh[:, ck * MLA_VT_CHUNK:(ck + 1) * MLA_VT_CHUNK]
    for g in range(F_GROUPS):
        zr_ref[0, :, g * LANES:(g + 1) * LANES] = zr[g].astype(BF16)
        zi_ref[0, :, g * LANES:(g + 1) * LANES] = zi[g].astype(BF16)
    gf_ref[0] = _silu(f_gate).astype(BF16)
    gm_ref[0] = _silu(m_gate).astype(BF16)


def _front0_call(x, ng, sc, sh, w_in, qn, w_qbt, kvn, w_k, w_vt, vone, tqt, tk, fc, tm):
    b, n, d = x.shape
    fw = jax.ShapeDtypeStruct((b, n, F_WIDTH), BF16)
    row_spec = pl.BlockSpec((1, tm, F_WIDTH), lambda bi, i: (bi, i, 0))
    vec_spec = pl.BlockSpec((1, 1, d), lambda bi, i: (bi, 0, 0))
    return pl.pallas_call(
        _front0_kernel,
        grid=(b, n // tm),
        in_specs=[
            pl.BlockSpec((1, tm, d), lambda bi, i: (bi, i, 0)),
            _full(ng.shape), vec_spec, vec_spec,
            _full(w_in.shape), _full(qn.shape), _full(w_qbt.shape), _full(kvn.shape), _full(w_k.shape),
            _full(w_vt.shape), _full(vone.shape),
            pl.BlockSpec((2, 16, tm), lambda bi, i: (0, 0, i)),
            pl.BlockSpec((3, tm, LANES), lambda bi, i: (0, i, 0)),
            _full(fc.shape),
        ],
        out_specs=[
            pl.BlockSpec((1, MLA_HEADS, LANES, tm), lambda bi, i: (bi, 0, 0, i)),
            pl.BlockSpec((1, MLA_HEADS, tm, LANES), lambda bi, i: (bi, 0, i, 0)),
            pl.BlockSpec((1, MLA_HEADS, tm // MLA_VT_CHUNK, MLA_VT_ROWS, MLA_VT_CHUNK),
                         lambda bi, i: (bi, 0, i, 0, 0)),
            row_spec, row_spec, row_spec, row_spec],
        out_shape=[
            jax.ShapeDtypeStruct((b, MLA_HEADS, LANES, n), BF16),
            jax.ShapeDtypeStruct((b, MLA_HEADS, n, LANES), BF16),
            jax.ShapeDtypeStruct((b, MLA_HEADS, n // MLA_VT_CHUNK, MLA_VT_ROWS, MLA_VT_CHUNK), BF16),
            fw, fw, fw, fw],
        compiler_params=_cparams(("parallel", "arbitrary")),
        name="front0",
    )(x, ng, sc, sh, w_in, qn, w_qbt, kvn, w_k, w_vt, vone, tqt, tk, fc)


def _zero_after(x):
    u = lax.bitcast_convert_type(x, jnp.uint32)
    z = lax.shift_right_logical(lax.shift_right_logical(u, jnp.uint32(16)), jnp.uint32(16))
    return lax.bitcast_convert_type(z, F32)


def _mla_chunk(qt, k, vt, m, acc):
    s = _dot(k, qt)
    m_new = jnp.maximum(m, jnp.max(s, axis=0, keepdims=True))
    alpha = jnp.exp2(m - m_new)
    p = jnp.exp2(s - m_new).astype(BF16)
    return m_new, alpha * acc + _dot(vt, p)


def _mla_pair_out(acc0, acc1):
    o0 = acc0[0:MLA_V] / acc0[MLA_V:MLA_V + 1]
    o1 = acc1[0:MLA_V] / acc1[MLA_V:MLA_V + 1]
    return jnp.concatenate([o0, o1], axis=0).T


def _ctx_kernel(ctx_scale, x_ref, ng0_ref, sc0_ref, sh0_ref, g0_ref, ng1_ref, sc1_ref, sh1_ref,
                w_in_ref, qn_ref, w_qbt_ref, kvn_ref, w_k_ref, w_vt_ref, vone_ref, tqt_ref, tk_ref, fc_ref,
                fn_ref, w_out_ref, w_k1_ref, w_vt1_ref, vone1_ref,
                kc_ref, vc_ref, ck1_ref, cv1_ref):
    x = x_ref[0]
    c = x.shape[0]
    qts, ks, vt, zr, zi, f_gate, m_gate = _even_front(
        x, ng0_ref[...], sc0_ref[...], sh0_ref[...], w_in_ref, qn_ref[...], w_qbt_ref, kvn_ref[...],
        w_k_ref, w_vt_ref, vone_ref[...], tqt_ref, tk_ref, fc_ref)
    pairs = []
    for hp in range(MLA_HEADS // 2):
        accs = []
        for e in range(2):
            hd = 2 * hp + e
            kb = ks[hd].astype(BF16)
            vtb = vt[hd * MLA_VT_ROWS:(hd + 1) * MLA_VT_ROWS].astype(BF16)
            kc_ref[0, hd] = kb
            vc_ref[0, hd, 0] = vtb
            m0 = jnp.full((1, c), -jnp.inf, F32)
            acc0 = jnp.zeros((MLA_VT_ROWS, c), F32)
            accs.append(_mla_chunk(qts[hd].astype(BF16), kb, vtb, m0, acc0)[1])
        pairs.append(_mla_pair_out(accs[0], accs[1]))
    a = jnp.concatenate(pairs, axis=-1)
    z = jnp.concatenate([jnp.concatenate(zr, axis=-1), jnp.concatenate(zi, axis=-1)], axis=0).astype(BF16)
    fm = _dot(fn_ref[...], z) * ctx_scale
    yf = (fm * _silu(f_gate)).astype(BF16)
    ya = (a * _silu(m_gate)).astype(BF16)
    y = _dot(yf, w_out_ref[0:F_WIDTH, :]) + _dot(ya, w_out_ref[F_WIDTH:, :])
    x1 = x + g0_ref[...] * y
    h1 = (_rms(x1, ng1_ref[...]) * (1.0 + sc1_ref[...]) + sh1_ref[...]).astype(BF16)
    vt1 = _dot_nt(w_vt1_ref[...], h1) + vone1_ref[...]
    for g in range(GQA_KV_HEADS):
        ck1_ref[0, g] = _dot(h1, w_k1_ref[:, g * LANES:(g + 1) * LANES]).astype(BF16)
        cv1_ref[0, g] = vt1[g * GQA_VT_ROWS:(g + 1) * GQA_VT_ROWS].astype(BF16)


def _ctx_call(ctx, vecs0, vecs1, w_in, qn, w_qbt, kvn, w_k, w_vt, vone, tqt, tk, fc, fn, w_out, w_k1, w_vt1,
              vone1):
    b, c, d = ctx.shape
    ng0, sc0, sh0, g0 = vecs0
    ng1, sc1, sh1 = vecs1
    ctx_scale = 1.0 / math.sqrt(c * F_GROUP_DIM)
    consts = [ng0, sc0, sh0, g0, ng1, sc1, sh1, w_in, qn, w_qbt, kvn, w_k, w_vt, vone, tqt, tk, fc, fn, w_out,
              w_k1, w_vt1, vone1]
    return pl.pallas_call(
        functools.partial(_ctx_kernel, ctx_scale),
        grid=(b,),
        in_specs=[pl.BlockSpec((1, c, d), lambda bi: (bi, 0, 0))] + [_full(a.shape) for a in consts],
        out_specs=[pl.BlockSpec((1, MLA_HEADS, c, LANES), lambda bi: (bi, 0, 0, 0)),
                   pl.BlockSpec((1, MLA_HEADS, 1, MLA_VT_ROWS, c), lambda bi: (bi, 0, 0, 0, 0)),
                   pl.BlockSpec((1, GQA_KV_HEADS, c, LANES), lambda bi: (bi, 0, 0, 0)),
                   pl.BlockSpec((1, GQA_KV_HEADS, GQA_VT_ROWS, c), lambda bi: (bi, 0, 0, 0))],
        out_shape=[jax.ShapeDtypeStruct((b, MLA_HEADS, c, LANES), BF16),
                   jax.ShapeDtypeStruct((b, MLA_HEADS, 1, MLA_VT_ROWS, c), BF16),
                   jax.ShapeDtypeStruct((b, GQA_KV_HEADS, c, LANES), BF16),
                   jax.ShapeDtypeStruct((b, GQA_KV_HEADS, GQA_VT_ROWS, c), BF16)],
        compiler_params=_cparams(("arbitrary",)),
        name="ctx",
    )(ctx, *consts)


def _attn0_kernel(qt_ref, kc_ref, vct_ref, kl_ref, vlt_ref, gm_ref, o_ref, s_ref):
    tq = qt_ref.shape[-1]
    nch, _, tk = vlt_ref.shape[2:]
    nc = kc_ref.shape[2]
    n_ctx = nc // PV_KEYS
    n_sub = n_ctx + nch * tk // PV_KEYS
    group = [0] * n_ctx + [1 + j // SOFTMAX_GROUP for j in range(n_sub - n_ctx)]
    first = [t for t in range(n_sub) if t == 0 or group[t] != group[t - 1]]
    last = [t for t in range(n_sub) if t == n_sub - 1 or group[t] != group[t + 1]]

    def keys(e, t):
        if t < n_ctx:
            return kc_ref[0, e, t * PV_KEYS:(t + 1) * PV_KEYS, :]
        lo = (t - n_ctx) * PV_KEYS
        return kl_ref[0, e, lo:lo + PV_KEYS, :]

    def values_t(e, t):
        if t < n_ctx:
            return vct_ref[0, e, 0, :, t * PV_KEYS:(t + 1) * PV_KEYS]
        j, off = divmod((t - n_ctx) * PV_KEYS, tk)
        return vlt_ref[0, e, j, :, off:off + PV_KEYS]

    def slot(i):
        r = i % SCORE_SLOTS
        return slice(r * PV_KEYS, (r + 1) * PV_KEYS)

    tasks = [(qi, t) for qi in range(tq // MLA_TQ) for t in range(n_sub)]
    colmax = []
    for step in range(len(tasks) + PIPE_SKEW):
        if step < len(tasks):
            qi, t = tasks[step]
            cols = slice(qi * MLA_TQ, (qi + 1) * MLA_TQ)
            cm = []
            for e in range(2):
                s = _dot(keys(e, t), qt_ref[0, e, :, cols])
                s_ref[e, slot(step)] = s
                cm.append(jnp.max(s, axis=0, keepdims=True))
            colmax.append(cm)
        if step < PIPE_SKEW:
            continue
        qi, u = tasks[step - PIPE_SKEW]
        if u == 0:
            m = [jnp.full((1, MLA_TQ), -jnp.inf, F32) for _ in range(2)]
            acc = [jnp.zeros((MLA_VT_ROWS, MLA_TQ), F32) for _ in range(2)]
        if u in first:
            members = [qi * n_sub + t for t in range(n_sub) if group[t] == group[u]]
            assert members[-1] <= step
            m_new, alpha = [], []
            for e in range(2):
                mc = functools.reduce(jnp.maximum, [colmax[i][e] for i in members])
                m_new.append(jnp.maximum(m[e], mc))
                alpha.append(jnp.exp2(m[e] - m_new[-1]))
            pv = [None, None]
        gate = min(step - GATE_LAG, len(tasks) - 1)
        for e in range(2):
            m_use = m_new[e] + _zero_after(colmax[gate][e])
            p = jnp.exp2(s_ref[e, slot(step - PIPE_SKEW)] - m_use).astype(BF16)
            d = _dot(values_t(e, u), p)
            pv[e] = d if pv[e] is None else pv[e] + d
        if u in last:
            for e in range(2):
                acc[e] = alpha[e] * acc[e] + pv[e]
                m[e] = m_new[e]
        if u == n_sub - 1:
            rows = slice(qi * MLA_TQ, (qi + 1) * MLA_TQ)
            o_ref[0, rows] = (_mla_pair_out(acc[0], acc[1]) * gm_ref[0, rows].astype(F32)).astype(BF16)


def _attn0_call(qt, kc, vct, kl, vlt, gm, tq):
    b, h, _, n = qt.shape
    c = kc.shape[2]
    nch, _, tk = vlt.shape[2:]
    assert c % PV_KEYS == 0 and tk % PV_KEYS == 0 and SCORE_SLOTS > PIPE_SKEW >= SOFTMAX_GROUP - 1
    assert tq % MLA_TQ == 0
    o_spec = pl.BlockSpec((1, tq, LANES), lambda bi, hp, i: (bi, i, hp))
    return pl.pallas_call(
        _attn0_kernel,
        grid=(b, h // 2, n // tq),
        in_specs=[
            pl.BlockSpec((1, 2, LANES, tq), lambda bi, hp, i: (bi, hp, 0, i)),
            pl.BlockSpec((1, 2, c, LANES), lambda bi, hp, i: (bi, hp, 0, 0)),
            pl.BlockSpec((1, 2, 1, MLA_VT_ROWS, c), lambda bi, hp, i: (bi, hp, 0, 0, 0)),
            pl.BlockSpec((1, 2, n, LANES), lambda bi, hp, i: (bi, hp, 0, 0)),
            pl.BlockSpec((1, 2, nch, MLA_VT_ROWS, tk), lambda bi, hp, i: (bi, hp, 0, 0, 0)),
            o_spec],
        out_specs=o_spec,
        out_shape=jax.ShapeDtypeStruct((b, n, MLA_WIDTH), BF16),
        scratch_shapes=[pltpu.VMEM((2, SCORE_SLOTS * PV_KEYS, MLA_TQ), F32)],
        compiler_params=_cparams(("parallel", "parallel", "arbitrary")),
        name="attn0",
    )(qt, kc, vct, kl, vlt, gm)


def _swap16(p16, blocks):
    ys = [_dot(p16, blk).astype(BF16) for blk in blocks]
    return [jnp.concatenate([y[t * FFT_TILE:(t + 1) * FFT_TILE] for y in ys], axis=0) for t in range(FFT_TILE)]


def _fft_a_kernel(n1, zr_ref, zi_ref, p16_ref, fa_ref, tw_ref, tr_ref, ti_ref):
    rows = n1 * FFT_TILE
    z = jnp.concatenate([zr_ref[0].reshape(rows, F_WIDTH), zi_ref[0].reshape(rows, F_WIDTH)], axis=1)
    grp = FFT_TILE * FFT_TILE
    per_r = _swap16(p16_ref[...], [z[j * grp:(j + 1) * grp] for j in range(rows // grp)])
    for r in range(FFT_TILE):
        zz = per_r[r]
        a = _dot(fa_ref[...], jnp.concatenate([zz[:, :F_WIDTH], zz[:, F_WIDTH:]], axis=0))
        c = tw_ref[0, r]
        s = tw_ref[1, r]
        for g in range(F_GROUPS):
            ar = a[0:n1, g * LANES:(g + 1) * LANES]
            ai = a[n1:2 * n1, g * LANES:(g + 1) * LANES]
            tr_ref[0, r, :, g * LANES:(g + 1) * LANES] = (ar * c + ai * s).astype(BF16)
            ti_ref[0, r, :, g * LANES:(g + 1) * LANES] = (ai * c - ar * s).astype(BF16)


def _fft_a_call(zr, zi, p16, fa, tw):
    b, n1, r, w = zr.shape
    in_spec = pl.BlockSpec((1, n1, FFT_TILE, w), lambda bi, j: (bi, 0, j, 0))
    out_spec = pl.BlockSpec((1, FFT_TILE, n1, w), lambda bi, j: (bi, j, 0, 0))
    out = jax.ShapeDtypeStruct((b, r, n1, w), BF16)
    return pl.pallas_call(
        functools.partial(_fft_a_kernel, n1),
        grid=(b, r // FFT_TILE),
        in_specs=[in_spec, in_spec, _full(p16.shape), _full(fa.shape),
                  pl.BlockSpec((2, FFT_TILE, n1, LANES), lambda bi, j: (0, j, 0, 0))],
        out_specs=[out_spec, out_spec],
        out_shape=[out, out],
        compiler_params=_cparams(("parallel", "arbitrary")),
        name="fft_a",
    )(zr, zi, p16, fa, tw)


def _fft_c_kernel(tr_ref, ti_ref, p16_ref, fcs_ref, gf_ref, o_ref):
    nr = tr_ref.shape[1]
    rows = nr * FFT_TILE
    grp = FFT_TILE * FFT_TILE
    t = jnp.concatenate([tr_ref[0].reshape(rows, F_WIDTH), ti_ref[0].reshape(rows, F_WIDTH)], axis=1)
    per_k = _swap16(p16_ref[...], [t[j * grp:(j + 1) * grp] for j in range(rows // grp)])
    outs = []
    for k in range(FFT_TILE):
        tt = per_k[k]
        rhs = jnp.concatenate([tt[:, :F_WIDTH], tt[:, F_WIDTH:]], axis=0)
        outs.append(_dot(fcs_ref[...], rhs).astype(BF16))
    gf = gf_ref[0].reshape(rows, F_WIDTH)
    for j in range(nr // FFT_TILE):
        blk = jnp.concatenate([o[j * FFT_TILE:(j + 1) * FFT_TILE] for o in outs], axis=0)
        y = _dot(p16_ref[...], blk)
        o_ref[0, j * FFT_TILE:(j + 1) * FFT_TILE] = (
            y * gf[j * grp:(j + 1) * grp].astype(F32)).astype(BF16).reshape(FFT_TILE, FFT_TILE, F_WIDTH)


def _fft_c_call(tr, ti, p16, fcs, gf):
    b, r, n1, w = tr.shape
    spec = pl.BlockSpec((1, r, FFT_TILE, w), lambda bi, i: (bi, 0, i, 0))
    return pl.pallas_call(
        _fft_c_kernel,
        grid=(b, n1 // FFT_TILE),
        in_specs=[spec, spec, _full(p16.shape), _full(fcs.shape), spec],
        out_specs=spec,
        out_shape=jax.ShapeDtypeStruct((b, r, n1, w), BF16),
        compiler_params=_cparams(("parallel", "arbitrary")),
        name="fft_c",
    )(tr, ti, p16, fcs, gf)


def _mid_kernel(fm_ref, ag_ref, x_ref, g0_ref, w_out_ref, ng_ref, sc_ref, sh_ref, w_qt_ref, w_k_ref, w_vt_ref,
                vone_ref, w_g_ref, tqt_ref, tk_ref, x1_ref, q_ref, k_ref, v_ref, sg_ref):
    tm = x_ref.shape[1]
    y = _dot(fm_ref[0], w_out_ref[0:F_WIDTH, :]) + _dot(ag_ref[0], w_out_ref[F_WIDTH:, :])
    x1 = x_ref[0] + g0_ref[0] * y
    x1_ref[0] = x1
    h = (_rms(x1, ng_ref[...]) * (1.0 + sc_ref[0]) + sh_ref[0]).astype(BF16)
    qt = _dot_nt(w_qt_ref[...], h)
    cos = tqt_ref[0]
    sin = tqt_ref[1]
    half = GQA_HEAD_DIM // 2
    for hd in range(GQA_HEADS):
        lo = hd * GQA_HEAD_DIM
        x1r = qt[lo:lo + half]
        x2r = qt[lo + half:lo + GQA_HEAD_DIM]
        qh = jnp.concatenate([x1r * cos - x2r * sin, x2r * cos + x1r * sin], axis=0).astype(BF16)
        g, j = divmod(hd, GQA_GROUP)
        for blk in range(tm // BLOCK):
            q_ref[0, blk, g, :, j * BLOCK:(j + 1) * BLOCK] = qh[:, blk * BLOCK:(blk + 1) * BLOCK]
    vt = _dot_nt(w_vt_ref[...], h) + vone_ref[...]
    k_all = _dot(h, w_k_ref[...])
    for g in range(GQA_KV_HEADS):
        k_ref[0, g] = _rope(k_all[:, g * LANES:(g + 1) * LANES], tk_ref, 32).astype(BF16)
        v_ref[0, g] = vt[g * GQA_VT_ROWS:(g + 1) * GQA_VT_ROWS].astype(BF16)
    sg_ref[0] = _silu(_dot(h, w_g_ref[...])).astype(BF16)


def _mid_call(fm, ag, x, g0, w_out, ng, sc, sh, w_qt, w_k, w_vt, vone, w_g, tqt, tk, tm):
    b, n, d = x.shape
    row = lambda w: pl.BlockSpec((1, tm, w), lambda bi, i: (bi, i, 0))
    vec_spec = pl.BlockSpec((1, 1, d), lambda bi, i: (bi, 0, 0))
    nblk = tm // BLOCK
    return pl.pallas_call(
        _mid_kernel,
        grid=(b, n // tm),
        in_specs=[row(F_WIDTH), row(MLA_WIDTH), row(d), vec_spec, _full(w_out.shape), _full(ng.shape),
                  vec_spec, vec_spec, _full(w_qt.shape), _full(w_k.shape), _full(w_vt.shape), _full(vone.shape),
                  _full(w_g.shape),
                  pl.BlockSpec((2, GQA_HEAD_DIM // 2, tm), lambda bi, i: (0, 0, i)),
                  pl.BlockSpec((3, tm, LANES), lambda bi, i: (0, i, 0))],
        out_specs=[row(d),
                   pl.BlockSpec((1, nblk, GQA_KV_HEADS, GQA_HEAD_DIM, GQA_GROUP * BLOCK),
                                lambda bi, i: (bi, i, 0, 0, 0)),
                   pl.BlockSpec((1, GQA_KV_HEADS, tm, LANES), lambda bi, i: (bi, 0, i, 0)),
                   pl.BlockSpec((1, GQA_KV_HEADS, GQA_VT_ROWS, tm), lambda bi, i: (bi, 0, 0, i)),
                   row(GQA_Q)],
        out_shape=[jax.ShapeDtypeStruct((b, n, d), F32),
                   jax.ShapeDtypeStruct((b, n // BLOCK, GQA_KV_HEADS, GQA_HEAD_DIM, GQA_GROUP * BLOCK), BF16),
                   jax.ShapeDtypeStruct((b, GQA_KV_HEADS, n, LANES), BF16),
                   jax.ShapeDtypeStruct((b, GQA_KV_HEADS, GQA_VT_ROWS, n), BF16),
                   jax.ShapeDtypeStruct((b, n, GQA_Q), BF16)],
        compiler_params=_cparams(("parallel", "arbitrary")),
        name="mid",
    )(fm, ag, x, g0, w_out, ng, sc, sh, w_qt, w_k, w_vt, vone, w_g, tqt, tk)


def _band_bias(nc):
    col = np.arange(3 * BLOCK)[:, None]
    q = np.arange(BLOCK)[None, :]
    dist = BLOCK + q - col
    band = np.abs(dist) <= WINDOW
    variants = [band & (col >= BLOCK), band, band & (col < 2 * BLOCK)]
    out = np.zeros((3, nc + 3 * BLOCK, BLOCK), np.float32)
    for v, ok in enumerate(variants):
        out[v, nc:] = np.where(ok, 0.0, NEG_BIG)
    return jnp.asarray(out)


def _attn1_kernel(sink_ref, q_ref, kp_ref, kc_ref, kn_ref, vp_ref, vc_ref, vn_ref, ck_ref, cv_ref, bias_ref,
                  sg_ref, x1_ref, g1_ref, w_out_ref, fg_ref, o_ref, s_ref):
    i = pl.program_id(1)
    last = pl.num_programs(1) - 1
    nc = ck_ref.shape[2]
    sel = []
    for blk in range(A1_BLOCKS):
        v = 1
        if blk == 0:
            v = jnp.where(i == 0, 0, v)
        if blk == A1_BLOCKS - 1:
            v = jnp.where(i == last, 2, v)
        sel.append(v)

    def band_keys(g, blk, j):
        pos = blk + j - 1
        if pos < 0:
            return kp_ref[0, g], vp_ref[0, g]
        if pos >= A1_BLOCKS:
            return kn_ref[0, g], vn_ref[0, g]
        return (kc_ref[0, g, pos * BLOCK:(pos + 1) * BLOCK, :], vc_ref[0, g, :, pos * BLOCK:(pos + 1) * BLOCK])

    def operands(blk, g, sb):
        if sb == 0:
            return ck_ref[0, g][:, :GQA_HEAD_DIM], cv_ref[0, g], None
        if sb == 1:
            (k0, v0), (k1, v1) = band_keys(g, blk, 0), band_keys(g, blk, 1)
            return (jnp.concatenate([k0, k1], axis=0)[:, :GQA_HEAD_DIM], jnp.concatenate([v0, v1], axis=1),
                    bias_ref[sel[blk], nc:nc + 2 * BLOCK])
        k2, v2 = band_keys(g, blk, 2)
        return k2[:, :GQA_HEAD_DIM], v2, bias_ref[sel[blk], nc + 2 * BLOCK:nc + 3 * BLOCK]

    def slot(idx, rows):
        r = idx % A1_SLOTS
        return slice(r * 2 * BLOCK, r * 2 * BLOCK + rows)

    tasks = [(blk, g, sb) for blk in range(A1_BLOCKS) for g in range(GQA_KV_HEADS) for sb in range(3)]
    colmax = []
    chunks = {}
    for step in range(len(tasks) + A1_SKEW):
        if step < len(tasks):
            blk, g, sb = tasks[step]
            kb, _, bias = operands(blk, g, sb)
            s = _dot(kb, q_ref[0, blk, g])
            if bias is not None:
                s = s + jnp.concatenate([bias] * GQA_GROUP, axis=1)
            s_ref[slot(step, s.shape[0])] = s
            colmax.append(jnp.max(s, axis=0, keepdims=True))
        if step < A1_SKEW:
            continue
        idx = step - A1_SKEW
        blk, g, sb = tasks[idx]
        _, vt, _ = operands(blk, g, sb)
        if sb == 0:
            sink = jnp.concatenate(
                [jnp.full((1, BLOCK), sink_ref[GQA_GROUP * g + j] * LOG2E, F32) for j in range(GQA_GROUP)], axis=1)
            m = jnp.maximum(colmax[idx], sink)
            acc = None
        if sb == 1:
            assert idx + 1 <= step
            m_prev = m
            m = functools.reduce(jnp.maximum, [m_prev, colmax[idx], colmax[idx + 1]])
            acc = jnp.exp2(m_prev - m) * acc
        gate = min(step - A1_GATE_LAG, len(tasks) - 1)
        m_use = m + _zero_after(colmax[gate])
        p = jnp.exp2(s_ref[slot(idx, vt.shape[1])] - m_use).astype(BF16)
        d = _dot(vt, p)
        acc = d if acc is None else acc + d
        if sb == 2:
            l = acc[GQA_HEAD_DIM:GQA_HEAD_DIM + 1] + jnp.exp2(sink - m)
            ot = acc[0:GQA_HEAD_DIM] / l
            for p2 in range(2):
                lo = 2 * p2 * BLOCK
                pair = jnp.concatenate([ot[:, lo:lo + BLOCK], ot[:, lo + BLOCK:lo + 2 * BLOCK]], axis=0)
                chunks[(blk, 2 * g + p2)] = pair.T
            if g == GQA_KV_HEADS - 1:
                rows = slice(blk * BLOCK, (blk + 1) * BLOCK)
                o = jnp.concatenate([chunks[(blk, c)] for c in range(GQA_Q // LANES)], axis=-1)
                og = (o * sg_ref[0, rows].astype(F32)).astype(BF16)
                y = _dot(og, w_out_ref[...])
                x2 = x1_ref[0, rows] + g1_ref[0] * y
                o_ref[0, rows] = _rms(x2, fg_ref[...])


def _attn1_call(sink, qt, k, vt, ck, cvt, bias, sg, x1, g1, w_out, fg):
    b, n, d = x1.shape
    nc = ck.shape[2]
    nb = n // BLOCK
    ns = nb // A1_BLOCKS
    assert nb % A1_BLOCKS == 0 and nc == 2 * BLOCK and A1_SLOTS > A1_SKEW >= 2
    tr = A1_BLOCKS * BLOCK
    row = lambda w: pl.BlockSpec((1, tr, w), lambda bi, i: (bi, i, 0))
    prv = lambda i: jnp.maximum(i * A1_BLOCKS - 1, 0)
    nxt = lambda i: jnp.minimum((i + 1) * A1_BLOCKS, nb - 1)
    k_spec = lambda f: pl.BlockSpec((1, GQA_KV_HEADS, BLOCK, LANES), lambda bi, i: (bi, 0, f(i), 0))
    v_spec = lambda f: pl.BlockSpec((1, GQA_KV_HEADS, GQA_VT_ROWS, BLOCK), lambda bi, i: (bi, 0, 0, f(i)))
    vec_spec = pl.BlockSpec((1, 1, d), lambda bi, i: (bi, 0, 0))
    return pl.pallas_call(
        _attn1_kernel,
        grid=(b, ns),
        in_specs=[pl.BlockSpec(memory_space=pltpu.SMEM),
                  pl.BlockSpec((1, A1_BLOCKS, GQA_KV_HEADS, GQA_HEAD_DIM, GQA_GROUP * BLOCK),
                               lambda bi, i: (bi, i, 0, 0, 0)),
                  k_spec(prv), pl.BlockSpec((1, GQA_KV_HEADS, tr, LANES), lambda bi, i: (bi, 0, i, 0)), k_spec(nxt),
                  v_spec(prv), pl.BlockSpec((1, GQA_KV_HEADS, GQA_VT_ROWS, tr), lambda bi, i: (bi, 0, 0, i)),
                  v_spec(nxt),
                  pl.BlockSpec((1, GQA_KV_HEADS, nc, LANES), lambda bi, i: (bi, 0, 0, 0)),
                  pl.BlockSpec((1, GQA_KV_HEADS, GQA_VT_ROWS, nc), lambda bi, i: (bi, 0, 0, 0)),
                  _full(bias.shape), row(GQA_Q), row(d), vec_spec, _full(w_out.shape), _full(fg.shape)],
        out_specs=row(d),
        out_shape=jax.ShapeDtypeStruct((b, n, d), F32),
        scratch_shapes=[pltpu.VMEM((A1_SLOTS * 2 * BLOCK, GQA_GROUP * BLOCK), F32)],
        compiler_params=_cparams(("parallel", "arbitrary")),
        name="attn1",
    )(sink, qt, k, k, k, vt, vt, vt, ck, cvt, bias, sg, x1, g1, w_out, fg)


def _prep_even_weights(w_in, w_qb, w_kvb):
    d = w_in.shape[0]
    f_in, f_gate, q_a, kv_a, k_pe, m_gate = jnp.split(
        w_in, np.cumsum([F_WIDTH, F_WIDTH, MLA_Q_RANK, MLA_KV_RANK, MLA_ROPE]).tolist(), axis=1)
    kpe_blk = jnp.concatenate([jnp.zeros((d, MLA_NOPE), F32), k_pe, jnp.zeros((d, 32), F32)], axis=1)
    w_in_p = jnp.concatenate([f_in, f_gate, m_gate, q_a, kv_a, kpe_blk], axis=1).astype(BF16)
    w_qbt = w_qb.T.astype(BF16)
    kvb = w_kvb.reshape(MLA_KV_RANK, MLA_HEADS, MLA_NOPE + MLA_V)
    w_k = jnp.pad(kvb[:, :, :MLA_NOPE], ((0, 0), (0, 0), (0, 64))).reshape(MLA_KV_RANK, MLA_HEADS * LANES)
    w_vt = jnp.pad(kvb[:, :, MLA_NOPE:], ((0, 0), (0, 0), (0, MLA_VT_ROWS - MLA_V)))
    w_vt = w_vt.reshape(MLA_KV_RANK, MLA_HEADS * MLA_VT_ROWS).T.astype(BF16)
    vone = np.zeros((MLA_HEADS * MLA_VT_ROWS, 1), np.float32)
    vone[MLA_V::MLA_VT_ROWS] = 1.0
    return w_in_p, w_qbt, w_k.astype(BF16), w_vt, jnp.asarray(vone)


def _prep_odd_weights(w_in):
    d = w_in.shape[0]
    w_qt = w_in[:, O_Q:O_K].T.astype(BF16)
    kw = w_in[:, O_K:O_V].reshape(d, GQA_KV_HEADS, GQA_HEAD_DIM)
    w_k = jnp.pad(kw, ((0, 0), (0, 0), (0, LANES - GQA_HEAD_DIM))).reshape(d, GQA_KV_HEADS * LANES).astype(BF16)
    vw = w_in[:, O_V:O_G].reshape(d, GQA_KV_HEADS, GQA_HEAD_DIM)
    w_vt = jnp.pad(vw, ((0, 0), (0, 0), (0, GQA_VT_ROWS - GQA_HEAD_DIM)))
    w_vt = w_vt.reshape(d, GQA_KV_HEADS * GQA_VT_ROWS).T.astype(BF16)
    vone = np.zeros((GQA_KV_HEADS * GQA_VT_ROWS, 1), np.float32)
    vone[GQA_HEAD_DIM::GQA_VT_ROWS] = 1.0
    return w_qt, w_k, w_vt, jnp.asarray(vone), w_in[:, O_G:O_END].astype(BF16)


def _fft_consts(n, c):
    n1 = n // LANES
    c1, s1 = _dft_cs(n1)
    fa = np.block([[c1, s1], [-s1, c1]])
    c2, s2 = _dft_cs(LANES)
    fcs = np.concatenate([c2, s2], axis=1) / math.sqrt(n * F_GROUP_DIM)
    fc = np.concatenate([c2, -s2], axis=1)
    r = np.arange(LANES, dtype=np.int64)[:, None]
    k1 = np.arange(n1, dtype=np.int64)[None, :]
    ang = 2.0 * np.pi * ((r * k1) % n).astype(np.float64) / n
    tw = np.stack([np.cos(ang), np.sin(ang)])
    tw = np.broadcast_to(tw[..., None], (2, LANES, n1, LANES))
    cn, sn = _dft_cs(c)
    fn = np.concatenate([cn, sn], axis=1)
    as32 = lambda a: jnp.asarray(np.ascontiguousarray(a, dtype=np.float32))
    idx = np.arange(FFT_TILE * FFT_TILE)
    p16 = np.zeros((FFT_TILE * FFT_TILE,) * 2)
    p16[(idx % FFT_TILE) * FFT_TILE + idx // FFT_TILE, idx] = 1.0
    return (as32(fa).astype(BF16), as32(fcs).astype(BF16), as32(fc).astype(BF16), as32(tw), as32(fn).astype(BF16),
            as32(p16).astype(BF16))


def _pick_tile(n, pref):
    t = pref
    while n % t:
        t //= 2
    return t


def _tiles(n):
    return dict(front=_pick_tile(n, 1024), mid=_pick_tile(n, 1024), attn0=_pick_tile(n, 2 * MLA_TQ))


def kernel(x, c, ctx, c_ctx, w_mod, b_mod, norm_g, e_w_in, e_q_norm, e_w_qb, e_kv_norm, e_w_kvb, e_w_out,
           o_w_in, o_sink, o_w_out, final_g):
    b, n, d = x.shape
    nc = ctx.shape[1]
    assert d == D_MODEL and n % (LANES * FFT_TILE) == 0 and nc % LANES == 0 and b <= 7

    svec = jnp.zeros((8, d), F32).at[:b].set(c).at[b].set(c_ctx)
    mod = _mod_call(svec, w_mod, b_mod)
    sh, sc, gt = mod[:, :, :d], mod[:, :, d:2 * d], mod[:, :, 2 * d:]
    lat = lambda t, l: t[l, :b].reshape(b, 1, d)
    cvec = lambda t, l: t[l, b].reshape(1, d)
    ng0, ng1 = norm_g[0].reshape(1, d), norm_g[1].reshape(1, d)

    cos_m, sin_m = _axial_rope_tables(n, MLA_ROPE)
    qs = MLA_SCALE * LOG2E
    tqt0 = _const(np.stack([cos_m.T, sin_m.T]) * qs)
    tk0 = _mla_rope_tab(cos_m, sin_m)
    one_c, zero_c = np.ones((nc, MLA_ROPE // 2)), np.zeros((nc, MLA_ROPE // 2))
    tqt0c = _const(np.stack([one_c.T, zero_c.T]) * qs)
    tk0c = _mla_rope_tab(one_c, zero_c)
    cos_g, sin_g = _axial_rope_tables(n, GQA_HEAD_DIM)
    tqt1 = _const(np.stack([cos_g.T, sin_g.T]) * (GQA_SCALE * LOG2E))
    tk1 = _gqa_rope_tab(cos_g, sin_g)
    fa, fcs, fc, tw, fn, p16 = _fft_consts(n, nc)
    bias1 = _band_bias(nc)

    w_in0, w_qbt0, w_k0, w_vt0, vone = _prep_even_weights(e_w_in[0], e_w_qb[0], e_w_kvb[0])
    qn0 = e_q_norm[0].reshape(1, MLA_Q_RANK)
    kvn0 = e_kv_norm[0].reshape(1, MLA_KV_RANK)
    w_out0 = e_w_out[0].astype(BF16)
    w_qt1, w_k1, w_vt1, vone1, w_g1 = _prep_odd_weights(o_w_in[0])
    w_out1 = o_w_out[0].astype(BF16)

    kc, vct, ck1, cvt1 = _ctx_call(
        ctx, (ng0, cvec(sc, 0), cvec(sh, 0), cvec(gt, 0)), (ng1, cvec(sc, 1), cvec(sh, 1)),
        w_in0, qn0, w_qbt0, kvn0, w_k0, w_vt0, vone, tqt0c, tk0c, fc, fn, w_out0, w_k1, w_vt1, vone1)

    tiles = _tiles(n)
    qt0, k0, vt0, zr, zi, gf, gm = _front0_call(
        x, ng0, lat(sc, 0), lat(sh, 0), w_in0, qn0, w_qbt0, kvn0, w_k0, w_vt0, vone, tqt0, tk0, fc,
        tiles["front"])

    ag = _attn0_call(qt0, kc, vct, k0, vt0, gm, tiles["attn0"])

    n1 = n // LANES
    tr, ti = _fft_a_call(zr.reshape(b, n1, LANES, F_WIDTH), zi.reshape(b, n1, LANES, F_WIDTH), p16, fa, tw)
    fmg = _fft_c_call(tr, ti, p16, fcs, gf.reshape(b, LANES, n1, F_WIDTH))
    fmg = fmg.reshape(b, n, F_WIDTH)

    x1, qt1, k1, vt1, sg = _mid_call(fmg, ag, x, lat(gt, 0), w_out0, ng1, lat(sc, 1), lat(sh, 1),
                                     w_qt1, w_k1, w_vt1, vone1, w_g1, tqt1, tk1, tiles["mid"])

    return _attn1_call(o_sink[0], qt1, k1, vt1, ck1, cvt1, bias1, sg, x1, lat(gt, 1), w_out1,
                       final_g.reshape(1, d))
```

```python
import functools
import math

import numpy as np
import jax
import jax.numpy as jnp
from jax import lax
from jax.experimental import pallas as pl
from jax.experimental.pallas import tpu as pltpu

F32 = jnp.float32
BF16 = jnp.bfloat16

D_MODEL = 1024
GRID_W = 64
EPS = 1e-6
ROPE_BASE = 10000.0
LANES = 128

F_GROUPS = 4
F_GROUP_DIM = 128
F_WIDTH = F_GROUPS * F_GROUP_DIM
FFT_TILE = 16

MLA_HEADS = 8
MLA_NOPE = 64
MLA_ROPE = 32
MLA_V = 64
MLA_Q_RANK = 384
MLA_KV_RANK = 256
MLA_WIDTH = MLA_HEADS * MLA_V
MLA_SCALE = 1.0 / math.sqrt(MLA_NOPE + MLA_ROPE)
LOG2E = math.log2(math.e)
MLA_VT_ROWS = 80
PV_KEYS = 256
MLA_VT_CHUNK = 512
MLA_TQ = 256
SOFTMAX_GROUP = 2
PIPE_SKEW = 3
SCORE_SLOTS = 4
GATE_LAG = 2

GQA_HEADS = 16
GQA_KV_HEADS = 4
GQA_GROUP = GQA_HEADS // GQA_KV_HEADS
GQA_HEAD_DIM = 64
WINDOW = 128
BLOCK = 128
GQA_Q = GQA_HEADS * GQA_HEAD_DIM
GQA_KV = GQA_KV_HEADS * GQA_HEAD_DIM
GQA_SCALE = 1.0 / math.sqrt(GQA_HEAD_DIM)
GQA_VT_ROWS = 80
A1_BLOCKS = 8
A1_SKEW = 4
A1_SLOTS = 6
A1_GATE_LAG = 3

E_FIN, E_FGATE, E_MGATE, E_QA, E_KVA, E_KPE, E_END = 0, 512, 1024, 1536, 1920, 2176, 2304
O_Q, O_K, O_V, O_G, O_END = 0, 1024, 1280, 1536, 2560

NEG_BIG = -1e30
V7X_VMEM_BYTES = 64 * 1024 * 1024
VMEM_LIMIT = V7X_VMEM_BYTES * 7 // 8


def _cparams(sem, n_in=None, fuse=()):
    fusion = None if n_in is None else [i in fuse for i in range(n_in)]
    return pltpu.CompilerParams(dimension_semantics=sem, vmem_limit_bytes=VMEM_LIMIT, allow_input_fusion=fusion)


def _dot(a, b):
    return jnp.dot(a, b, preferred_element_type=F32)


def _dot_nt(a, b):
    return lax.dot_general(a, b, (((1,), (1,)), ((), ())), preferred_element_type=F32)


def _rms(x, g):
    return x * lax.rsqrt(jnp.mean(x * x, axis=-1, keepdims=True) + EPS) * g


def _silu(x):
    return x * jax.nn.sigmoid(x)


def _rope(x, tab_ref, shift):
    up = pltpu.roll(x, LANES - shift, 1)
    dn = pltpu.roll(x, shift, 1)
    return x * tab_ref[0] + up * tab_ref[1] + dn * tab_ref[2]


def _full(shape):
    nd = len(shape)
    return pl.BlockSpec(shape, lambda *_: (0,) * nd)


def _axial_rope_tables(n, rot_dim):
    rows = n // GRID_W
    row = np.repeat(np.arange(rows), GRID_W).astype(np.float64)
    col = np.tile(np.arange(GRID_W), rows).astype(np.float64)
    nf = rot_dim // 4
    inv = (np.float32(ROPE_BASE) ** (-np.arange(nf, dtype=np.float32) / np.float32(nf))).astype(np.float64)
    ang = np.concatenate([row[:, None] * inv, col[:, None] * inv], axis=-1)
    return np.cos(ang), np.sin(ang)


def _mla_rope_tab(cos, sin):
    n = cos.shape[0]
    z = lambda w: np.zeros((n, w))
    cf = np.concatenate([np.ones((n, MLA_NOPE)), cos, cos, z(32)], axis=-1)
    s1 = np.concatenate([z(MLA_NOPE), -sin, z(16), z(32)], axis=-1)
    s2 = np.concatenate([z(MLA_NOPE), z(16), sin, z(32)], axis=-1)
    return _const(np.stack([cf, s1, s2]))


def _gqa_rope_tab(cos, sin):
    z = np.zeros_like(sin)
    cf = np.concatenate([cos, cos, z, z], axis=-1)
    s1 = np.concatenate([-sin, z, z, z], axis=-1)
    s2 = np.concatenate([z, sin, z, z], axis=-1)
    return _const(np.stack([cf, s1, s2]))


def _const(a):
    return jnp.asarray(np.ascontiguousarray(a, dtype=np.float32))


def _dft_cs(n):
    idx = np.arange(n, dtype=np.int64)
    ang = 2.0 * np.pi * ((idx[:, None] * idx[None, :]) % n).astype(np.float64) / n
    return np.cos(ang), np.sin(ang)


def _mod_kernel(s_ref, w_ref, b_ref, o_ref):
    s = _silu(s_ref[...])
    o_ref[0] = _dot(s.astype(BF16), w_ref[0].astype(BF16)) + b_ref[0]


def _mod_call(svec, w_mod, b_mod):
    depth, d, d3 = w_mod.shape
    tn = 768
    return pl.pallas_call(
        _mod_kernel,
        grid=(depth, d3 // tn),
        in_specs=[
            pl.BlockSpec((8, d), lambda l, j: (0, 0)),
            pl.BlockSpec((1, d, tn), lambda l, j: (l, 0, j)),
            pl.BlockSpec((1, 1, tn), lambda l, j: (l, 0, j)),
        ],
        out_specs=pl.BlockSpec((1, 8, tn), lambda l, j: (l, 0, j)),
        out_shape=jax.ShapeDtypeStruct((depth, 8, d3), F32),
        compiler_params=_cparams(("arbitrary", "arbitrary")),
        name="mod",
    )(svec, w_mod, b_mod.reshape(depth, 1, d3))


def _even_front(x, ng, sc, sh, w_in_ref, qn, w_qbt_ref, kvn, w_k_ref, w_vt_ref, vone, tqt_ref, tk_ref, fc_ref):
    m = x.shape[0]
    h = (_rms(x, ng) * (1.0 + sc) + sh).astype(BF16)
    f_in = _dot(h, w_in_ref[:, E_FIN:E_FGATE])
    f_gate = _dot(h, w_in_ref[:, E_FGATE:E_MGATE])
    m_gate = _dot(h, w_in_ref[:, E_MGATE:E_QA])
    low_rank = _dot(h, w_in_ref[:, E_QA:E_END])
    q_a = low_rank[:, 0:E_KVA - E_QA]
    kv_a = low_rank[:, E_KVA - E_QA:E_KPE - E_QA]
    kpe = low_rank[:, E_KPE - E_QA:E_END - E_QA]
    qh = _rms(q_a, qn).astype(BF16)
    ch = _rms(kv_a, kvn).astype(BF16)
    kpe_r = _rope(kpe, tk_ref, 16)
    qt = _dot_nt(w_qbt_ref[...], qh)
    k_all = _dot(ch, w_k_ref[...])
    cos = tqt_ref[0]
    sin = tqt_ref[1]
    pad = jnp.zeros((LANES - MLA_NOPE - MLA_ROPE, m), F32)
    qts, ks = [], []
    for hd in range(MLA_HEADS):
        lo = hd * LANES
        ql = hd * (MLA_NOPE + MLA_ROPE)
        x1 = qt[ql + MLA_NOPE:ql + MLA_NOPE + 16]
        x2 = qt[ql + MLA_NOPE + 16:ql + MLA_NOPE + 32]
        qts.append(jnp.concatenate(
            [qt[ql:ql + MLA_NOPE] * (MLA_SCALE * LOG2E), x1 * cos - x2 * sin, x2 * cos + x1 * sin, pad], axis=0))
        ks.append(k_all[:, lo:lo + LANES] + kpe_r)
    vt = _dot_nt(w_vt_ref[...], ch) + vone
    zr, zi = [], []
    fb = f_in.astype(BF16)
    for g in range(F_GROUPS):
        z = _dot(fb[:, g * LANES:(g + 1) * LANES], fc_ref[...])
        zr.append(z[:, :LANES])
        zi.append(z[:, LANES:])
    return qts, ks, vt, zr, zi, f_gate, m_gate


def _front0_kernel(x_ref, ng_ref, sc_ref, sh_ref, w_in_ref, qn_ref, w_qbt_ref, kvn_ref, w_k_ref, w_vt_ref,
                   vone_ref, tqt_ref, tk_ref, fc_ref, q_ref, k_ref, v_ref, zr_ref, zi_ref, gf_ref, gm_ref):
    qts, ks, vt, zr, zi, f_gate, m_gate = _even_front(
        x_ref[0], ng_ref[...], sc_ref[0], sh_ref[0], w_in_ref, qn_ref[...], w_qbt_ref, kvn_ref[...],
        w_k_ref, w_vt_ref, vone_ref[...], tqt_ref, tk_ref, fc_ref)
    for hd in range(MLA_HEADS):
        q_ref[0, hd] = qts[hd].astype(BF16)
        k_ref[0, hd] = ks[hd].astype(BF16)
        vh = vt[hd * MLA_VT_ROWS:(hd + 1) * MLA_VT_ROWS].astype(BF16)
        for ck in range(v_ref.shape[2]):
            v_ref[0, hd, ck] = vh[:, ck * MLA_VT_CHUNK:(ck + 1) * MLA_VT_CHUNK]
    for g in range(F_GROUPS):
        zr_ref[0, :, g * LANES:(g + 1) * LANES] = zr[g].astype(BF16)
        zi_ref[0, :, g * LANES:(g + 1) * LANES] = zi[g].astype(BF16)
    gf_ref[0] = _silu(f_gate).astype(BF16)
    gm_ref[0] = _silu(m_gate).astype(BF16)


def _front0_call(x, ng, sc, sh, w_in, qn, w_qbt, kvn, w_k, w_vt, vone, tqt, tk, fc, tm):
    b, n, d = x.shape
    fw = jax.ShapeDtypeStruct((b, n, F_WIDTH), BF16)
    row_spec = pl.BlockSpec((1, tm, F_WIDTH), lambda bi, i: (bi, i, 0))
    vec_spec = pl.BlockSpec((1, 1, d), lambda bi, i: (bi, 0, 0))
    return pl.pallas_call(
        _front0_kernel,
        grid=(b, n // tm),
        in_specs=[
            pl.BlockSpec((1, tm, d), lambda bi, i: (bi, i, 0)),
            _full(ng.shape), vec_spec, vec_spec,
            _full(w_in.shape), _full(qn.shape), _full(w_qbt.shape), _full(kvn.shape), _full(w_k.shape),
            _full(w_vt.shape), _full(vone.shape),
            pl.BlockSpec((2, 16, tm), lambda bi, i: (0, 0, i)),
            pl.BlockSpec((3, tm, LANES), lambda bi, i: (0, i, 0)),
            _full(fc.shape),
        ],
        out_specs=[
            pl.BlockSpec((1, MLA_HEADS, LANES, tm), lambda bi, i: (bi, 0, 0, i)),
            pl.BlockSpec((1, MLA_HEADS, tm, LANES), lambda bi, i: (bi, 0, i, 0)),
            pl.BlockSpec((1, MLA_HEADS, tm // MLA_VT_CHUNK, MLA_VT_ROWS, MLA_VT_CHUNK),
                         lambda bi, i: (bi, 0, i, 0, 0)),
            row_spec, row_spec, row_spec, row_spec],
        out_shape=[
            jax.ShapeDtypeStruct((b, MLA_HEADS, LANES, n), BF16),
            jax.ShapeDtypeStruct((b, MLA_HEADS, n, LANES), BF16),
            jax.ShapeDtypeStruct((b, MLA_HEADS, n // MLA_VT_CHUNK, MLA_VT_ROWS, MLA_VT_CHUNK), BF16),
            fw, fw, fw, fw],
        compiler_params=_cparams(("parallel", "arbitrary")),
        name="front0",
    )(x, ng, sc, sh, w_in, qn, w_qbt, kvn, w_k, w_vt, vone, tqt, tk, fc)


def _zero_after(x):
    u = lax.bitcast_convert_type(x, jnp.uint32)
    z = lax.shift_right_logical(lax.shift_right_logical(u, jnp.uint32(16)), jnp.uint32(16))
    return lax.bitcast_convert_type(z, F32)


def _mla_chunk(qt, k, vt, m, acc):
    s = _dot(k, qt)
    m_new = jnp.maximum(m, jnp.max(s, axis=0, keepdims=True))
    alpha = jnp.exp2(m - m_new)
    p = jnp.exp2(s - m_new).astype(BF16)
    return m_new, alpha * acc + _dot(vt, p)


def _mla_pair_out(acc0, acc1):
    o0 = acc0[0:MLA_V] / acc0[MLA_V:MLA_V + 1]
    o1 = acc1[0:MLA_V] / acc1[MLA_V:MLA_V + 1]
    return jnp.concatenate([o0, o1], axis=0).T


def _ctx_kernel(ctx_scale, x_ref, ng0_ref, sc0_ref, sh0_ref, g0_ref, ng1_ref, sc1_ref, sh1_ref,
                w_in_ref, qn_ref, w_qbt_ref, kvn_ref, w_k_ref, w_vt_ref, vone_ref, tqt_ref, tk_ref, fc_ref,
                fn_ref, w_out_ref, w_k1_ref, w_vt1_ref, vone1_ref,
                kc_ref, vc_ref, ck1_ref, cv1_ref):
    x = x_ref[0]
    c = x.shape[0]
    qts, ks, vt, zr, zi, f_gate, m_gate = _even_front(
        x, ng0_ref[...], sc0_ref[...], sh0_ref[...], w_in_ref, qn_ref[...], w_qbt_ref, kvn_ref[...],
        w_k_ref, w_vt_ref, vone_ref[...], tqt_ref, tk_ref, fc_ref)
    pairs = []
    for hp in range(MLA_HEADS // 2):
        accs = []
        for e in range(2):
            hd = 2 * hp + e
            kb = ks[hd].astype(BF16)
            vtb = vt[hd * MLA_VT_ROWS:(hd + 1) * MLA_VT_ROWS].astype(BF16)
            kc_ref[0, hd] = kb
            vc_ref[0, hd, 0] = vtb
            m0 = jnp.full((1, c), -jnp.inf, F32)
            acc0 = jnp.zeros((MLA_VT_ROWS, c), F32)
            accs.append(_mla_chunk(qts[hd].astype(BF16), kb, vtb, m0, acc0)[1])
        pairs.append(_mla_pair_out(accs[0], accs[1]))
    a = jnp.concatenate(pairs, axis=-1)
    z = jnp.concatenate([jnp.concatenate(zr, axis=-1), jnp.concatenate(zi, axis=-1)], axis=0).astype(BF16)
    fm = _dot(fn_ref[...], z) * ctx_scale
    yf = (fm * _silu(f_gate)).astype(BF16)
    ya = (a * _silu(m_gate)).astype(BF16)
    y = _dot(yf, w_out_ref[0:F_WIDTH, :]) + _dot(ya, w_out_ref[F_WIDTH:, :])
    x1 = x + g0_ref[...] * y
    h1 = (_rms(x1, ng1_ref[...]) * (1.0 + sc1_ref[...]) + sh1_ref[...]).astype(BF16)
    vt1 = _dot_nt(w_vt1_ref[...], h1) + vone1_ref[...]
    for g in range(GQA_KV_HEADS):
        ck1_ref[0, g] = _dot(h1, w_k1_ref[:, g * LANES:(g + 1) * LANES]).astype(BF16)
        cv1_ref[0, g] = vt1[g * GQA_VT_ROWS:(g + 1) * GQA_VT_ROWS].astype(BF16)


def _ctx_call(ctx, vecs0, vecs1, w_in, qn, w_qbt, kvn, w_k, w_vt, vone, tqt, tk, fc, fn, w_out, w_k1, w_vt1,
              vone1):
    b, c, d = ctx.shape
    ng0, sc0, sh0, g0 = vecs0
    ng1, sc1, sh1 = vecs1
    ctx_scale = 1.0 / math.sqrt(c * F_GROUP_DIM)
    consts = [ng0, sc0, sh0, g0, ng1, sc1, sh1, w_in, qn, w_qbt, kvn, w_k, w_vt, vone, tqt, tk, fc, fn, w_out,
              w_k1, w_vt1, vone1]
    return pl.pallas_call(
        functools.partial(_ctx_kernel, ctx_scale),
        grid=(b,),
        in_specs=[pl.BlockSpec((1, c, d), lambda bi: (bi, 0, 0))] + [_full(a.shape) for a in consts],
        out_specs=[pl.BlockSpec((1, MLA_HEADS, c, LANES), lambda bi: (bi, 0, 0, 0)),
                   pl.BlockSpec((1, MLA_HEADS, 1, MLA_VT_ROWS, c), lambda bi: (bi, 0, 0, 0, 0)),
                   pl.BlockSpec((1, GQA_KV_HEADS, c, LANES), lambda bi: (bi, 0, 0, 0)),
                   pl.BlockSpec((1, GQA_KV_HEADS, GQA_VT_ROWS, c), lambda bi: (bi, 0, 0, 0))],
        out_shape=[jax.ShapeDtypeStruct((b, MLA_HEADS, c, LANES), BF16),
                   jax.ShapeDtypeStruct((b, MLA_HEADS, 1, MLA_VT_ROWS, c), BF16),
                   jax.ShapeDtypeStruct((b, GQA_KV_HEADS, c, LANES), BF16),
                   jax.ShapeDtypeStruct((b, GQA_KV_HEADS, GQA_VT_ROWS, c), BF16)],
        compiler_params=_cparams(("arbitrary",)),
        name="ctx",
    )(ctx, *consts)


def _attn0_kernel(qt_ref, kc_ref, vct_ref, kl_ref, vlt_ref, gm_ref, o_ref, s_ref):
    tq = qt_ref.shape[-1]
    nch, _, tk = vlt_ref.shape[2:]
    nc = kc_ref.shape[2]
    n_ctx = nc // PV_KEYS
    n_sub = n_ctx + nch * tk // PV_KEYS
    group = [0] * n_ctx + [1 + j // SOFTMAX_GROUP for j in range(n_sub - n_ctx)]
    first = [t for t in range(n_sub) if t == 0 or group[t] != group[t - 1]]
    last = [t for t in range(n_sub) if t == n_sub - 1 or group[t] != group[t + 1]]

    def keys(e, t):
        if t < n_ctx:
            return kc_ref[0, e, t * PV_KEYS:(t + 1) * PV_KEYS, :]
        lo = (t - n_ctx) * PV_KEYS
        return kl_ref[0, e, lo:lo + PV_KEYS, :]

    def values_t(e, t):
        if t < n_ctx:
            return vct_ref[0, e, 0, :, t * PV_KEYS:(t + 1) * PV_KEYS]
        j, off = divmod((t - n_ctx) * PV_KEYS, tk)
        return vlt_ref[0, e, j, :, off:off + PV_KEYS]

    def slot(i):
        r = i % SCORE_SLOTS
        return slice(r * PV_KEYS, (r + 1) * PV_KEYS)

    tasks = [(qi, t) for qi in range(tq // MLA_TQ) for t in range(n_sub)]
    colmax = []
    for step in range(len(tasks) + PIPE_SKEW):
        if step < len(tasks):
            qi, t = tasks[step]
            cols = slice(qi * MLA_TQ, (qi + 1) * MLA_TQ)
            cm = []
            for e in range(2):
                s = _dot(keys(e, t), qt_ref[0, e, :, cols])
                s_ref[e, slot(step)] = s
                cm.append(jnp.max(s, axis=0, keepdims=True))
            colmax.append(cm)
        if step < PIPE_SKEW:
            continue
        qi, u = tasks[step - PIPE_SKEW]
        if u == 0:
            m = [jnp.full((1, MLA_TQ), -jnp.inf, F32) for _ in range(2)]
            acc = [jnp.zeros((MLA_VT_ROWS, MLA_TQ), F32) for _ in range(2)]
        if u in first:
            members = [qi * n_sub + t for t in range(n_sub) if group[t] == group[u]]
            assert members[-1] <= step
            m_new, alpha = [], []
            for e in range(2):
                mc = functools.reduce(jnp.maximum, [colmax[i][e] for i in members])
                m_new.append(jnp.maximum(m[e], mc))
                alpha.append(jnp.exp2(m[e] - m_new[-1]))
            pv = [None, None]
        gate = min(step - GATE_LAG, len(tasks) - 1)
        for e in range(2):
            m_use = m_new[e] + _zero_after(colmax[gate][e])
            p = jnp.exp2(s_ref[e, slot(step - PIPE_SKEW)] - m_use).astype(BF16)
            d = _dot(values_t(e, u), p)
            pv[e] = d if pv[e] is None else pv[e] + d
        if u in last:
            for e in range(2):
                acc[e] = alpha[e] * acc[e] + pv[e]
                m[e] = m_new[e]
        if u == n_sub - 1:
            rows = slice(qi * MLA_TQ, (qi + 1) * MLA_TQ)
            o_ref[0, rows] = (_mla_pair_out(acc[0], acc[1]) * gm_ref[0, rows].astype(F32)).astype(BF16)


def _attn0_call(qt, kc, vct, kl, vlt, gm, tq):
    b, h, _, n = qt.shape
    c = kc.shape[2]
    nch, _, tk = vlt.shape[2:]
    assert c % PV_KEYS == 0 and tk % PV_KEYS == 0 and SCORE_SLOTS > PIPE_SKEW >= SOFTMAX_GROUP - 1
    assert tq % MLA_TQ == 0
    o_spec = pl.BlockSpec((1, tq, LANES), lambda bi, hp, i: (bi, i, hp))
    return pl.pallas_call(
        _attn0_kernel,
        grid=(b, h // 2, n // tq),
        in_specs=[
            pl.BlockSpec((1, 2, LANES, tq), lambda bi, hp, i: (bi, hp, 0, i)),
            pl.BlockSpec((1, 2, c, LANES), lambda bi, hp, i: (bi, hp, 0, 0)),
            pl.BlockSpec((1, 2, 1, MLA_VT_ROWS, c), lambda bi, hp, i: (bi, hp, 0, 0, 0)),
            pl.BlockSpec((1, 2, n, LANES), lambda bi, hp, i: (bi, hp, 0, 0)),
            pl.BlockSpec((1, 2, nch, MLA_VT_ROWS, tk), lambda bi, hp, i: (bi, hp, 0, 0, 0)),
            o_spec],
        out_specs=o_spec,
        out_shape=jax.ShapeDtypeStruct((b, n, MLA_WIDTH), BF16),
        scratch_shapes=[pltpu.VMEM((2, SCORE_SLOTS * PV_KEYS, MLA_TQ), F32)],
        compiler_params=_cparams(("parallel", "parallel", "arbitrary")),
        name="attn0",
    )(qt, kc, vct, kl, vlt, gm)


def _swap16(p16, blocks):
    ys = [_dot(p16, blk).astype(BF16) for blk in blocks]
    return [jnp.concatenate([y[t * FFT_TILE:(t + 1) * FFT_TILE] for y in ys], axis=0) for t in range(FFT_TILE)]


def _fft_a_kernel(n1, zr_ref, zi_ref, p16_ref, fa_ref, tw_ref, tr_ref, ti_ref):
    rows = n1 * FFT_TILE
    z = jnp.concatenate([zr_ref[0].reshape(rows, F_WIDTH), zi_ref[0].reshape(rows, F_WIDTH)], axis=1)
    grp = FFT_TILE * FFT_TILE
    per_r = _swap16(p16_ref[...], [z[j * grp:(j + 1) * grp] for j in range(rows // grp)])
    for r in range(FFT_TILE):
        zz = per_r[r]
        a = _dot(fa_ref[...], jnp.concatenate([zz[:, :F_WIDTH], zz[:, F_WIDTH:]], axis=0))
        c = tw_ref[0, r]
        s = tw_ref[1, r]
        for g in range(F_GROUPS):
            ar = a[0:n1, g * LANES:(g + 1) * LANES]
            ai = a[n1:2 * n1, g * LANES:(g + 1) * LANES]
            tr_ref[0, r, :, g * LANES:(g + 1) * LANES] = (ar * c + ai * s).astype(BF16)
            ti_ref[0, r, :, g * LANES:(g + 1) * LANES] = (ai * c - ar * s).astype(BF16)


def _fft_a_call(zr, zi, p16, fa, tw):
    b, n1, r, w = zr.shape
    in_spec = pl.BlockSpec((1, n1, FFT_TILE, w), lambda bi, j: (bi, 0, j, 0))
    out_spec = pl.BlockSpec((1, FFT_TILE, n1, w), lambda bi, j: (bi, j, 0, 0))
    out = jax.ShapeDtypeStruct((b, r, n1, w), BF16)
    return pl.pallas_call(
        functools.partial(_fft_a_kernel, n1),
        grid=(b, r // FFT_TILE),
        in_specs=[in_spec, in_spec, _full(p16.shape), _full(fa.shape),
                  pl.BlockSpec((2, FFT_TILE, n1, LANES), lambda bi, j: (0, j, 0, 0))],
        out_specs=[out_spec, out_spec],
        out_shape=[out, out],
        compiler_params=_cparams(("parallel", "arbitrary")),
        name="fft_a",
    )(zr, zi, p16, fa, tw)


def _fft_c_kernel(tr_ref, ti_ref, p16_ref, fcs_ref, gf_ref, o_ref):
    nr = tr_ref.shape[1]
    rows = nr * FFT_TILE
    grp = FFT_TILE * FFT_TILE
    t = jnp.concatenate([tr_ref[0].reshape(rows, F_WIDTH), ti_ref[0].reshape(rows, F_WIDTH)], axis=1)
    per_k = _swap16(p16_ref[...], [t[j * grp:(j + 1) * grp] for j in range(rows // grp)])
    outs = []
    for k in range(FFT_TILE):
        tt = per_k[k]
        rhs = jnp.concatenate([tt[:, :F_WIDTH], tt[:, F_WIDTH:]], axis=0)
        outs.append(_dot(fcs_ref[...], rhs).astype(BF16))
    gf = gf_ref[0].reshape(rows, F_WIDTH)
    for j in range(nr // FFT_TILE):
        blk = jnp.concatenate([o[j * FFT_TILE:(j + 1) * FFT_TILE] for o in outs], axis=0)
        y = _dot(p16_ref[...], blk)
        o_ref[0, j * FFT_TILE:(j + 1) * FFT_TILE] = (
            y * gf[j * grp:(j + 1) * grp].astype(F32)).astype(BF16).reshape(FFT_TILE, FFT_TILE, F_WIDTH)


def _fft_c_call(tr, ti, p16, fcs, gf):
    b, r, n1, w = tr.shape
    spec = pl.BlockSpec((1, r, FFT_TILE, w), lambda bi, i: (bi, 0, i, 0))
    return pl.pallas_call(
        _fft_c_kernel,
        grid=(b, n1 // FFT_TILE),
        in_specs=[spec, spec, _full(p16.shape), _full(fcs.shape), spec],
        out_specs=spec,
        out_shape=jax.ShapeDtypeStruct((b, r, n1, w), BF16),
        compiler_params=_cparams(("parallel", "arbitrary")),
        name="fft_c",
    )(tr, ti, p16, fcs, gf)


def _mid_kernel(fm_ref, ag_ref, x_ref, g0_ref, w_out_ref, ng_ref, sc_ref, sh_ref, w_qt_ref, w_k_ref, w_vt_ref,
                vone_ref, w_g_ref, tqt_ref, tk_ref, x1_ref, q_ref, k_ref, v_ref, sg_ref):
    tm = x_ref.shape[1]
    y = _dot(fm_ref[0], w_out_ref[0:F_WIDTH, :]) + _dot(ag_ref[0], w_out_ref[F_WIDTH:, :])
    x1 = x_ref[0] + g0_ref[0] * y
    x1_ref[0] = x1
    h = (_rms(x1, ng_ref[...]) * (1.0 + sc_ref[0]) + sh_ref[0]).astype(BF16)
    qt = _dot_nt(w_qt_ref[...], h)
    cos = tqt_ref[0]
    sin = tqt_ref[1]
    half = GQA_HEAD_DIM // 2
    for hd in range(GQA_HEADS):
        lo = hd * GQA_HEAD_DIM
        x1r = qt[lo:lo + half]
        x2r = qt[lo + half:lo + GQA_HEAD_DIM]
        qh = jnp.concatenate([x1r * cos - x2r * sin, x2r * cos + x1r * sin], axis=0).astype(BF16)
        g, j = divmod(hd, GQA_GROUP)
        for blk in range(tm // BLOCK):
            q_ref[0, blk, g, :, j * BLOCK:(j + 1) * BLOCK] = qh[:, blk * BLOCK:(blk + 1) * BLOCK]
    vt = _dot_nt(w_vt_ref[...], h) + vone_ref[...]
    k_all = _dot(h, w_k_ref[...])
    for g in range(GQA_KV_HEADS):
        k_ref[0, g] = _rope(k_all[:, g * LANES:(g + 1) * LANES], tk_ref, 32).astype(BF16)
        v_ref[0, g] = vt[g * GQA_VT_ROWS:(g + 1) * GQA_VT_ROWS].astype(BF16)
    sg_ref[0] = _silu(_dot(h, w_g_ref[...])).astype(BF16)


def _mid_call(fm, ag, x, g0, w_out, ng, sc, sh, w_qt, w_k, w_vt, vone, w_g, tqt, tk, tm):
    b, n, d = x.shape
    row = lambda w: pl.BlockSpec((1, tm, w), lambda bi, i: (bi, i, 0))
    vec_spec = pl.BlockSpec((1, 1, d), lambda bi, i: (bi, 0, 0))
    nblk = tm // BLOCK
    return pl.pallas_call(
        _mid_kernel,
        grid=(b, n // tm),
        in_specs=[row(F_WIDTH), row(MLA_WIDTH), row(d), vec_spec, _full(w_out.shape), _full(ng.shape),
                  vec_spec, vec_spec, _full(w_qt.shape), _full(w_k.shape), _full(w_vt.shape), _full(vone.shape),
                  _full(w_g.shape),
                  pl.BlockSpec((2, GQA_HEAD_DIM // 2, tm), lambda bi, i: (0, 0, i)),
                  pl.BlockSpec((3, tm, LANES), lambda bi, i: (0, i, 0))],
        out_specs=[row(d),
                   pl.BlockSpec((1, nblk, GQA_KV_HEADS, GQA_HEAD_DIM, GQA_GROUP * BLOCK),
                                lambda bi, i: (bi, i, 0, 0, 0)),
                   pl.BlockSpec((1, GQA_KV_HEADS, tm, LANES), lambda bi, i: (bi, 0, i, 0)),
                   pl.BlockSpec((1, GQA_KV_HEADS, GQA_VT_ROWS, tm), lambda bi, i: (bi, 0, 0, i)),
                   row(GQA_Q)],
        out_shape=[jax.ShapeDtypeStruct((b, n, d), F32),
                   jax.ShapeDtypeStruct((b, n // BLOCK, GQA_KV_HEADS, GQA_HEAD_DIM, GQA_GROUP * BLOCK), BF16),
                   jax.ShapeDtypeStruct((b, GQA_KV_HEADS, n, LANES), BF16),
                   jax.ShapeDtypeStruct((b, GQA_KV_HEADS, GQA_VT_ROWS, n), BF16),
                   jax.ShapeDtypeStruct((b, n, GQA_Q), BF16)],
        compiler_params=_cparams(("parallel", "arbitrary"), n_in=15, fuse=(8, 12)),
        name="mid",
    )(fm, ag, x, g0, w_out, ng, sc, sh, w_qt, w_k, w_vt, vone, w_g, tqt, tk)


def _band_bias(nc):
    col = np.arange(3 * BLOCK)[:, None]
    q = np.arange(BLOCK)[None, :]
    dist = BLOCK + q - col
    band = np.abs(dist) <= WINDOW
    variants = [band & (col >= BLOCK), band, band & (col < 2 * BLOCK)]
    out = np.zeros((3, nc + 3 * BLOCK, BLOCK), np.float32)
    for v, ok in enumerate(variants):
        out[v, nc:] = np.where(ok, 0.0, NEG_BIG)
    return jnp.asarray(out)


def _attn1_kernel(sink_ref, q_ref, kp_ref, kc_ref, kn_ref, vp_ref, vc_ref, vn_ref, ck_ref, cv_ref, bias_ref,
                  sg_ref, x1_ref, g1_ref, w_out_ref, fg_ref, o_ref, s_ref):
    i = pl.program_id(1)
    last = pl.num_programs(1) - 1
    nc = ck_ref.shape[2]
    sel = []
    for blk in range(A1_BLOCKS):
        v = 1
        if blk == 0:
            v = jnp.where(i == 0, 0, v)
        if blk == A1_BLOCKS - 1:
            v = jnp.where(i == last, 2, v)
        sel.append(v)

    def band_keys(g, blk, j):
        pos = blk + j - 1
        if pos < 0:
            return kp_ref[0, g], vp_ref[0, g]
        if pos >= A1_BLOCKS:
            return kn_ref[0, g], vn_ref[0, g]
        return (kc_ref[0, g, pos * BLOCK:(pos + 1) * BLOCK, :], vc_ref[0, g, :, pos * BLOCK:(pos + 1) * BLOCK])

    def operands(blk, g, sb):
        if sb == 0:
            return ck_ref[0, g][:, :GQA_HEAD_DIM], cv_ref[0, g], None
        if sb == 1:
            (k0, v0), (k1, v1) = band_keys(g, blk, 0), band_keys(g, blk, 1)
            return (jnp.concatenate([k0, k1], axis=0)[:, :GQA_HEAD_DIM], jnp.concatenate([v0, v1], axis=1),
                    bias_ref[sel[blk], nc:nc + 2 * BLOCK])
        k2, v2 = band_keys(g, blk, 2)
        return k2[:, :GQA_HEAD_DIM], v2, bias_ref[sel[blk], nc + 2 * BLOCK:nc + 3 * BLOCK]

    def slot(idx, rows):
        r = idx % A1_SLOTS
        return slice(r * 2 * BLOCK, r * 2 * BLOCK + rows)

    tasks = [(blk, g, sb) for blk in range(A1_BLOCKS) for g in range(GQA_KV_HEADS) for sb in range(3)]
    colmax = []
    chunks = {}
    for step in range(len(tasks) + A1_SKEW):
        if step < len(tasks):
            blk, g, sb = tasks[step]
            kb, _, bias = operands(blk, g, sb)
            s = _dot(kb, q_ref[0, blk, g])
            if bias is not None:
                s = s + jnp.concatenate([bias] * GQA_GROUP, axis=1)
            s_ref[slot(step, s.shape[0])] = s
            colmax.append(jnp.max(s, axis=0, keepdims=True))
        if step < A1_SKEW:
            continue
        idx = step - A1_SKEW
        blk, g, sb = tasks[idx]
        _, vt, _ = operands(blk, g, sb)
        if sb == 0:
            sink = jnp.concatenate(
                [jnp.full((1, BLOCK), sink_ref[GQA_GROUP * g + j] * LOG2E, F32) for j in range(GQA_GROUP)], axis=1)
            m = jnp.maximum(colmax[idx], sink)
            acc = None
        if sb == 1:
            assert idx + 1 <= step
            m_prev = m
            m = functools.reduce(jnp.maximum, [m_prev, colmax[idx], colmax[idx + 1]])
            acc = jnp.exp2(m_prev - m) * acc
        gate = min(step - A1_GATE_LAG, len(tasks) - 1)
        m_use = m + _zero_after(colmax[gate])
        p = jnp.exp2(s_ref[slot(idx, vt.shape[1])] - m_use).astype(BF16)
        d = _dot(vt, p)
        acc = d if acc is None else acc + d
        if sb == 2:
            l = acc[GQA_HEAD_DIM:GQA_HEAD_DIM + 1] + jnp.exp2(sink - m)
            ot = acc[0:GQA_HEAD_DIM] / l
            for p2 in range(2):
                lo = 2 * p2 * BLOCK
                pair = jnp.concatenate([ot[:, lo:lo + BLOCK], ot[:, lo + BLOCK:lo + 2 * BLOCK]], axis=0)
                chunks[(blk, 2 * g + p2)] = pair.T
            if g == GQA_KV_HEADS - 1:
                rows = slice(blk * BLOCK, (blk + 1) * BLOCK)
                o = jnp.concatenate([chunks[(blk, c)] for c in range(GQA_Q // LANES)], axis=-1)
                og = (o * sg_ref[0, rows].astype(F32)).astype(BF16)
                y = _dot(og, w_out_ref[...])
                x2 = x1_ref[0, rows] + g1_ref[0] * y
                o_ref[0, rows] = _rms(x2, fg_ref[...])


def _attn1_call(sink, qt, k, vt, ck, cvt, bias, sg, x1, g1, w_out, fg):
    b, n, d = x1.shape
    nc = ck.shape[2]
    nb = n // BLOCK
    ns = nb // A1_BLOCKS
    assert nb % A1_BLOCKS == 0 and nc == 2 * BLOCK and A1_SLOTS > A1_SKEW >= 2
    tr = A1_BLOCKS * BLOCK
    row = lambda w: pl.BlockSpec((1, tr, w), lambda bi, i: (bi, i, 0))
    prv = lambda i: jnp.maximum(i * A1_BLOCKS - 1, 0)
    nxt = lambda i: jnp.minimum((i + 1) * A1_BLOCKS, nb - 1)
    k_spec = lambda f: pl.BlockSpec((1, GQA_KV_HEADS, BLOCK, LANES), lambda bi, i: (bi, 0, f(i), 0))
    v_spec = lambda f: pl.BlockSpec((1, GQA_KV_HEADS, GQA_VT_ROWS, BLOCK), lambda bi, i: (bi, 0, 0, f(i)))
    vec_spec = pl.BlockSpec((1, 1, d), lambda bi, i: (bi, 0, 0))
    return pl.pallas_call(
        _attn1_kernel,
        grid=(b, ns),
        in_specs=[pl.BlockSpec(memory_space=pltpu.SMEM),
                  pl.BlockSpec((1, A1_BLOCKS, GQA_KV_HEADS, GQA_HEAD_DIM, GQA_GROUP * BLOCK),
                               lambda bi, i: (bi, i, 0, 0, 0)),
                  k_spec(prv), pl.BlockSpec((1, GQA_KV_HEADS, tr, LANES), lambda bi, i: (bi, 0, i, 0)), k_spec(nxt),
                  v_spec(prv), pl.BlockSpec((1, GQA_KV_HEADS, GQA_VT_ROWS, tr), lambda bi, i: (bi, 0, 0, i)),
                  v_spec(nxt),
                  pl.BlockSpec((1, GQA_KV_HEADS, nc, LANES), lambda bi, i: (bi, 0, 0, 0)),
                  pl.BlockSpec((1, GQA_KV_HEADS, GQA_VT_ROWS, nc), lambda bi, i: (bi, 0, 0, 0)),
                  _full(bias.shape), row(GQA_Q), row(d), vec_spec, _full(w_out.shape), _full(fg.shape)],
        out_specs=row(d),
        out_shape=jax.ShapeDtypeStruct((b, n, d), F32),
        scratch_shapes=[pltpu.VMEM((A1_SLOTS * 2 * BLOCK, GQA_GROUP * BLOCK), F32)],
        compiler_params=_cparams(("parallel", "arbitrary"), n_in=16, fuse=(14,)),
        name="attn1",
    )(sink, qt, k, k, k, vt, vt, vt, ck, cvt, bias, sg, x1, g1, w_out, fg)


def _prep_even_weights(w_in, w_qb, w_kvb):
    d = w_in.shape[0]
    f_in, f_gate, q_a, kv_a, k_pe, m_gate = jnp.split(
        w_in, np.cumsum([F_WIDTH, F_WIDTH, MLA_Q_RANK, MLA_KV_RANK, MLA_ROPE]).tolist(), axis=1)
    kpe_blk = jnp.concatenate([jnp.zeros((d, MLA_NOPE), F32), k_pe, jnp.zeros((d, 32), F32)], axis=1)
    w_in_p = jnp.concatenate([f_in, f_gate, m_gate, q_a, kv_a, kpe_blk], axis=1).astype(BF16)
    w_qbt = w_qb.T.astype(BF16)
    kvb = w_kvb.reshape(MLA_KV_RANK, MLA_HEADS, MLA_NOPE + MLA_V)
    w_k = jnp.pad(kvb[:, :, :MLA_NOPE], ((0, 0), (0, 0), (0, 64))).reshape(MLA_KV_RANK, MLA_HEADS * LANES)
    w_vt = jnp.pad(kvb[:, :, MLA_NOPE:], ((0, 0), (0, 0), (0, MLA_VT_ROWS - MLA_V)))
    w_vt = w_vt.reshape(MLA_KV_RANK, MLA_HEADS * MLA_VT_ROWS).T.astype(BF16)
    vone = np.zeros((MLA_HEADS * MLA_VT_ROWS, 1), np.float32)
    vone[MLA_V::MLA_VT_ROWS] = 1.0
    return w_in_p, w_qbt, w_k.astype(BF16), w_vt, jnp.asarray(vone)


def _prep_odd_weights(w_in):
    d = w_in.shape[0]
    w_qt = w_in[:, O_Q:O_K].T.astype(BF16)
    kw = w_in[:, O_K:O_V].reshape(d, GQA_KV_HEADS, GQA_HEAD_DIM)
    w_k = jnp.pad(kw, ((0, 0), (0, 0), (0, LANES - GQA_HEAD_DIM))).reshape(d, GQA_KV_HEADS * LANES).astype(BF16)
    vw = w_in[:, O_V:O_G].reshape(d, GQA_KV_HEADS, GQA_HEAD_DIM)
    w_vt = jnp.pad(vw, ((0, 0), (0, 0), (0, GQA_VT_ROWS - GQA_HEAD_DIM)))
    w_vt = w_vt.reshape(d, GQA_KV_HEADS * GQA_VT_ROWS).T.astype(BF16)
    vone = np.zeros((GQA_KV_HEADS * GQA_VT_ROWS, 1), np.float32)
    vone[GQA_HEAD_DIM::GQA_VT_ROWS] = 1.0
    return w_qt, w_k, w_vt, jnp.asarray(vone), w_in[:, O_G:O_END].astype(BF16)


def _fft_consts(n, c):
    n1 = n // LANES
    c1, s1 = _dft_cs(n1)
    fa = np.block([[c1, s1], [-s1, c1]])
    c2, s2 = _dft_cs(LANES)
    fcs = np.concatenate([c2, s2], axis=1) / math.sqrt(n * F_GROUP_DIM)
    fc = np.concatenate([c2, -s2], axis=1)
    r = np.arange(LANES, dtype=np.int64)[:, None]
    k1 = np.arange(n1, dtype=np.int64)[None, :]
    ang = 2.0 * np.pi * ((r * k1) % n).astype(np.float64) / n
    tw = np.stack([np.cos(ang), np.sin(ang)])
    tw = np.broadcast_to(tw[..., None], (2, LANES, n1, LANES))
    cn, sn = _dft_cs(c)
    fn = np.concatenate([cn, sn], axis=1)
    as32 = lambda a: jnp.asarray(np.ascontiguousarray(a, dtype=np.float32))
    idx = np.arange(FFT_TILE * FFT_TILE)
    p16 = np.zeros((FFT_TILE * FFT_TILE,) * 2)
    p16[(idx % FFT_TILE) * FFT_TILE + idx // FFT_TILE, idx] = 1.0
    return (as32(fa).astype(BF16), as32(fcs).astype(BF16), as32(fc).astype(BF16), as32(tw), as32(fn).astype(BF16),
            as32(p16).astype(BF16))


def _pick_tile(n, pref):
    t = pref
    while n % t:
        t //= 2
    return t


def _tiles(n):
    return dict(front=_pick_tile(n, 1024), mid=_pick_tile(n, 1024), attn0=_pick_tile(n, 2 * MLA_TQ))


def kernel(x, c, ctx, c_ctx, w_mod, b_mod, norm_g, e_w_in, e_q_norm, e_w_qb, e_kv_norm, e_w_kvb, e_w_out,
           o_w_in, o_sink, o_w_out, final_g):
    b, n, d = x.shape
    nc = ctx.shape[1]
    assert d == D_MODEL and n % (LANES * FFT_TILE) == 0 and nc % LANES == 0 and b <= 7

    svec = jnp.zeros((8, d), F32).at[:b].set(c).at[b].set(c_ctx)
    mod = _mod_call(svec, w_mod, b_mod)
    sh, sc, gt = mod[:, :, :d], mod[:, :, d:2 * d], mod[:, :, 2 * d:]
    lat = lambda t, l: t[l, :b].reshape(b, 1, d)
    cvec = lambda t, l: t[l, b].reshape(1, d)
    ng0, ng1 = norm_g[0].reshape(1, d), norm_g[1].reshape(1, d)

    cos_m, sin_m = _axial_rope_tables(n, MLA_ROPE)
    qs = MLA_SCALE * LOG2E
    tqt0 = _const(np.stack([cos_m.T, sin_m.T]) * qs)
    tk0 = _mla_rope_tab(cos_m, sin_m)
    one_c, zero_c = np.ones((nc, MLA_ROPE // 2)), np.zeros((nc, MLA_ROPE // 2))
    tqt0c = _const(np.stack([one_c.T, zero_c.T]) * qs)
    tk0c = _mla_rope_tab(one_c, zero_c)
    cos_g, sin_g = _axial_rope_tables(n, GQA_HEAD_DIM)
    tqt1 = _const(np.stack([cos_g.T, sin_g.T]) * (GQA_SCALE * LOG2E))
    tk1 = _gqa_rope_tab(cos_g, sin_g)
    fa, fcs, fc, tw, fn, p16 = _fft_consts(n, nc)
    bias1 = _band_bias(nc)

    w_in0, w_qbt0, w_k0, w_vt0, vone = _prep_even_weights(e_w_in[0], e_w_qb[0], e_w_kvb[0])
    qn0 = e_q_norm[0].reshape(1, MLA_Q_RANK)
    kvn0 = e_kv_norm[0].reshape(1, MLA_KV_RANK)
    w_out0 = e_w_out[0].astype(BF16)
    w_qt1, w_k1, w_vt1, vone1, w_g1 = _prep_odd_weights(o_w_in[0])
    w_out1 = o_w_out[0].astype(BF16)

    kc, vct, ck1, cvt1 = _ctx_call(
        ctx, (ng0, cvec(sc, 0), cvec(sh, 0), cvec(gt, 0)), (ng1, cvec(sc, 1), cvec(sh, 1)),
        w_in0, qn0, w_qbt0, kvn0, w_k0, w_vt0, vone, tqt0c, tk0c, fc, fn, w_out0, w_k1, w_vt1, vone1)

    tiles = _tiles(n)
    qt0, k0, vt0, zr, zi, gf, gm = _front0_call(
        x, ng0, lat(sc, 0), lat(sh, 0), w_in0, qn0, w_qbt0, kvn0, w_k0, w_vt0, vone, tqt0, tk0, fc,
        tiles["front"])

    ag = _attn0_call(qt0, kc, vct, k0, vt0, gm, tiles["attn0"])

    n1 = n // LANES
    tr, ti = _fft_a_call(zr.reshape(b, n1, LANES, F_WIDTH), zi.reshape(b, n1, LANES, F_WIDTH), p16, fa, tw)
    fmg = _fft_c_call(tr, ti, p16, fcs, gf.reshape(b, LANES, n1, F_WIDTH))
    fmg = fmg.reshape(b, n, F_WIDTH)

    x1, qt1, k1, vt1, sg = _mid_call(fmg, ag, x, lat(gt, 0), w_out0, ng1, lat(sc, 1), lat(sh, 1),
                                     w_qt1, w_k1, w_vt1, vone1, w_g1, tqt1, tk1, tiles["mid"])

    return _attn1_call(o_sink[0], qt1, k1, vt1, ck1, cvt1, bias1, sg, x1, lat(gt, 1), w_out1,
                       final_g.reshape(1, d))
```
